```python
import jax, jax.numpy as jnp
from jax import lax
import numpy as np

D_MODEL = 1024
BATCH = 2
SEQ = 8192
DEPTH = 1

D_MIX = D_MODEL
RG_WIDTH = D_MIX // 2
RG_BLOCKS = 8
RG_BLOCK = RG_WIDTH // RG_BLOCKS
RG_C = 8.0
CONV_W = 4
GDN_HEADS = 4
GDN_DK = 128
GDN_DV = 128
GDN_QK = GDN_HEADS * GDN_DK
GDN_VW = GDN_HEADS * GDN_DV
CHUNK = 64
D_FF = 2816
N_DIR = 2
EPS = 1e-6

OFF_RG_X = 0
OFF_RG_G = OFF_RG_X + RG_WIDTH
OFF_QKV = OFF_RG_G + RG_WIDTH
OFF_Z = OFF_QKV + 2 * GDN_QK + GDN_VW
OFF_BETA = OFF_Z + GDN_VW
OFF_ALPHA = OFF_BETA + N_DIR * GDN_HEADS
D_IN_PROJ = OFF_ALPHA + N_DIR * GDN_HEADS

kernel_name = "hymba_style_rglru_gdn_macaron_encoder"


def rmsnorm(x, g):
    xf = x.astype(jnp.float32)
    y = xf * lax.rsqrt(jnp.mean(xf * xf, axis=-1, keepdims=True) + EPS)
    return (y * g.astype(jnp.float32)).astype(x.dtype)


def l2norm(t):
    return t * lax.rsqrt(jnp.sum(t * t, axis=-1, keepdims=True) + EPS)


def centred_dwconv(x, w):
    left = CONV_W // 2
    return lax.conv_general_dilated(
        x, w[:, None, :].astype(x.dtype), window_strides=(1,),
        padding=[(left, CONV_W - 1 - left)],
        dimension_numbers=("NWC", "WIO", "NWC"),
        feature_group_count=x.shape[-1])


def swiglu_ffn(x, g, w_gate, w_up, w_down):
    h = rmsnorm(x, g)
    return (jax.nn.silu(h @ w_gate) * (h @ w_up)) @ w_down


def linear_scan(a, b, reverse):
    def combine(l, r):
        return (l[0] * r[0], r[0] * l[1] + r[1])
    _, h = lax.associative_scan(combine, (a, b), reverse=reverse, axis=1)
    return h


def rg_lru_bidir(xc, wa, ba, wx, bx, lam):
    B, S, _ = xc.shape
    xb = xc.reshape(B, S, RG_BLOCKS, RG_BLOCK)
    r = jax.nn.sigmoid(jnp.einsum("bsni,dnij->dbsnj", xb, wa.astype(jnp.float32)).reshape(N_DIR, B, S, RG_WIDTH)
                       + ba.astype(jnp.float32)[:, None, None, :])
    i = jax.nn.sigmoid(jnp.einsum("bsni,dnij->dbsnj", xb, wx.astype(jnp.float32)).reshape(N_DIR, B, S, RG_WIDTH)
                       + bx.astype(jnp.float32)[:, None, None, :])
    log_a = -RG_C * r * jax.nn.softplus(-lam.astype(jnp.float32))[:, None, None, :]
    a = jnp.exp(log_a)
    b = jnp.sqrt(-jnp.expm1(2.0 * log_a)) * (i * xc[None])
    h_f = linear_scan(a[0], b[0], reverse=False)
    h_b = linear_scan(a[1], b[1], reverse=True)
    return h_f + h_b


def gdn_chunked(q, k, v, beta, g):
    B, S, H, DK = q.shape
    DV = v.shape[-1]
    N = S // CHUNK

    def chunks(t):
        t = t.reshape((B, N, CHUNK, H) + t.shape[3:])
        return jnp.moveaxis(t, 3, 1)

    q, k, v, beta, g = chunks(q), chunks(k), chunks(v), chunks(beta), chunks(g)
    g_cum = jnp.cumsum(g, axis=-1)
    idx = jnp.arange(CHUNK)
    incl = idx[:, None] >= idx[None, :]
    strict = idx[:, None] > idx[None, :]
    decay = jnp.exp(jnp.where(incl, g_cum[..., :, None] - g_cum[..., None, :], -jnp.inf))
    k_beta = k * beta[..., None]
    v_beta = v * beta[..., None]
    L = jnp.where(strict, jnp.einsum("bhnik,bhnjk->bhnij", k_beta, k) * decay, 0.0)
    eye = jnp.broadcast_to(jnp.eye(CHUNK, dtype=q.dtype), L.shape)
    T = lax.linalg.triangular_solve(L, eye, left_side=True, lower=True, unit_diagonal=True)
    u = jnp.einsum("bhnij,bhnjv->bhniv", T, v_beta)
    w = jnp.einsum("bhnij,bhnjk->bhnik", T, k_beta * jnp.exp(g_cum)[..., None])
    attn = jnp.einsum("bhnik,bhnjk->bhnij", q, k) * decay
    g_last = g_cum[..., -1:]
    q_dec = q * jnp.exp(g_cum)[..., None]
    k_dec = k * jnp.exp(g_last - g_cum)[..., None]
    c_decay = jnp.exp(g_last[..., 0])

    xs = tuple(jnp.moveaxis(t, 2, 0) for t in (w, u, q_dec, k_dec, attn, c_decay))

    def step(state, inp):
        w_n, u_n, qd_n, kd_n, a_n, cd_n = inp
        v_new = u_n - jnp.einsum("bhck,bhkv->bhcv", w_n, state)
        o_n = jnp.einsum("bhck,bhkv->bhcv", qd_n, state) + jnp.einsum("bhij,bhjv->bhiv", a_n, v_new)
        state = state * cd_n[..., None, None] + jnp.einsum("bhck,bhcv->bhkv", kd_n, v_new)
        return state, o_n

    s0 = jnp.zeros((B, H, DK, DV), q.dtype)
    _, o = lax.scan(step, s0, xs)
    return jnp.transpose(o, (1, 0, 3, 2, 4)).reshape(B, S, H, DV)


def hybrid_mixer(h, w_in, w_out, rg_conv_w, rg_conv_b, rg_gate_a_w, rg_gate_a_b,
                 rg_gate_x_w, rg_gate_x_b, rg_lambda, gdn_conv_w, gdn_a_log, gdn_dt_bias, gdn_norm):
    B, S, _ = h.shape
    f32 = jnp.float32
    p = h @ w_in
    x_rg = p[..., OFF_RG_X:OFF_RG_G]
    gate_rg = p[..., OFF_RG_G:OFF_QKV]
    qkv = p[..., OFF_QKV:OFF_Z]
    z = p[..., OFF_Z:OFF_BETA]
    beta_raw = p[..., OFF_BETA:OFF_ALPHA].reshape(B, S, N_DIR, GDN_HEADS)
    alpha_raw = p[..., OFF_ALPHA:D_IN_PROJ].reshape(B, S, N_DIR, GDN_HEADS)

    xc = (centred_dwconv(x_rg, rg_conv_w) + rg_conv_b).astype(f32)
    hr = rg_lru_bidir(xc, rg_gate_a_w, rg_gate_a_b, rg_gate_x_w, rg_gate_x_b, rg_lambda)
    y_rg = (hr * jax.nn.gelu(gate_rg.astype(f32))).astype(h.dtype)

    qkv = jax.nn.silu(centred_dwconv(qkv, gdn_conv_w)).astype(f32)
    q = l2norm(qkv[..., :GDN_QK].reshape(B, S, GDN_HEADS, GDN_DK)) * (GDN_DK ** -0.5)
    k = l2norm(qkv[..., GDN_QK:2 * GDN_QK].reshape(B, S, GDN_HEADS, GDN_DK))
    v = qkv[..., 2 * GDN_QK:].reshape(B, S, GDN_HEADS, GDN_DV)
    beta = jax.nn.sigmoid(beta_raw.astype(f32))
    g = -jnp.exp(gdn_a_log.astype(f32)) * jax.nn.softplus(alpha_raw.astype(f32) + gdn_dt_bias.astype(f32))
    flip = lambda t: jnp.flip(t, axis=1)
    o_f = gdn_chunked(q, k, v, beta[:, :, 0], g[:, :, 0])
    o_b = flip(gdn_chunked(flip(q), flip(k), flip(v), flip(beta[:, :, 1]), flip(g[:, :, 1])))
    o = rmsnorm(o_f + o_b, gdn_norm) * jax.nn.silu(z.astype(f32).reshape(B, S, GDN_HEADS, GDN_DV))
    y_gdn = o.reshape(B, S, GDN_VW).astype(h.dtype)

    return jnp.concatenate([y_rg, y_gdn], axis=-1) @ w_out


def setup_inputs(seed: int = 0) -> dict:
    key = jax.random.key(seed)
    ks = iter(jax.random.split(key, 32))
    nrm = lambda shape, scale: jax.random.normal(next(ks), shape, jnp.float32) * scale
    gain = lambda shape: 1.0 + nrm(shape, 0.01)
    L = DEPTH
    a_c = jax.random.uniform(next(ks), (L, N_DIR, RG_WIDTH), jnp.float32, 0.9, 0.999)
    s = a_c ** (1.0 / RG_C)
    rg_lambda = jnp.log(s) - jnp.log1p(-s)
    gdn_a_log = jnp.log(jax.random.uniform(next(ks), (L, N_DIR, GDN_HEADS), jnp.float32, 1.0, 16.0))
    dt = jnp.exp(jax.random.uniform(next(ks), (L, N_DIR, GDN_HEADS), jnp.float32, np.log(1e-3), np.log(1e-1)))
    gdn_dt_bias = dt + jnp.log(-jnp.expm1(-dt))
    return {
        "x": nrm((BATCH, SEQ, D_MODEL), 1.0),
        "ffn1_norm": gain((L, D_MODEL)),
        "ffn1_w_gate": nrm((L, D_MODEL, D_FF), D_MODEL ** -0.5),
        "ffn1_w_up": nrm((L, D_MODEL, D_FF), D_MODEL ** -0.5),
        "ffn1_w_down": nrm((L, D_FF, D_MODEL), D_FF ** -0.5),
        "mix_norm": gain((L, D_MODEL)),
        "w_in": nrm((L, D_MODEL, D_IN_PROJ), D_MODEL ** -0.5),
        "w_out": nrm((L, D_MIX, D_MODEL), D_MIX ** -0.5),
        "rg_conv_w": nrm((L, CONV_W, RG_WIDTH), CONV_W ** -0.5),
        "rg_conv_b": nrm((L, RG_WIDTH), 0.01),
        "rg_gate_a_w": nrm((L, N_DIR, RG_BLOCKS, RG_BLOCK, RG_BLOCK), RG_BLOCK ** -0.5),
        "rg_gate_a_b": nrm((L, N_DIR, RG_WIDTH), 0.01),
        "rg_gate_x_w": nrm((L, N_DIR, RG_BLOCKS, RG_BLOCK, RG_BLOCK), RG_BLOCK ** -0.5),
        "rg_gate_x_b": nrm((L, N_DIR, RG_WIDTH), 0.01),
        "rg_lambda": rg_lambda,
        "gdn_conv_w": nrm((L, CONV_W, 2 * GDN_QK + GDN_VW), CONV_W ** -0.5),
        "gdn_a_log": gdn_a_log,
        "gdn_dt_bias": gdn_dt_bias,
        "gdn_norm": gain((L, GDN_DV)),
        "ffn2_norm": gain((L, D_MODEL)),
        "ffn2_w_gate": nrm((L, D_MODEL, D_FF), D_MODEL ** -0.5),
        "ffn2_w_up": nrm((L, D_MODEL, D_FF), D_MODEL ** -0.5),
        "ffn2_w_down": nrm((L, D_FF, D_MODEL), D_FF ** -0.5),
        "final_norm": gain((D_MODEL,)),
    }


def reference(x, ffn1_norm, ffn1_w_gate, ffn1_w_up, ffn1_w_down, mix_norm, w_in, w_out,
              rg_conv_w, rg_conv_b, rg_gate_a_w, rg_gate_a_b, rg_gate_x_w, rg_gate_x_b, rg_lambda,
              gdn_conv_w, gdn_a_log, gdn_dt_bias, gdn_norm,
              ffn2_norm, ffn2_w_gate, ffn2_w_up, ffn2_w_down, final_norm):
    for l in range(DEPTH):
        x = x + 0.5 * swiglu_ffn(x, ffn1_norm[l], ffn1_w_gate[l], ffn1_w_up[l], ffn1_w_down[l])
        x = x + hybrid_mixer(rmsnorm(x, mix_norm[l]), w_in[l], w_out[l],
                             rg_conv_w[l], rg_conv_b[l], rg_gate_a_w[l], rg_gate_a_b[l],
                             rg_gate_x_w[l], rg_gate_x_b[l], rg_lambda[l],
                             gdn_conv_w[l], gdn_a_log[l], gdn_dt_bias[l], gdn_norm[l])
        x = x + 0.5 * swiglu_ffn(x, ffn2_norm[l], ffn2_w_gate[l], ffn2_w_up[l], ffn2_w_down[l])
    return rmsnorm(x, final_norm)
```

```python
import functools

import jax
import jax.numpy as jnp
from jax import lax
from jax.experimental import pallas as pl
from jax.experimental.pallas import tpu as pltpu

F32 = jnp.float32
BF16 = jnp.bfloat16

EPS = 1e-6
RG_C = 8.0
RG_BLOCKS = 8
GDN_HEADS = 4
HEAD_DIM = 128
CHUNK = 64
LANES = 128
SUBLANES = 8
VMEM_LIMIT = 56 * 1024 * 1024


def _cparams(sem):
    return pltpu.CompilerParams(dimension_semantics=sem, vmem_limit_bytes=VMEM_LIMIT)


def _rms(x):
    return x * lax.rsqrt(jnp.mean(x * x, axis=-1, keepdims=True) + EPS)


def _sigmoid(x):
    return 1.0 / (1.0 + jnp.exp(-x))


def _softplus(x):
    return jnp.maximum(x, 0.0) + jnp.log(1.0 + jnp.exp(-jnp.abs(x)))


def _ffn_kernel(x_ref, g_ref, wg_ref, wu_ref, wd_ref, fg_ref, o_ref, h_scr, acc_scr, *, final_norm):
    j = pl.program_id(1)

    @pl.when(j == 0)
    def _():
        h_scr[...] = (_rms(x_ref[...]) * g_ref[...]).astype(BF16)
        acc_scr[...] = jnp.zeros_like(acc_scr)

    h = h_scr[...]
    gate = jnp.dot(h, wg_ref[...], preferred_element_type=F32)
    up = jnp.dot(h, wu_ref[...], preferred_element_type=F32)
    act = (gate * _sigmoid(gate) * up).astype(BF16)
    acc_scr[...] += jnp.dot(act, wd_ref[...], preferred_element_type=F32)

    @pl.when(j == pl.num_programs(1) - 1)
    def _():
        y = x_ref[...] + 0.5 * acc_scr[...]
        if final_norm:
            y = _rms(y) * fg_ref[...]
        o_ref[...] = y


def _ffn(x, g, wg, wu, wd, fg, *, final_norm, tm=1024, tf=256):
    m, d = x.shape
    f = wg.shape[1]
    assert m % tm == 0 and f % tf == 0
    return pl.pallas_call(
        functools.partial(_ffn_kernel, final_norm=final_norm),
        grid=(m // tm, f // tf),
        in_specs=[
            pl.BlockSpec((tm, d), lambda i, j: (i, 0)),
            pl.BlockSpec((1, d), lambda i, j: (0, 0)),
            pl.BlockSpec((d, tf), lambda i, j: (0, j)),
            pl.BlockSpec((d, tf), lambda i, j: (0, j)),
            pl.BlockSpec((tf, d), lambda i, j: (j, 0)),
            pl.BlockSpec((1, d), lambda i, j: (0, 0)),
        ],
        out_specs=pl.BlockSpec((tm, d), lambda i, j: (i, 0)),
        out_shape=jax.ShapeDtypeStruct((m, d), F32),
        scratch_shapes=[pltpu.VMEM((tm, d), BF16), pltpu.VMEM((tm, d), F32)],
        compiler_params=_cparams(("parallel", "arbitrary")),
        name="ffn_final" if final_norm else "ffn",
    )(x, g.reshape(1, d), wg, wu, wd, fg.reshape(1, d))


def _inproj_kernel(x_ref, g_ref, w_ref, wba_ref, p_ref, ba_ref, h_scr):
    j = pl.program_id(1)

    @pl.when(j == 0)
    def _():
        h = (_rms(x_ref[...]) * g_ref[...]).astype(BF16)
        h_scr[...] = h
        ba_ref[...] = jnp.dot(h, wba_ref[...], preferred_element_type=F32)

    p_ref[...] = jnp.dot(h_scr[...], w_ref[...], preferred_element_type=F32)


def _in_proj(x, g, w_main, w_ba, *, tm=1024, tn=512):
    m, d = x.shape
    n = w_main.shape[1]
    assert m % tm == 0 and n % tn == 0
    return pl.pallas_call(
        _inproj_kernel,
        grid=(m // tm, n // tn),
        in_specs=[
            pl.BlockSpec((tm, d), lambda i, j: (i, 0)),
            pl.BlockSpec((1, d), lambda i, j: (0, 0)),
            pl.BlockSpec((d, tn), lambda i, j: (0, j)),
            pl.BlockSpec((d, LANES), lambda i, j: (0, 0)),
        ],
        out_specs=[
            pl.BlockSpec((tm, tn), lambda i, j: (i, j)),
            pl.BlockSpec((tm, LANES), lambda i, j: (i, 0)),
        ],
        out_shape=[jax.ShapeDtypeStruct((m, n), F32), jax.ShapeDtypeStruct((m, LANES), F32)],
        scratch_shapes=[pltpu.VMEM((tm, d), BF16)],
        compiler_params=_cparams(("parallel", "arbitrary")),
        name="in_proj",
    )(x, g.reshape(1, d), w_main, w_ba)


def _conv_tile(x_ref, cw, r0, rt, first, last, seq):
    x0 = x_ref[pl.ds(r0, rt), :]
    prev = x_ref[pl.ds(jnp.maximum(r0 - SUBLANES, 0), SUBLANES), :]
    prev = jnp.where(first, 0.0, prev)
    nxt = x_ref[pl.ds(jnp.minimum(r0 + rt, seq - SUBLANES), SUBLANES), :]
    nxt = jnp.where(last, 0.0, nxt)
    w = jnp.concatenate([prev, x0, nxt], axis=0)
    acc = cw[0:1, :] * w[SUBLANES - 2:SUBLANES - 2 + rt]
    acc = acc + cw[1:2, :] * w[SUBLANES - 1:SUBLANES - 1 + rt]
    acc = acc + cw[2:3, :] * x0
    acc = acc + cw[3:4, :] * w[SUBLANES + 1:SUBLANES + 1 + rt]
    return acc


RG_SEGS = SUBLANES
RG_PAD = 8


def _rglru_kernel(x_ref, gate_ref, cw_ref, cb_ref, wg_ref, bg_ref, lam_ref, o_ref,
                  af_scr, bf_scr, ab_scr, bb_scr, *, seq, rt):
    seg_len = seq // RG_SEGS
    pitch = seg_len + RG_PAD
    tiles_per_seg = seg_len // rt
    n_tiles = seq // rt
    a_scr = (af_scr, ab_scr)
    b_scr = (bf_scr, bb_scr)

    cw = cw_ref[...]
    cb = cb_ref[...]
    bg = bg_ref[...]
    neg_c_sp = -RG_C * _softplus(-lam_ref[...])

    def gates(i, _):
        r0 = pl.multiple_of(i * rt, rt)
        xc = _conv_tile(x_ref, cw, r0, rt, i == 0, i == n_tiles - 1, seq) + cb
        pre = jnp.dot(xc.astype(BF16), wg_ref[...], preferred_element_type=F32) + bg
        seg = i // tiles_per_seg
        off = pl.multiple_of(seg * pitch + (i - seg * tiles_per_seg) * rt, SUBLANES)
        for d in range(2):
            r = _sigmoid(pre[:, (2 * d) * LANES:(2 * d + 1) * LANES])
            ig = _sigmoid(pre[:, (2 * d + 1) * LANES:(2 * d + 2) * LANES])
            log_a = r * neg_c_sp[d:d + 1, :]
            a = jnp.exp(log_a)
            b = jnp.sqrt(1.0 - a * a) * (ig * xc)
            a_scr[d][pl.ds(off, rt), :] = a
            b_scr[d][pl.ds(off, rt), :] = b
        return 0

    lax.fori_loop(0, n_tiles, gates, 0)

    def scan(t, carry):
        hf, pf, hb, pb = carry
        idx = pl.ds(t, RG_SEGS, stride=pitch)
        a = af_scr[idx, :]
        hf = a * hf + bf_scr[idx, :]
        pf = a * pf
        bf_scr[idx, :] = hf
        af_scr[idx, :] = pf
        idx = pl.ds(seg_len - 1 - t, RG_SEGS, stride=pitch)
        a = ab_scr[idx, :]
        hb = a * hb + bb_scr[idx, :]
        pb = a * pb
        bb_scr[idx, :] = hb
        ab_scr[idx, :] = pb
        return hf, pf, hb, pb

    zeros = jnp.zeros((RG_SEGS, LANES), F32)
    ones = jnp.ones((RG_SEGS, LANES), F32)
    hf, pf, hb, pb = lax.fori_loop(0, seg_len, scan, (zeros, ones, zeros, ones), unroll=8)

    cf = [jnp.zeros((1, LANES), F32)]
    for s in range(1, RG_SEGS):
        cf.append(hf[s - 1:s, :] + pf[s - 1:s, :] * cf[s - 1])
    cbk = [None] * RG_SEGS
    cbk[RG_SEGS - 1] = jnp.zeros((1, LANES), F32)
    for s in range(RG_SEGS - 2, -1, -1):
        cbk[s] = hb[s + 1:s + 2, :] + pb[s + 1:s + 2, :] * cbk[s + 1]

    for s in range(RG_SEGS):
        def fix(k, _, s=s):
            off = pl.multiple_of(s * pitch + k * rt, SUBLANES)
            rows = pl.multiple_of(s * seg_len + k * rt, SUBLANES)
            h = bf_scr[pl.ds(off, rt), :] + af_scr[pl.ds(off, rt), :] * cf[s]
            h = h + (bb_scr[pl.ds(off, rt), :] + ab_scr[pl.ds(off, rt), :] * cbk[s])
            o_ref[pl.ds(rows, rt), :] = h * jax.nn.gelu(gate_ref[pl.ds(rows, rt), :], approximate=True)
            return 0

        lax.fori_loop(0, tiles_per_seg, fix, 0)


def _rglru(p, conv_w, conv_b, w_gates, b_gates, lam, *, batch, seq, gate_col0, rt=512):
    ngrp = conv_w.shape[1] // LANES
    seg_len = seq // RG_SEGS
    rt = min(rt, seg_len)
    assert seq % RG_SEGS == 0 and seg_len % rt == 0
    scr = pltpu.VMEM((RG_SEGS * (seg_len + RG_PAD), LANES), F32)
    return pl.pallas_call(
        functools.partial(_rglru_kernel, seq=seq, rt=rt),
        grid=(batch, ngrp),
        in_specs=[
            pl.BlockSpec((seq, LANES), lambda b, c: (b, c)),
            pl.BlockSpec((seq, LANES), lambda b, c: (b, gate_col0 + c)),
            pl.BlockSpec((4, LANES), lambda b, c: (0, c)),
            pl.BlockSpec((1, LANES), lambda b, c: (0, c)),
            pl.BlockSpec((None, LANES, 4 * LANES), lambda b, c: (c, 0, 0)),
            pl.BlockSpec((None, 1, 4 * LANES), lambda b, c: (c, 0, 0)),
            pl.BlockSpec((2, LANES), lambda b, c: (0, c)),
        ],
        out_specs=pl.BlockSpec((seq, LANES), lambda b, c: (b, c)),
        out_shape=jax.ShapeDtypeStruct((batch * seq, ngrp * LANES), F32),
        scratch_shapes=[scr, scr, scr, scr],
        compiler_params=_cparams(("parallel", "parallel")),
        name="rglru",
    )(p, p, conv_w, conv_b.reshape(1, -1), w_gates, b_gates, lam)


def _rg_gate_weights(wa, ba, wx, bx):
    ndir, nblk, blk, _ = wa.shape
    ngrp = nblk * blk // LANES
    per = LANES // blk

    def bd(w):
        w = w.reshape(ndir, ngrp, per, blk, blk)
        eye = jnp.eye(per, dtype=w.dtype)
        return jnp.einsum("dgpij,pq->dgpiqj", w, eye).reshape(ndir, ngrp, LANES, LANES)

    a, x = bd(wa), bd(wx)
    w = jnp.concatenate([a[0], x[0], a[1], x[1]], axis=-1)
    ba = ba.reshape(ndir, ngrp, 1, LANES)
    bx = bx.reshape(ndir, ngrp, 1, LANES)
    b = jnp.concatenate([ba[0], bx[0], ba[1], bx[1]], axis=-1)
    return w.astype(BF16), b.astype(F32)


def _qkv_kernel(x_ref, cw_ref, o_ref, *, seq, rt, n_qk, n_q, q_scale):
    c = pl.program_id(1)
    cw = cw_ref[...]
    n_tiles = seq // rt
    is_qk = c < n_qk
    scale = jnp.where(c < n_q, q_scale, 1.0).astype(F32)

    def body(i, _):
        r0 = pl.multiple_of(i * rt, rt)
        y = _conv_tile(x_ref, cw, r0, rt, i == 0, i == n_tiles - 1, seq)
        y = y * _sigmoid(y)
        nrm = y * (lax.rsqrt(jnp.sum(y * y, axis=-1, keepdims=True) + EPS) * scale)
        o_ref[pl.ds(r0, rt), :] = jnp.where(is_qk, nrm, y)
        return 0

    lax.fori_loop(0, n_tiles, body, 0)


def _qkv_prep(p, conv_w, *, batch, seq, col0, rt=512):
    ngrp = conv_w.shape[1] // LANES
    return pl.pallas_call(
        functools.partial(_qkv_kernel, seq=seq, rt=rt, n_qk=2 * GDN_HEADS, n_q=GDN_HEADS,
                          q_scale=HEAD_DIM ** -0.5),
        grid=(batch, ngrp),
        in_specs=[
            pl.BlockSpec((seq, LANES), lambda b, c: (b, col0 + c)),
            pl.BlockSpec((4, LANES), lambda b, c: (0, c)),
        ],
        out_specs=pl.BlockSpec((seq, LANES), lambda b, c: (b, c)),
        out_shape=jax.ShapeDtypeStruct((batch * seq, ngrp * LANES), F32),
        compiler_params=_cparams(("parallel", "parallel")),
        name="qkv_prep",
    )(p, conv_w)


def _gate_kernel(ba_ref, alog_ref, dtb_ref, col_ref, row_ref, *, tm):
    nh = GDN_HEADS
    ri = lax.broadcasted_iota(jnp.int32, (LANES, LANES), 0)
    ci = lax.broadcasted_iota(jnp.int32, (LANES, LANES), 1)
    same = (ri // CHUNK) == (ci // CHUNK)
    lower = jnp.where(same & (ri >= ci), 1.0, 0.0).astype(F32)
    upper = jnp.where(same & (ri <= ci), 1.0, 0.0).astype(F32)
    lane = ci
    neg_a = -jnp.exp(alog_ref[...])
    dtb = dtb_ref[...]
    for k in range(tm // LANES):
        sl = slice(k * LANES, (k + 1) * LANES)
        raw = ba_ref[sl, :]
        beta = _sigmoid(raw)
        gk = neg_a * _softplus(raw + dtb)
        gk = jnp.where((lane >= 2 * nh) & (lane < 4 * nh), gk, 0.0)
        cf = jnp.dot(lower, gk, preferred_element_type=F32, precision=lax.Precision.HIGHEST)
        cb = jnp.dot(upper, gk, preferred_element_type=F32, precision=lax.Precision.HIGHEST)
        gc = jnp.where(lane < 3 * nh, cf, cb)
        col = jnp.where(lane < 2 * nh, beta, gc)
        col_ref[sl, :] = col
        colt = col.T
        for q in range(LANES // CHUNK):
            row_ref[k * (LANES // CHUNK) + q] = colt[0:4 * nh, q * CHUNK:(q + 1) * CHUNK]


def _gate_prep(ba, a_log, dt_bias, *, tm=512):
    m = ba.shape[0]
    nh = GDN_HEADS
    pad = lambda v: jnp.zeros((1, LANES), F32).at[0, 2 * nh:4 * nh].set(v.reshape(-1).astype(F32))
    return pl.pallas_call(
        functools.partial(_gate_kernel, tm=tm),
        grid=(m // tm,),
        in_specs=[
            pl.BlockSpec((tm, LANES), lambda i: (i, 0)),
            pl.BlockSpec((1, LANES), lambda i: (0, 0)),
            pl.BlockSpec((1, LANES), lambda i: (0, 0)),
        ],
        out_specs=[
            pl.BlockSpec((tm, LANES), lambda i: (i, 0)),
            pl.BlockSpec((tm // CHUNK, 4 * nh, CHUNK), lambda i: (i, 0, 0)),
        ],
        out_shape=[jax.ShapeDtypeStruct((m, LANES), F32),
                   jax.ShapeDtypeStruct((m // CHUNK, 4 * nh, CHUNK), F32)],
        compiler_params=_cparams(("parallel",)),
        name="gate_prep",
    )(ba, pad(a_log), pad(dt_bias))


def _unit_tri_inverse(lm, ri, ci):
    eye = jnp.where(ri == ci, 1.0, 0.0).astype(F32)
    x = eye - jnp.where((ri // 2) == (ci // 2), lm, 0.0)
    s = 2
    while s < CHUNK:
        c = jnp.where(((ri // (2 * s)) == (ci // (2 * s))) & ((ri // s) != (ci // s)), lm, 0.0)
        xc = jnp.dot(x, c, preferred_element_type=F32)
        x = x - jnp.dot(xc, x, preferred_element_type=F32)
        s *= 2
    return x


def _gdn_kernel(q_ref, k_ref, v_ref, col_ref, row_ref, o_ref, s_scr, *, reverse, n_chunks, d):
    nh = GDN_HEADS
    i = pl.program_id(1)

    @pl.when(i == 0)
    def _():
        s_scr[...] = jnp.zeros_like(s_scr)

    ri = lax.broadcasted_iota(jnp.int32, (CHUNK, CHUNK), 0)
    ci = lax.broadcasted_iota(jnp.int32, (CHUNK, CHUNK), 1)
    if reverse:
        incl, strict, last = ri <= ci, ri < ci, 0
    else:
        incl, strict, last = ri >= ci, ri > ci, CHUNK - 1

    def chunk(c, _):
        cc = (n_chunks - 1 - c) if reverse else c
        r0 = pl.multiple_of(cc * CHUNK, CHUNK)
        col = col_ref[pl.ds(r0, CHUNK), :]
        row = row_ref[cc]
        for h in range(nh):
            li = d * nh + h
            hs = slice(h * HEAD_DIM, (h + 1) * HEAD_DIM)
            q = q_ref[pl.ds(r0, CHUNK), hs]
            k = k_ref[pl.ds(r0, CHUNK), hs]
            v = v_ref[pl.ds(r0, CHUNK), hs]
            beta = jnp.broadcast_to(col[:, li:li + 1], (CHUNK, HEAD_DIM))
            gc = jnp.broadcast_to(col[:, 2 * nh + li:2 * nh + li + 1], (CHUNK, HEAD_DIM))
            gc_row = jnp.broadcast_to(row[2 * nh + li:2 * nh + li + 1, :], (CHUNK, CHUNK))
            g_last = gc[last:last + 1, :]
            eg = jnp.exp(gc)
            decay = jnp.where(incl, jnp.exp(gc[:, :CHUNK] - gc_row), 0.0)
            k_beta = k * beta
            v_beta = v * beta
            kk = lax.dot_general(k_beta, k, (((1,), (1,)), ((), ())), preferred_element_type=F32)
            lm = jnp.where(strict, kk * decay, 0.0)
            t = _unit_tri_inverse(lm, ri, ci)
            u = jnp.dot(t, v_beta, preferred_element_type=F32)
            w = jnp.dot(t, k_beta * eg, preferred_element_type=F32)
            attn = lax.dot_general(q, k, (((1,), (1,)), ((), ())), preferred_element_type=F32) * decay
            q_dec = q * eg
            k_dec = k * jnp.exp(g_last - gc)
            state = s_scr[h]
            v_new = u - jnp.dot(w, state, preferred_element_type=F32)
            o = jnp.dot(q_dec, state, preferred_element_type=F32)
            o = o + jnp.dot(attn, v_new, preferred_element_type=F32)
            o_ref[pl.ds(r0, CHUNK), hs] = o
            upd = lax.dot_general(k_dec, v_new, (((0,), (0,)), ((), ())), preferred_element_type=F32)
            s_scr[h] = state * jnp.exp(g_last) + upd
        return 0

    lax.fori_loop(0, n_chunks, chunk, 0)


def _gdn(qkv, col, row, *, batch, seq, d, tt=512):
    nh = GDN_HEADS
    width = nh * HEAD_DIM
    nt = seq // tt
    n_chunks = tt // CHUNK
    reverse = d == 1
    tile = (lambda i: nt - 1 - i) if reverse else (lambda i: i)
    return pl.pallas_call(
        functools.partial(_gdn_kernel, reverse=reverse, n_chunks=n_chunks, d=d),
        grid=(batch, nt),
        in_specs=[
            pl.BlockSpec((tt, width), lambda b, i: (b * nt + tile(i), 0)),
            pl.BlockSpec((tt, width), lambda b, i: (b * nt + tile(i), 1)),
            pl.BlockSpec((tt, width), lambda b, i: (b * nt + tile(i), 2)),
            pl.BlockSpec((tt, LANES), lambda b, i: (b * nt + tile(i), 0)),
            pl.BlockSpec((n_chunks, 4 * nh, CHUNK), lambda b, i: (b * nt + tile(i), 0, 0)),
        ],
        out_specs=pl.BlockSpec((tt, width), lambda b, i: (b * nt + tile(i), 0)),
        out_shape=jax.ShapeDtypeStruct((batch * seq, width), F32),
        scratch_shapes=[pltpu.VMEM((nh, HEAD_DIM, HEAD_DIM), F32)],
        compiler_params=_cparams(("parallel", "arbitrary")),
        name="gdn_bwd" if reverse else "gdn_fwd",
    )(qkv, qkv, qkv, col, row)


def _outproj_kernel(yrg_ref, of_ref, ob_ref, z_ref, gn_ref, w_ref, x_ref, o_ref):
    o = of_ref[...] + ob_ref[...]
    z = z_ref[...]
    gn = gn_ref[...]
    parts = [yrg_ref[...].astype(BF16)]
    for h in range(GDN_HEADS):
        hs = slice(h * HEAD_DIM, (h + 1) * HEAD_DIM)
        zh = z[:, hs]
        parts.append((_rms(o[:, hs]) * gn * (zh * _sigmoid(zh))).astype(BF16))
    y = jnp.concatenate(parts, axis=-1)
    o_ref[...] = x_ref[...] + jnp.dot(y, w_ref[...], preferred_element_type=F32)


def _out_proj(y_rg, o_f, o_b, p, gdn_norm, w_out, x, *, z_col, tm=512):
    m, d = x.shape
    wr = y_rg.shape[1]
    wg = o_f.shape[1]
    return pl.pallas_call(
        _outproj_kernel,
        grid=(m // tm,),
        in_specs=[
            pl.BlockSpec((tm, wr), lambda i: (i, 0)),
            pl.BlockSpec((tm, wg), lambda i: (i, 0)),
            pl.BlockSpec((tm, wg), lambda i: (i, 0)),
            pl.BlockSpec((tm, wg), lambda i: (i, z_col)),
            pl.BlockSpec((1, HEAD_DIM), lambda i: (0, 0)),
            pl.BlockSpec((wr + wg, d), lambda i: (0, 0)),
            pl.BlockSpec((tm, d), lambda i: (i, 0)),
        ],
        out_specs=pl.BlockSpec((tm, d), lambda i: (i, 0)),
        out_shape=jax.ShapeDtypeStruct((m, d), F32),
        compiler_params=_cparams(("parallel",)),
        name="out_proj",
    )(y_rg, o_f, o_b, p, gdn_norm.reshape(1, -1), w_out, x)


def _mixer(x1, batch, seq, mix_norm, w_in, w_out, rg_conv_w, rg_conv_b, rg_gate_a_w, rg_gate_a_b,
           rg_gate_x_w, rg_gate_x_b, rg_lambda, gdn_conv_w, gdn_a_log, gdn_dt_bias, gdn_norm):
    rg_w = rg_conv_w.shape[1]
    qkv_w = gdn_conv_w.shape[1]
    gdn_vw = GDN_HEADS * HEAD_DIM
    n_main = 2 * rg_w + qkv_w + gdn_vw
    n_gate = w_in.shape[1] - n_main
    w_main = w_in[:, :n_main].astype(BF16)
    w_ba = jnp.zeros((w_in.shape[0], LANES), F32).at[:, :n_gate].set(w_in[:, n_main:]).astype(BF16)
    p, ba = _in_proj(x1, mix_norm, w_main, w_ba)

    w_gates, b_gates = _rg_gate_weights(rg_gate_a_w, rg_gate_a_b, rg_gate_x_w, rg_gate_x_b)
    y_rg = _rglru(p, rg_conv_w, rg_conv_b, w_gates, b_gates, rg_lambda.astype(F32),
                  batch=batch, seq=seq, gate_col0=rg_w // LANES)

    qkv = _qkv_prep(p, gdn_conv_w, batch=batch, seq=seq, col0=2 * rg_w // LANES)
    col, row = _gate_prep(ba, gdn_a_log, gdn_dt_bias)
    o_f = _gdn(qkv, col, row, batch=batch, seq=seq, d=0)
    o_b = _gdn(qkv, col, row, batch=batch, seq=seq, d=1)
    return _out_proj(y_rg, o_f, o_b, p, gdn_norm, w_out.astype(BF16), x1,
                     z_col=(2 * rg_w + qkv_w) // gdn_vw)


def kernel(x, ffn1_norm, ffn1_w_gate, ffn1_w_up, ffn1_w_down, mix_norm, w_in, w_out, rg_conv_w, rg_conv_b, rg_gate_a_w, rg_gate_a_b, rg_gate_x_w, rg_gate_x_b, rg_lambda, gdn_conv_w, gdn_a_log, gdn_dt_bias, gdn_norm, ffn2_norm, ffn2_w_gate, ffn2_w_up, ffn2_w_down, final_norm):
    batch, seq, d_model = x.shape
    depth = ffn1_norm.shape[0]
    h = x.reshape(batch * seq, d_model)
    for l in range(depth):
        last = l == depth - 1
        h = _ffn(h, ffn1_norm[l], ffn1_w_gate[l].astype(BF16), ffn1_w_up[l].astype(BF16),
                 ffn1_w_down[l].astype(BF16), final_norm, final_norm=False)
        h = _mixer(h, batch, seq, mix_norm[l], w_in[l], w_out[l], rg_conv_w[l], rg_conv_b[l],
                   rg_gate_a_w[l], rg_gate_a_b[l], rg_gate_x_w[l], rg_gate_x_b[l], rg_lambda[l],
                   gdn_conv_w[l], gdn_a_log[l], gdn_dt_bias[l], gdn_norm[l])
        h = _ffn(h, ffn2_norm[l], ffn2_w_gate[l].astype(BF16), ffn2_w_up[l].astype(BF16),
                 ffn2_w_down[l].astype(BF16), final_norm, final_norm=last)
    return h.reshape(batch, seq, d_model)
```

```python
import functools

import jax
import jax.numpy as jnp
from jax import lax
from jax.experimental import pallas as pl
from jax.experimental.pallas import tpu as pltpu

F32 = jnp.float32
BF16 = jnp.bfloat16

EPS = 1e-6
RG_C = 8.0
RG_BLOCKS = 8
GDN_HEADS = 4
HEAD_DIM = 128
CHUNK = 64
LANES = 128
SUBLANES = 8
VMEM_LIMIT = 56 * 1024 * 1024


def _cparams(sem):
    return pltpu.CompilerParams(dimension_semantics=sem, vmem_limit_bytes=VMEM_LIMIT)


def _rms(x):
    return x * lax.rsqrt(jnp.mean(x * x, axis=-1, keepdims=True) + EPS)


def _sigmoid(x):
    return 1.0 / (1.0 + jnp.exp(-x))


def _softplus(x):
    return jnp.maximum(x, 0.0) + jnp.log(1.0 + jnp.exp(-jnp.abs(x)))


def _ffn_kernel(x_ref, g_ref, wg_ref, wu_ref, wd_ref, fg_ref, o_ref, h_scr, acc_scr, *, final_norm):
    j = pl.program_id(1)

    @pl.when(j == 0)
    def _():
        h_scr[...] = (_rms(x_ref[...]) * g_ref[...]).astype(BF16)
        acc_scr[...] = jnp.zeros_like(acc_scr)

    h = h_scr[...]
    gate = jnp.dot(h, wg_ref[...], preferred_element_type=F32)
    up = jnp.dot(h, wu_ref[...], preferred_element_type=F32)
    act = (gate * _sigmoid(gate) * up).astype(BF16)
    acc_scr[...] += jnp.dot(act, wd_ref[...], preferred_element_type=F32)

    @pl.when(j == pl.num_programs(1) - 1)
    def _():
        y = x_ref[...] + 0.5 * acc_scr[...]
        if final_norm:
            y = _rms(y) * fg_ref[...]
        o_ref[...] = y


def _ffn(x, g, wg, wu, wd, fg, *, final_norm, tm=1024, tf=256):
    m, d = x.shape
    f = wg.shape[1]
    assert m % tm == 0 and f % tf == 0
    return pl.pallas_call(
        functools.partial(_ffn_kernel, final_norm=final_norm),
        grid=(m // tm, f // tf),
        in_specs=[
            pl.BlockSpec((tm, d), lambda i, j: (i, 0)),
            pl.BlockSpec((1, d), lambda i, j: (0, 0)),
            pl.BlockSpec((d, tf), lambda i, j: (0, j)),
            pl.BlockSpec((d, tf), lambda i, j: (0, j)),
            pl.BlockSpec((tf, d), lambda i, j: (j, 0)),
            pl.BlockSpec((1, d), lambda i, j: (0, 0)),
        ],
        out_specs=pl.BlockSpec((tm, d), lambda i, j: (i, 0)),
        out_shape=jax.ShapeDtypeStruct((m, d), F32),
        scratch_shapes=[pltpu.VMEM((tm, d), BF16), pltpu.VMEM((tm, d), F32)],
        compiler_params=_cparams(("parallel", "arbitrary")),
        name="ffn_final" if final_norm else "ffn",
    )(x, g.reshape(1, d), wg, wu, wd, fg.reshape(1, d))


def _inproj_kernel(x_ref, g_ref, w_ref, wba_ref, p_ref, ba_ref, h_scr):
    j = pl.program_id(1)

    @pl.when(j == 0)
    def _():
        h = (_rms(x_ref[...]) * g_ref[...]).astype(BF16)
        h_scr[...] = h
        ba_ref[...] = jnp.dot(h, wba_ref[...], preferred_element_type=F32)

    p_ref[...] = jnp.dot(h_scr[...], w_ref[...], preferred_element_type=F32)


def _in_proj(x, g, w_main, w_ba, *, tm=1024, tn=512):
    m, d = x.shape
    n = w_main.shape[1]
    assert m % tm == 0 and n % tn == 0
    return pl.pallas_call(
        _inproj_kernel,
        grid=(m // tm, n // tn),
        in_specs=[
            pl.BlockSpec((tm, d), lambda i, j: (i, 0)),
            pl.BlockSpec((1, d), lambda i, j: (0, 0)),
            pl.BlockSpec((d, tn), lambda i, j: (0, j)),
            pl.BlockSpec((d, LANES), lambda i, j: (0, 0)),
        ],
        out_specs=[
            pl.BlockSpec((tm, tn), lambda i, j: (i, j)),
            pl.BlockSpec((tm, LANES), lambda i, j: (i, 0)),
        ],
        out_shape=[jax.ShapeDtypeStruct((m, n), F32), jax.ShapeDtypeStruct((m, LANES), F32)],
        scratch_shapes=[pltpu.VMEM((tm, d), BF16)],
        compiler_params=_cparams(("parallel", "arbitrary")),
        name="in_proj",
    )(x, g.reshape(1, d), w_main, w_ba)


def _conv_tile(x_ref, cw, r0, rt, first, last, seq):
    x0 = x_ref[pl.ds(r0, rt), :]
    prev = x_ref[pl.ds(jnp.maximum(r0 - SUBLANES, 0), SUBLANES), :]
    prev = jnp.where(first, 0.0, prev)
    nxt = x_ref[pl.ds(jnp.minimum(r0 + rt, seq - SUBLANES), SUBLANES), :]
    nxt = jnp.where(last, 0.0, nxt)
    w = jnp.concatenate([prev, x0, nxt], axis=0)
    acc = cw[0:1, :] * w[SUBLANES - 2:SUBLANES - 2 + rt]
    acc = acc + cw[1:2, :] * w[SUBLANES - 1:SUBLANES - 1 + rt]
    acc = acc + cw[2:3, :] * x0
    acc = acc + cw[3:4, :] * w[SUBLANES + 1:SUBLANES + 1 + rt]
    return acc


RG_SEGS = SUBLANES
RG_PAD = 8


def _rglru_kernel(x_ref, gate_ref, cw_ref, cb_ref, wg_ref, bg_ref, lam_ref, o_ref,
                  af_scr, bf_scr, ab_scr, bb_scr, *, seq, rt):
    seg_len = seq // RG_SEGS
    pitch = seg_len + RG_PAD
    tiles_per_seg = seg_len // rt
    n_tiles = seq // rt
    a_scr = (af_scr, ab_scr)
    b_scr = (bf_scr, bb_scr)

    cw = cw_ref[...]
    cb = cb_ref[...]
    bg = bg_ref[...]
    neg_c_sp = -RG_C * _softplus(-lam_ref[...])

    def gates(i, _):
        r0 = pl.multiple_of(i * rt, rt)
        xc = _conv_tile(x_ref, cw, r0, rt, i == 0, i == n_tiles - 1, seq) + cb
        pre = jnp.dot(xc.astype(BF16), wg_ref[...], preferred_element_type=F32) + bg
        seg = i // tiles_per_seg
        off = pl.multiple_of(seg * pitch + (i - seg * tiles_per_seg) * rt, SUBLANES)
        for d in range(2):
            r = _sigmoid(pre[:, (2 * d) * LANES:(2 * d + 1) * LANES])
            ig = _sigmoid(pre[:, (2 * d + 1) * LANES:(2 * d + 2) * LANES])
            log_a = r * neg_c_sp[d:d + 1, :]
            a = jnp.exp(log_a)
            b = jnp.sqrt(1.0 - a * a) * (ig * xc)
            a_scr[d][pl.ds(off, rt), :] = a
            b_scr[d][pl.ds(off, rt), :] = b
        return 0

    lax.fori_loop(0, n_tiles, gates, 0)

    def scan(t, carry):
        hf, pf, hb, pb = carry
        idx = pl.ds(t, RG_SEGS, stride=pitch)
        a = af_scr[idx, :]
        hf = a * hf + bf_scr[idx, :]
        pf = a * pf
        bf_scr[idx, :] = hf
        af_scr[idx, :] = pf
        idx = pl.ds(seg_len - 1 - t, RG_SEGS, stride=pitch)
        a = ab_scr[idx, :]
        hb = a * hb + bb_scr[idx, :]
        pb = a * pb
        bb_scr[idx, :] = hb
        ab_scr[idx, :] = pb
        return hf, pf, hb, pb

    zeros = jnp.zeros((RG_SEGS, LANES), F32)
    ones = jnp.ones((RG_SEGS, LANES), F32)
    hf, pf, hb, pb = lax.fori_loop(0, seg_len, scan, (zeros, ones, zeros, ones), unroll=8)

    cf = [jnp.zeros((1, LANES), F32)]
    for s in range(1, RG_SEGS):
        cf.append(hf[s - 1:s, :] + pf[s - 1:s, :] * cf[s - 1])
    cbk = [None] * RG_SEGS
    cbk[RG_SEGS - 1] = jnp.zeros((1, LANES), F32)
    for s in range(RG_SEGS - 2, -1, -1):
        cbk[s] = hb[s + 1:s + 2, :] + pb[s + 1:s + 2, :] * cbk[s + 1]

    for s in range(RG_SEGS):
        def fix(k, _, s=s):
            off = pl.multiple_of(s * pitch + k * rt, SUBLANES)
            rows = pl.multiple_of(s * seg_len + k * rt, SUBLANES)
            h = bf_scr[pl.ds(off, rt), :] + af_scr[pl.ds(off, rt), :] * cf[s]
            h = h + (bb_scr[pl.ds(off, rt), :] + ab_scr[pl.ds(off, rt), :] * cbk[s])
            o_ref[pl.ds(rows, rt), :] = h * jax.nn.gelu(gate_ref[pl.ds(rows, rt), :], approximate=True)
            return 0

        lax.fori_loop(0, tiles_per_seg, fix, 0)


def _rglru(p, conv_w, conv_b, w_gates, b_gates, lam, *, batch, seq, gate_col0, rt=512):
    ngrp = conv_w.shape[1] // LANES
    seg_len = seq // RG_SEGS
    rt = min(rt, seg_len)
    assert seq % RG_SEGS == 0 and seg_len % rt == 0
    scr = pltpu.VMEM((RG_SEGS * (seg_len + RG_PAD), LANES), F32)
    return pl.pallas_call(
        functools.partial(_rglru_kernel, seq=seq, rt=rt),
        grid=(batch, ngrp),
        in_specs=[
            pl.BlockSpec((seq, LANES), lambda b, c: (b, c)),
            pl.BlockSpec((seq, LANES), lambda b, c: (b, gate_col0 + c)),
            pl.BlockSpec((4, LANES), lambda b, c: (0, c)),
            pl.BlockSpec((1, LANES), lambda b, c: (0, c)),
            pl.BlockSpec((None, LANES, 4 * LANES), lambda b, c: (c, 0, 0)),
            pl.BlockSpec((None, 1, 4 * LANES), lambda b, c: (c, 0, 0)),
            pl.BlockSpec((2, LANES), lambda b, c: (0, c)),
        ],
        out_specs=pl.BlockSpec((seq, LANES), lambda b, c: (b, c)),
        out_shape=jax.ShapeDtypeStruct((batch * seq, ngrp * LANES), F32),
        scratch_shapes=[scr, scr, scr, scr],
        compiler_params=_cparams(("parallel", "parallel")),
        name="rglru",
    )(p, p, conv_w, conv_b.reshape(1, -1), w_gates, b_gates, lam)


def _rg_gate_weights(wa, ba, wx, bx):
    ndir, nblk, blk, _ = wa.shape
    ngrp = nblk * blk // LANES
    per = LANES // blk

    def bd(w):
        w = w.reshape(ndir, ngrp, per, blk, blk)
        eye = jnp.eye(per, dtype=w.dtype)
        return jnp.einsum("dgpij,pq->dgpiqj", w, eye).reshape(ndir, ngrp, LANES, LANES)

    a, x = bd(wa), bd(wx)
    w = jnp.concatenate([a[0], x[0], a[1], x[1]], axis=-1)
    ba = ba.reshape(ndir, ngrp, 1, LANES)
    bx = bx.reshape(ndir, ngrp, 1, LANES)
    b = jnp.concatenate([ba[0], bx[0], ba[1], bx[1]], axis=-1)
    return w.astype(BF16), b.astype(F32)


def _qkv_kernel(x_ref, cw_ref, o_ref, *, seq, rt, n_qk, n_q, q_scale):
    c = pl.program_id(1)
    cw = cw_ref[...]
    n_tiles = seq // rt
    is_qk = c < n_qk
    scale = jnp.where(c < n_q, q_scale, 1.0).astype(F32)

    def body(i, _):
        r0 = pl.multiple_of(i * rt, rt)
        y = _conv_tile(x_ref, cw, r0, rt, i == 0, i == n_tiles - 1, seq)
        y = y * _sigmoid(y)
        nrm = y * (lax.rsqrt(jnp.sum(y * y, axis=-1, keepdims=True) + EPS) * scale)
        o_ref[pl.ds(r0, rt), :] = jnp.where(is_qk, nrm, y)
        return 0

    lax.fori_loop(0, n_tiles, body, 0)


def _qkv_prep(p, conv_w, *, batch, seq, col0, rt=512):
    ngrp = conv_w.shape[1] // LANES
    return pl.pallas_call(
        functools.partial(_qkv_kernel, seq=seq, rt=rt, n_qk=2 * GDN_HEADS, n_q=GDN_HEADS,
                          q_scale=HEAD_DIM ** -0.5),
        grid=(batch, ngrp),
        in_specs=[
            pl.BlockSpec((seq, LANES), lambda b, c: (b, col0 + c)),
            pl.BlockSpec((4, LANES), lambda b, c: (0, c)),
        ],
        out_specs=pl.BlockSpec((seq, LANES), lambda b, c: (b, c)),
        out_shape=jax.ShapeDtypeStruct((batch * seq, ngrp * LANES), F32),
        compiler_params=_cparams(("parallel", "parallel")),
        name="qkv_prep",
    )(p, conv_w)


def _gate_kernel(ba_ref, alog_ref, dtb_ref, col_ref, row_ref, *, tm):
    nh = GDN_HEADS
    ri = lax.broadcasted_iota(jnp.int32, (LANES, LANES), 0)
    ci = lax.broadcasted_iota(jnp.int32, (LANES, LANES), 1)
    same = (ri // CHUNK) == (ci // CHUNK)
    lower = jnp.where(same & (ri >= ci), 1.0, 0.0).astype(F32)
    upper = jnp.where(same & (ri <= ci), 1.0, 0.0).astype(F32)
    lane = ci
    neg_a = -jnp.exp(alog_ref[...])
    dtb = dtb_ref[...]
    for k in range(tm // LANES):
        sl = slice(k * LANES, (k + 1) * LANES)
        raw = ba_ref[sl, :]
        beta = _sigmoid(raw)
        gk = neg_a * _softplus(raw + dtb)
        gk = jnp.where((lane >= 2 * nh) & (lane < 4 * nh), gk, 0.0)
        cf = jnp.dot(lower, gk, preferred_element_type=F32, precision=lax.Precision.HIGHEST)
        cb = jnp.dot(upper, gk, preferred_element_type=F32, precision=lax.Precision.HIGHEST)
        gc = jnp.where(lane < 3 * nh, cf, cb)
        col = jnp.where(lane < 2 * nh, beta, gc)
        col_ref[sl, :] = col
        colt = col.T
        for q in range(LANES // CHUNK):
            row_ref[k * (LANES // CHUNK) + q] = colt[0:4 * nh, q * CHUNK:(q + 1) * CHUNK]


def _gate_prep(ba, a_log, dt_bias, *, tm=512):
    m = ba.shape[0]
    nh = GDN_HEADS
    pad = lambda v: jnp.zeros((1, LANES), F32).at[0, 2 * nh:4 * nh].set(v.reshape(-1).astype(F32))
    return pl.pallas_call(
        functools.partial(_gate_kernel, tm=tm),
        grid=(m // tm,),
        in_specs=[
            pl.BlockSpec((tm, LANES), lambda i: (i, 0)),
            pl.BlockSpec((1, LANES), lambda i: (0, 0)),
            pl.BlockSpec((1, LANES), lambda i: (0, 0)),
        ],
        out_specs=[
            pl.BlockSpec((tm, LANES), lambda i: (i, 0)),
            pl.BlockSpec((tm // CHUNK, 4 * nh, CHUNK), lambda i: (i, 0, 0)),
        ],
        out_shape=[jax.ShapeDtypeStruct((m, LANES), F32),
                   jax.ShapeDtypeStruct((m // CHUNK, 4 * nh, CHUNK), F32)],
        compiler_params=_cparams(("parallel",)),
        name="gate_prep",
    )(ba, pad(a_log), pad(dt_bias))


def _bmm(a, b):
    return lax.dot_general(a.astype(BF16), b.astype(BF16), (((2,), (1,)), ((0,), (0,))),
                           preferred_element_type=F32)


def _bmm_nt(a, b):
    return lax.dot_general(a.astype(BF16), b.astype(BF16), (((2,), (2,)), ((0,), (0,))),
                           preferred_element_type=F32)


def _bmm_tn(a, b):
    return lax.dot_general(a.astype(BF16), b.astype(BF16), (((1,), (1,)), ((0,), (0,))),
                           preferred_element_type=F32)


def _unit_tri_inverse(lm, ri, ci):
    eye = jnp.where(ri == ci, 1.0, 0.0).astype(F32)
    x = eye - jnp.where((ri // 2) == (ci // 2), lm, 0.0)
    s = 2
    while s < CHUNK:
        c = jnp.where(((ri // (2 * s)) == (ci // (2 * s))) & ((ri // s) != (ci // s)), lm, 0.0)
        x = x - _bmm(_bmm(x, c), x)
        s *= 2
    return x


def _gdn_kernel(q_ref, k_ref, v_ref, col_ref, row_ref, o_ref,
                s_scr, u_scr, w_scr, qd_scr, kd_scr, at_scr, cd_scr, *, reverse, n_chunks, d):
    nh = GDN_HEADS
    g, c, hd = n_chunks, CHUNK, HEAD_DIM
    i = pl.program_id(1)

    @pl.when(i == 0)
    def _():
        s_scr[...] = jnp.zeros_like(s_scr)

    ri = lax.broadcasted_iota(jnp.int32, (1, c, c), 1)
    ci = lax.broadcasted_iota(jnp.int32, (1, c, c), 2)
    if reverse:
        incl, strict, last = ri <= ci, ri < ci, 0
    else:
        incl, strict, last = ri >= ci, ri > ci, c - 1

    col = col_ref[...]
    for h in range(nh):
        li = d * nh + h
        hs = slice(h * hd, (h + 1) * hd)
        q = q_ref[:, hs].reshape(g, c, hd)
        k = k_ref[:, hs].reshape(g, c, hd)
        v = v_ref[:, hs].reshape(g, c, hd)
        beta = jnp.broadcast_to(col[:, li:li + 1], (g * c, hd)).reshape(g, c, hd)
        gc = jnp.broadcast_to(col[:, 2 * nh + li:2 * nh + li + 1], (g * c, hd)).reshape(g, c, hd)
        gc_row = jnp.broadcast_to(row_ref[:, 2 * nh + li:2 * nh + li + 1, :], (g, c, c))
        g_last = gc[:, last:last + 1, :]
        eg = jnp.exp(gc)
        decay = jnp.where(incl, jnp.exp(gc[:, :, :c] - gc_row), 0.0)
        k_beta = k * beta
        v_beta = v * beta
        lm = jnp.where(strict, _bmm_nt(k_beta, k) * decay, 0.0)
        t = _unit_tri_inverse(lm, ri, ci)
        uw = _bmm(t, jnp.concatenate([v_beta, k_beta * eg], axis=-1))
        u_scr[h] = uw[:, :, :hd]
        w_scr[h] = uw[:, :, hd:].astype(BF16)
        at_scr[h] = (_bmm_nt(q, k) * decay).astype(BF16)
        qd_scr[h] = (q * eg).astype(BF16)
        kd_scr[h] = (k * jnp.exp(g_last - gc)).astype(BF16)
        cd_scr[h] = jnp.exp(g_last)

    for step in range(g):
        n = g - 1 - step if reverse else step
        state = s_scr[...]
        sb = state.astype(BF16)
        v_new = u_scr[:, n] - _bmm(w_scr[:, n], sb)
        vb = v_new.astype(BF16)
        o = _bmm(qd_scr[:, n], sb) + _bmm(at_scr[:, n], vb)
        s_scr[...] = state * cd_scr[:, n] + _bmm_tn(kd_scr[:, n], vb)
        for h in range(nh):
            o_ref[n * c:(n + 1) * c, h * hd:(h + 1) * hd] = o[h]


def _gdn(qkv, col, row, *, batch, seq, d, tt=512):
    nh = GDN_HEADS
    width = nh * HEAD_DIM
    nt = seq // tt
    n_chunks = tt // CHUNK
    reverse = d == 1
    tile = (lambda i: nt - 1 - i) if reverse else (lambda i: i)
    return pl.pallas_call(
        functools.partial(_gdn_kernel, reverse=reverse, n_chunks=n_chunks, d=d),
        grid=(batch, nt),
        in_specs=[
            pl.BlockSpec((tt, width), lambda b, i: (b * nt + tile(i), 0)),
            pl.BlockSpec((tt, width), lambda b, i: (b * nt + tile(i), 1)),
            pl.BlockSpec((tt, width), lambda b, i: (b * nt + tile(i), 2)),
            pl.BlockSpec((tt, LANES), lambda b, i: (b * nt + tile(i), 0)),
            pl.BlockSpec((n_chunks, 4 * nh, CHUNK), lambda b, i: (b * nt + tile(i), 0, 0)),
        ],
        out_specs=pl.BlockSpec((tt, width), lambda b, i: (b * nt + tile(i), 0)),
        out_shape=jax.ShapeDtypeStruct((batch * seq, width), F32),
        scratch_shapes=[
            pltpu.VMEM((nh, HEAD_DIM, HEAD_DIM), F32),
            pltpu.VMEM((nh, n_chunks, CHUNK, HEAD_DIM), F32),
            pltpu.VMEM((nh, n_chunks, CHUNK, HEAD_DIM), BF16),
            pltpu.VMEM((nh, n_chunks, CHUNK, HEAD_DIM), BF16),
            pltpu.VMEM((nh, n_chunks, CHUNK, HEAD_DIM), BF16),
            pltpu.VMEM((nh, n_chunks, CHUNK, CHUNK), BF16),
            pltpu.VMEM((nh, n_chunks, 1, HEAD_DIM), F32),
        ],
        compiler_params=_cparams(("parallel", "arbitrary")),
        name="gdn_bwd" if reverse else "gdn_fwd",
    )(qkv, qkv, qkv, col, row)


def _outproj_kernel(yrg_ref, of_ref, ob_ref, z_ref, gn_ref, w_ref, x_ref, o_ref):
    o = of_ref[...] + ob_ref[...]
    z = z_ref[...]
    gn = gn_ref[...]
    parts = [yrg_ref[...].astype(BF16)]
    for h in range(GDN_HEADS):
        hs = slice(h * HEAD_DIM, (h + 1) * HEAD_DIM)
        zh = z[:, hs]
        parts.append((_rms(o[:, hs]) * gn * (zh * _sigmoid(zh))).astype(BF16))
    y = jnp.concatenate(parts, axis=-1)
    o_ref[...] = x_ref[...] + jnp.dot(y, w_ref[...], preferred_element_type=F32)


def _out_proj(y_rg, o_f, o_b, p, gdn_norm, w_out, x, *, z_col, tm=512):
    m, d = x.shape
    wr = y_rg.shape[1]
    wg = o_f.shape[1]
    return pl.pallas_call(
        _outproj_kernel,
        grid=(m // tm,),
        in_specs=[
            pl.BlockSpec((tm, wr), lambda i: (i, 0)),
            pl.BlockSpec((tm, wg), lambda i: (i, 0)),
            pl.BlockSpec((tm, wg), lambda i: (i, 0)),
            pl.BlockSpec((tm, wg), lambda i: (i, z_col)),
            pl.BlockSpec((1, HEAD_DIM), lambda i: (0, 0)),
            pl.BlockSpec((wr + wg, d), lambda i: (0, 0)),
            pl.BlockSpec((tm, d), lambda i: (i, 0)),
        ],
        out_specs=pl.BlockSpec((tm, d), lambda i: (i, 0)),
        out_shape=jax.ShapeDtypeStruct((m, d), F32),
        compiler_params=_cparams(("parallel",)),
        name="out_proj",
    )(y_rg, o_f, o_b, p, gdn_norm.reshape(1, -1), w_out, x)


def _mixer(x1, batch, seq, mix_norm, w_in, w_out, rg_conv_w, rg_conv_b, rg_gate_a_w, rg_gate_a_b,
           rg_gate_x_w, rg_gate_x_b, rg_lambda, gdn_conv_w, gdn_a_log, gdn_dt_bias, gdn_norm):
    rg_w = rg_conv_w.shape[1]
    qkv_w = gdn_conv_w.shape[1]
    gdn_vw = GDN_HEADS * HEAD_DIM
    n_main = 2 * rg_w + qkv_w + gdn_vw
    n_gate = w_in.shape[1] - n_main
    w_main = w_in[:, :n_main].astype(BF16)
    w_ba = jnp.zeros((w_in.shape[0], LANES), F32).at[:, :n_gate].set(w_in[:, n_main:]).astype(BF16)
    p, ba = _in_proj(x1, mix_norm, w_main, w_ba)

    w_gates, b_gates = _rg_gate_weights(rg_gate_a_w, rg_gate_a_b, rg_gate_x_w, rg_gate_x_b)
    y_rg = _rglru(p, rg_conv_w, rg_conv_b, w_gates, b_gates, rg_lambda.astype(F32),
                  batch=batch, seq=seq, gate_col0=rg_w // LANES)

    qkv = _qkv_prep(p, gdn_conv_w, batch=batch, seq=seq, col0=2 * rg_w // LANES)
    col, row = _gate_prep(ba, gdn_a_log, gdn_dt_bias)
    o_f = _gdn(qkv, col, row, batch=batch, seq=seq, d=0)
    o_b = _gdn(qkv, col, row, batch=batch, seq=seq, d=1)
    return _out_proj(y_rg, o_f, o_b, p, gdn_norm, w_out.astype(BF16), x1,
                     z_col=(2 * rg_w + qkv_w) // gdn_vw)


def kernel(x, ffn1_norm, ffn1_w_gate, ffn1_w_up, ffn1_w_down, mix_norm, w_in, w_out, rg_conv_w, rg_conv_b, rg_gate_a_w, rg_gate_a_b, rg_gate_x_w, rg_gate_x_b, rg_lambda, gdn_conv_w, gdn_a_log, gdn_dt_bias, gdn_norm, ffn2_norm, ffn2_w_gate, ffn2_w_up, ffn2_w_down, final_norm):
    batch, seq, d_model = x.shape
    depth = ffn1_norm.shape[0]
    h = x.reshape(batch * seq, d_model)
    for l in range(depth):
        last = l == depth - 1
        h = _ffn(h, ffn1_norm[l], ffn1_w_gate[l].astype(BF16), ffn1_w_up[l].astype(BF16),
                 ffn1_w_down[l].astype(BF16), final_norm, final_norm=False)
        h = _mixer(h, batch, seq, mix_norm[l], w_in[l], w_out[l], rg_conv_w[l], rg_conv_b[l],
                   rg_gate_a_w[l], rg_gate_a_b[l], rg_gate_x_w[l], rg_gate_x_b[l], rg_lambda[l],
                   gdn_conv_w[l], gdn_a_log[l], gdn_dt_bias[l], gdn_norm[l])
        h = _ffn(h, ffn2_norm[l], ffn2_w_gate[l].astype(BF16), ffn2_w_up[l].astype(BF16),
                 ffn2_w_down[l].astype(BF16), final_norm, final_norm=last)
    return h.reshape(batch, seq, d_model)
```

```python
import functools
import itertools

import jax
import jax.numpy as jnp
from jax import lax
from jax.experimental import pallas as pl
from jax.experimental.pallas import tpu as pltpu

F32 = jnp.float32
BF16 = jnp.bfloat16

EPS = 1e-6
RG_C = 8.0
RG_BLOCKS = 8
GDN_HEADS = 4
HEAD_DIM = 128
CHUNK = 64
LANES = 128
SUBLANES = 8
VMEM_LIMIT = 56 * 1024 * 1024


def _cparams(sem):
    return pltpu.CompilerParams(dimension_semantics=sem, vmem_limit_bytes=VMEM_LIMIT)


def _rms(x):
    return x * lax.rsqrt(jnp.mean(x * x, axis=-1, keepdims=True) + EPS)


def _sigmoid(x):
    return 1.0 / (1.0 + jnp.exp(-x))


def _softplus(x):
    return jnp.maximum(x, 0.0) + jnp.log(1.0 + jnp.exp(-jnp.abs(x)))


def _ffn_kernel(x_ref, g_ref, wg_ref, wu_ref, wd_ref, fg_ref, o_ref, h_scr, acc_scr, *, final_norm):
    j = pl.program_id(1)

    @pl.when(j == 0)
    def _():
        h_scr[...] = (_rms(x_ref[...]) * g_ref[...]).astype(BF16)
        acc_scr[...] = jnp.zeros_like(acc_scr)

    h = h_scr[...]
    gate = jnp.dot(h, wg_ref[...], preferred_element_type=F32)
    up = jnp.dot(h, wu_ref[...], preferred_element_type=F32)
    act = (gate * _sigmoid(gate) * up).astype(BF16)
    acc_scr[...] += jnp.dot(act, wd_ref[...], preferred_element_type=F32)

    @pl.when(j == pl.num_programs(1) - 1)
    def _():
        y = x_ref[...] + 0.5 * acc_scr[...]
        if final_norm:
            y = _rms(y) * fg_ref[...]
        o_ref[...] = y


def _ffn(x, g, wg, wu, wd, fg, *, final_norm, tm=1024, tf=256):
    m, d = x.shape
    f = wg.shape[1]
    assert m % tm == 0 and f % tf == 0
    return pl.pallas_call(
        functools.partial(_ffn_kernel, final_norm=final_norm),
        grid=(m // tm, f // tf),
        in_specs=[
            pl.BlockSpec((tm, d), lambda i, j: (i, 0)),
            pl.BlockSpec((1, d), lambda i, j: (0, 0)),
            pl.BlockSpec((d, tf), lambda i, j: (0, j)),
            pl.BlockSpec((d, tf), lambda i, j: (0, j)),
            pl.BlockSpec((tf, d), lambda i, j: (j, 0)),
            pl.BlockSpec((1, d), lambda i, j: (0, 0)),
        ],
        out_specs=pl.BlockSpec((tm, d), lambda i, j: (i, 0)),
        out_shape=jax.ShapeDtypeStruct((m, d), F32),
        scratch_shapes=[pltpu.VMEM((tm, d), BF16), pltpu.VMEM((tm, d), F32)],
        compiler_params=_cparams(("parallel", "arbitrary")),
        name="ffn_final" if final_norm else "ffn",
    )(x, g.reshape(1, d), wg, wu, wd, fg.reshape(1, d))


def _inproj_kernel(x_ref, g_ref, w_ref, wba_ref, p_ref, ba_ref, h_scr):
    j = pl.program_id(1)

    @pl.when(j == 0)
    def _():
        h = (_rms(x_ref[...]) * g_ref[...]).astype(BF16)
        h_scr[...] = h
        ba_ref[...] = jnp.dot(h, wba_ref[...], preferred_element_type=F32)

    p_ref[...] = jnp.dot(h_scr[...], w_ref[...], preferred_element_type=F32)


def _in_proj(x, g, w_main, w_ba, *, tm=1024, tn=512):
    m, d = x.shape
    n = w_main.shape[1]
    assert m % tm == 0 and n % tn == 0
    return pl.pallas_call(
        _inproj_kernel,
        grid=(m // tm, n // tn),
        in_specs=[
            pl.BlockSpec((tm, d), lambda i, j: (i, 0)),
            pl.BlockSpec((1, d), lambda i, j: (0, 0)),
            pl.BlockSpec((d, tn), lambda i, j: (0, j)),
            pl.BlockSpec((d, LANES), lambda i, j: (0, 0)),
        ],
        out_specs=[
            pl.BlockSpec((tm, tn), lambda i, j: (i, j)),
            pl.BlockSpec((tm, LANES), lambda i, j: (i, 0)),
        ],
        out_shape=[jax.ShapeDtypeStruct((m, n), F32), jax.ShapeDtypeStruct((m, LANES), F32)],
        scratch_shapes=[pltpu.VMEM((tm, d), BF16)],
        compiler_params=_cparams(("parallel", "arbitrary")),
        name="in_proj",
    )(x, g.reshape(1, d), w_main, w_ba)


def _conv_tile(x_ref, cw, r0, rt, first, last, seq):
    x0 = x_ref[pl.ds(r0, rt), :]
    prev = x_ref[pl.ds(jnp.maximum(r0 - SUBLANES, 0), SUBLANES), :]
    prev = jnp.where(first, 0.0, prev)
    nxt = x_ref[pl.ds(jnp.minimum(r0 + rt, seq - SUBLANES), SUBLANES), :]
    nxt = jnp.where(last, 0.0, nxt)
    w = jnp.concatenate([prev, x0, nxt], axis=0)
    acc = cw[0:1, :] * w[SUBLANES - 2:SUBLANES - 2 + rt]
    acc = acc + cw[1:2, :] * w[SUBLANES - 1:SUBLANES - 1 + rt]
    acc = acc + cw[2:3, :] * x0
    acc = acc + cw[3:4, :] * w[SUBLANES + 1:SUBLANES + 1 + rt]
    return acc


RG_SEGS = SUBLANES
RG_PAD = 8


def _rglru_kernel(x_ref, gate_ref, cw_ref, cb_ref, wg_ref, bg_ref, lam_ref, o_ref,
                  af_scr, bf_scr, ab_scr, bb_scr, *, seq, rt):
    seg_len = seq // RG_SEGS
    pitch = seg_len + RG_PAD
    tiles_per_seg = seg_len // rt
    n_tiles = seq // rt
    a_scr = (af_scr, ab_scr)
    b_scr = (bf_scr, bb_scr)

    cw = cw_ref[...]
    cb = cb_ref[...]
    bg = bg_ref[...]
    neg_c_sp = -RG_C * _softplus(-lam_ref[...])

    def gates(i, _):
        r0 = pl.multiple_of(i * rt, rt)
        xc = _conv_tile(x_ref, cw, r0, rt, i == 0, i == n_tiles - 1, seq) + cb
        pre = jnp.dot(xc.astype(BF16), wg_ref[...], preferred_element_type=F32) + bg
        seg = i // tiles_per_seg
        off = pl.multiple_of(seg * pitch + (i - seg * tiles_per_seg) * rt, SUBLANES)
        for d in range(2):
            r = _sigmoid(pre[:, (2 * d) * LANES:(2 * d + 1) * LANES])
            ig = _sigmoid(pre[:, (2 * d + 1) * LANES:(2 * d + 2) * LANES])
            log_a = r * neg_c_sp[d:d + 1, :]
            a = jnp.exp(log_a)
            b = jnp.sqrt(1.0 - a * a) * (ig * xc)
            a_scr[d][pl.ds(off, rt), :] = a
            b_scr[d][pl.ds(off, rt), :] = b
        return 0

    lax.fori_loop(0, n_tiles, gates, 0)

    def scan(t, carry):
        hf, pf, hb, pb = carry
        idx = pl.ds(t, RG_SEGS, stride=pitch)
        a = af_scr[idx, :]
        hf = a * hf + bf_scr[idx, :]
        pf = a * pf
        bf_scr[idx, :] = hf
        af_scr[idx, :] = pf
        idx = pl.ds(seg_len - 1 - t, RG_SEGS, stride=pitch)
        a = ab_scr[idx, :]
        hb = a * hb + bb_scr[idx, :]
        pb = a * pb
        bb_scr[idx, :] = hb
        ab_scr[idx, :] = pb
        return hf, pf, hb, pb

    zeros = jnp.zeros((RG_SEGS, LANES), F32)
    ones = jnp.ones((RG_SEGS, LANES), F32)
    hf, pf, hb, pb = lax.fori_loop(0, seg_len, scan, (zeros, ones, zeros, ones), unroll=8)

    cf = [jnp.zeros((1, LANES), F32)]
    for s in range(1, RG_SEGS):
        cf.append(hf[s - 1:s, :] + pf[s - 1:s, :] * cf[s - 1])
    cbk = [None] * RG_SEGS
    cbk[RG_SEGS - 1] = jnp.zeros((1, LANES), F32)
    for s in range(RG_SEGS - 2, -1, -1):
        cbk[s] = hb[s + 1:s + 2, :] + pb[s + 1:s + 2, :] * cbk[s + 1]

    for s in range(RG_SEGS):
        def fix(k, _, s=s):
            off = pl.multiple_of(s * pitch + k * rt, SUBLANES)
            rows = pl.multiple_of(s * seg_len + k * rt, SUBLANES)
            h = bf_scr[pl.ds(off, rt), :] + af_scr[pl.ds(off, rt), :] * cf[s]
            h = h + (bb_scr[pl.ds(off, rt), :] + ab_scr[pl.ds(off, rt), :] * cbk[s])
            o_ref[pl.ds(rows, rt), :] = h * jax.nn.gelu(gate_ref[pl.ds(rows, rt), :], approximate=True)
            return 0

        lax.fori_loop(0, tiles_per_seg, fix, 0)


def _rglru(p, conv_w, conv_b, w_gates, b_gates, lam, *, batch, seq, gate_col0, rt=512):
    ngrp = conv_w.shape[1] // LANES
    seg_len = seq // RG_SEGS
    rt = min(rt, seg_len)
    assert seq % RG_SEGS == 0 and seg_len % rt == 0
    scr = pltpu.VMEM((RG_SEGS * (seg_len + RG_PAD), LANES), F32)
    return pl.pallas_call(
        functools.partial(_rglru_kernel, seq=seq, rt=rt),
        grid=(batch, ngrp),
        in_specs=[
            pl.BlockSpec((seq, LANES), lambda b, c: (b, c)),
            pl.BlockSpec((seq, LANES), lambda b, c: (b, gate_col0 + c)),
            pl.BlockSpec((4, LANES), lambda b, c: (0, c)),
            pl.BlockSpec((1, LANES), lambda b, c: (0, c)),
            pl.BlockSpec((None, LANES, 4 * LANES), lambda b, c: (c, 0, 0)),
            pl.BlockSpec((None, 1, 4 * LANES), lambda b, c: (c, 0, 0)),
            pl.BlockSpec((2, LANES), lambda b, c: (0, c)),
        ],
        out_specs=pl.BlockSpec((seq, LANES), lambda b, c: (b, c)),
        out_shape=jax.ShapeDtypeStruct((batch * seq, ngrp * LANES), F32),
        scratch_shapes=[scr, scr, scr, scr],
        compiler_params=_cparams(("parallel", "parallel")),
        name="rglru",
    )(p, p, conv_w, conv_b.reshape(1, -1), w_gates, b_gates, lam)


def _rg_gate_weights(wa, ba, wx, bx):
    ndir, nblk, blk, _ = wa.shape
    ngrp = nblk * blk // LANES
    per = LANES // blk

    def bd(w):
        w = w.reshape(ndir, ngrp, per, blk, blk)
        eye = jnp.eye(per, dtype=w.dtype)
        return jnp.einsum("dgpij,pq->dgpiqj", w, eye).reshape(ndir, ngrp, LANES, LANES)

    a, x = bd(wa), bd(wx)
    w = jnp.concatenate([a[0], x[0], a[1], x[1]], axis=-1)
    ba = ba.reshape(ndir, ngrp, 1, LANES)
    bx = bx.reshape(ndir, ngrp, 1, LANES)
    b = jnp.concatenate([ba[0], bx[0], ba[1], bx[1]], axis=-1)
    return w.astype(BF16), b.astype(F32)


def _qkv_kernel(x_ref, cw_ref, o_ref, *, seq, rt, n_qk, n_q, q_scale):
    c = pl.program_id(1)
    cw = cw_ref[...]
    n_tiles = seq // rt
    is_qk = c < n_qk
    scale = jnp.where(c < n_q, q_scale, 1.0).astype(F32)

    def body(i, _):
        r0 = pl.multiple_of(i * rt, rt)
        y = _conv_tile(x_ref, cw, r0, rt, i == 0, i == n_tiles - 1, seq)
        y = y * _sigmoid(y)
        nrm = y * (lax.rsqrt(jnp.sum(y * y, axis=-1, keepdims=True) + EPS) * scale)
        o_ref[pl.ds(r0, rt), :] = jnp.where(is_qk, nrm, y)
        return 0

    lax.fori_loop(0, n_tiles, body, 0)


def _qkv_prep(p, conv_w, *, batch, seq, col0, rt=512):
    ngrp = conv_w.shape[1] // LANES
    return pl.pallas_call(
        functools.partial(_qkv_kernel, seq=seq, rt=rt, n_qk=2 * GDN_HEADS, n_q=GDN_HEADS,
                          q_scale=HEAD_DIM ** -0.5),
        grid=(batch, ngrp),
        in_specs=[
            pl.BlockSpec((seq, LANES), lambda b, c: (b, col0 + c)),
            pl.BlockSpec((4, LANES), lambda b, c: (0, c)),
        ],
        out_specs=pl.BlockSpec((seq, LANES), lambda b, c: (b, c)),
        out_shape=jax.ShapeDtypeStruct((batch * seq, ngrp * LANES), F32),
        compiler_params=_cparams(("parallel", "parallel")),
        name="qkv_prep",
    )(p, conv_w)


def _gate_kernel(ba_ref, alog_ref, dtb_ref, col_ref, row_ref, *, tm):
    nh = GDN_HEADS
    ri = lax.broadcasted_iota(jnp.int32, (LANES, LANES), 0)
    ci = lax.broadcasted_iota(jnp.int32, (LANES, LANES), 1)
    same = (ri // CHUNK) == (ci // CHUNK)
    lower = jnp.where(same & (ri >= ci), 1.0, 0.0).astype(F32)
    upper = jnp.where(same & (ri <= ci), 1.0, 0.0).astype(F32)
    lane = ci
    neg_a = -jnp.exp(alog_ref[...])
    dtb = dtb_ref[...]
    for k in range(tm // LANES):
        sl = slice(k * LANES, (k + 1) * LANES)
        raw = ba_ref[sl, :]
        beta = _sigmoid(raw)
        gk = neg_a * _softplus(raw + dtb)
        gk = jnp.where((lane >= 2 * nh) & (lane < 4 * nh), gk, 0.0)
        cf = jnp.dot(lower, gk, preferred_element_type=F32, precision=lax.Precision.HIGHEST)
        cb = jnp.dot(upper, gk, preferred_element_type=F32, precision=lax.Precision.HIGHEST)
        gc = jnp.where(lane < 3 * nh, cf, cb)
        col = jnp.where(lane < 2 * nh, beta, gc)
        col_ref[sl, :] = col
        colt = col.T
        for q in range(LANES // CHUNK):
            rows = [jnp.concatenate([colt[(2 + dr) * nh + h:(2 + dr) * nh + h + 1, q * CHUNK:(q + 1) * CHUNK]
                                     for h in range(nh)], axis=1) for dr in range(2)]
            rows.append(jnp.zeros((SUBLANES - 2, nh * CHUNK), F32))
            row_ref[k * (LANES // CHUNK) + q] = jnp.concatenate(rows, axis=0)


def _gate_prep(ba, a_log, dt_bias, *, tm=512):
    m = ba.shape[0]
    nh = GDN_HEADS
    pad = lambda v: jnp.zeros((1, LANES), F32).at[0, 2 * nh:4 * nh].set(v.reshape(-1).astype(F32))
    return pl.pallas_call(
        functools.partial(_gate_kernel, tm=tm),
        grid=(m // tm,),
        in_specs=[
            pl.BlockSpec((tm, LANES), lambda i: (i, 0)),
            pl.BlockSpec((1, LANES), lambda i: (0, 0)),
            pl.BlockSpec((1, LANES), lambda i: (0, 0)),
        ],
        out_specs=[
            pl.BlockSpec((tm, LANES), lambda i: (i, 0)),
            pl.BlockSpec((tm // CHUNK, SUBLANES, nh * CHUNK), lambda i: (i, 0, 0)),
        ],
        out_shape=[jax.ShapeDtypeStruct((m, LANES), F32),
                   jax.ShapeDtypeStruct((m // CHUNK, SUBLANES, nh * CHUNK), F32)],
        compiler_params=_cparams(("parallel",)),
        name="gate_prep",
    )(ba, pad(a_log), pad(dt_bias))


def _bmm(a, b):
    return lax.dot_general(a.astype(BF16), b.astype(BF16), (((2,), (1,)), ((0,), (0,))),
                           preferred_element_type=F32)


def _bmm_nt(a, b):
    return lax.dot_general(a.astype(BF16), b.astype(BF16), (((2,), (2,)), ((0,), (0,))),
                           preferred_element_type=F32)


def _bmm_tn(a, b):
    return lax.dot_general(a.astype(BF16), b.astype(BF16), (((1,), (1,)), ((0,), (0,))),
                           preferred_element_type=F32)


def _unit_tri_inverse_levels(lm, ri, ci, bdmask, nb):
    def blockdiag(a):
        ab = a.astype(BF16)
        return jnp.where(bdmask, jnp.concatenate([ab] * nb, axis=1), jnp.zeros((), BF16))

    eye = jnp.where(ri == ci, 1.0, 0.0).astype(F32)
    x = eye - jnp.where((ri // 2) == (ci // 2), lm, 0.0)
    s = 2
    while s < CHUNK:
        cm = jnp.where(((ri // (2 * s)) == (ci // (2 * s))) & ((ri // s) != (ci // s)), lm, 0.0)
        x = x - _bmm(_bmm(x, blockdiag(cm)), blockdiag(x))
        yield x
        s *= 2


def _gdn_chunk_local(q_ref, k_ref, v_ref, col_ref, row_ref, dst, *, reverse, g, d):
    u_scr, w_scr, qd_scr, kd_scr, at_scr, cd_scr = dst
    nh, c, hd = GDN_HEADS, CHUNK, HEAD_DIM
    pw = nh * c
    fw = nh * hd
    ri = lax.broadcasted_iota(jnp.int32, (1, c, pw), 1)
    ci = lax.broadcasted_iota(jnp.int32, (1, c, pw), 2) % c
    if reverse:
        incl, strict, last = ri <= ci, ri < ci, 0
    else:
        incl, strict, last = ri >= ci, ri > ci, c - 1
    bdmask = (lax.broadcasted_iota(jnp.int32, (1, pw, pw), 1) // c
              == lax.broadcasted_iota(jnp.int32, (1, pw, pw), 2) // c)
    kmask = (lax.broadcasted_iota(jnp.int32, (1, pw, fw), 1) // c
             == lax.broadcasted_iota(jnp.int32, (1, pw, fw), 2) // hd)
    low_half = lax.broadcasted_iota(jnp.int32, (g * c, hd), 1) < c

    col = col_ref[...]
    bcast = lambda j: jnp.broadcast_to(col[:, j:j + 1], (g * c, hd))
    beta = jnp.concatenate([bcast(d * nh + h) for h in range(nh)], axis=1).reshape(g, c, fw)
    gcs = [bcast((2 + d) * nh + h) for h in range(nh)]
    gc = jnp.concatenate(gcs, axis=1).reshape(g, c, fw)
    gc_col = jnp.concatenate([jnp.where(low_half, gcs[h], gcs[h + 1]) for h in range(0, nh, 2)],
                             axis=1).reshape(g, c, pw)
    gc_row = row_ref[:, d:d + 1, :]
    g_last = gc[:, last:last + 1, :]
    eg = jnp.exp(gc)
    decay = jnp.where(incl, jnp.exp(gc_col - gc_row), 0.0)
    q = q_ref[...].reshape(g, c, fw)
    k = k_ref[...].reshape(g, c, fw)
    v = v_ref[...].reshape(g, c, fw)
    k_beta = k * beta
    v_beta = v * beta
    kb16 = k.astype(BF16)
    k_bd = jnp.where(kmask, jnp.concatenate([kb16] * nh, axis=1), jnp.zeros((), BF16))
    kq = _bmm_nt(jnp.concatenate([k_beta, q], axis=1), k_bd)
    lm = jnp.where(strict, kq[:, :c] * decay, 0.0)
    attn = kq[:, c:] * decay
    yield
    for t in _unit_tri_inverse_levels(lm, ri, ci, bdmask, nh):
        yield
    kbe = k_beta * eg
    qd = q * eg
    kd = k * jnp.exp(g_last - gc)
    cd = jnp.exp(g_last)
    for h in range(nh):
        hs = slice(h * hd, (h + 1) * hd)
        ps = slice(h * c, (h + 1) * c)
        uw = _bmm(t[:, :, ps], jnp.concatenate([v_beta[:, :, hs], kbe[:, :, hs]], axis=-1))
        u_scr[h] = uw[:, :, :hd]
        w_scr[h] = uw[:, :, hd:].astype(BF16)
        at_scr[h] = attn[:, :, ps].astype(BF16)
        qd_scr[h] = qd[:, :, hs].astype(BF16)
        kd_scr[h] = kd[:, :, hs].astype(BF16)
        cd_scr[h] = cd[:, :, hs]
        if h % 2 == 1:
            yield


def _gdn_recurrence(src, s_scr, o_ref, *, reverse, g):
    u_scr, w_scr, qd_scr, kd_scr, at_scr, cd_scr = src
    nh, c, hd = GDN_HEADS, CHUNK, HEAD_DIM
    for step in range(g):
        n = g - 1 - step if reverse else step
        state = s_scr[...]
        sb = state.astype(BF16)
        v_new = u_scr[:, n] - _bmm(w_scr[:, n], sb)
        vb = v_new.astype(BF16)
        o = _bmm(qd_scr[:, n], sb) + _bmm(at_scr[:, n], vb)
        s_scr[...] = state * cd_scr[:, n] + _bmm_tn(kd_scr[:, n], vb)
        for h in range(nh):
            o_ref[n * c:(n + 1) * c, h * hd:(h + 1) * hd] = o[h]
        yield


def _gdn_kernel(q_ref, k_ref, v_ref, col_ref, row_ref, o_ref, s_scr, *sets, reverse, n_chunks, d):
    set0, set1 = sets[:len(sets) // 2], sets[len(sets) // 2:]
    i = pl.program_id(1)

    @pl.when(i == 0)
    def _():
        s_scr[...] = jnp.zeros_like(s_scr)
        for r in set1:
            r[...] = jnp.zeros_like(r)

    def step(dst, src):
        local = _gdn_chunk_local(q_ref, k_ref, v_ref, col_ref, row_ref, dst, reverse=reverse, g=n_chunks, d=d)
        recur = _gdn_recurrence(src, s_scr, o_ref, reverse=reverse, g=n_chunks)
        for _ in itertools.zip_longest(local, recur):
            pass

    @pl.when(i % 2 == 0)
    def _():
        step(set0, set1)

    @pl.when(i % 2 == 1)
    def _():
        step(set1, set0)


def _gdn(qkv, col, row, *, batch, seq, d, tt=512):
    nh = GDN_HEADS
    width = nh * HEAD_DIM
    nt = seq // tt
    n_chunks = tt // CHUNK
    reverse = d == 1
    order = (lambda t: nt - 1 - t) if reverse else (lambda t: t)
    tile_in = lambda b, i: b * nt + order(jnp.minimum(i, nt - 1))
    tile_out = lambda b, i: b * nt + order(jnp.maximum(i - 1, 0))
    scratch_set = [
        pltpu.VMEM((nh, n_chunks, CHUNK, HEAD_DIM), F32),
        pltpu.VMEM((nh, n_chunks, CHUNK, HEAD_DIM), BF16),
        pltpu.VMEM((nh, n_chunks, CHUNK, HEAD_DIM), BF16),
        pltpu.VMEM((nh, n_chunks, CHUNK, HEAD_DIM), BF16),
        pltpu.VMEM((nh, n_chunks, CHUNK, CHUNK), BF16),
        pltpu.VMEM((nh, n_chunks, 1, HEAD_DIM), F32),
    ]
    return pl.pallas_call(
        functools.partial(_gdn_kernel, reverse=reverse, n_chunks=n_chunks, d=d),
        grid=(batch, nt + 1),
        in_specs=[
            pl.BlockSpec((tt, width), lambda b, i: (tile_in(b, i), 0)),
            pl.BlockSpec((tt, width), lambda b, i: (tile_in(b, i), 1)),
            pl.BlockSpec((tt, width), lambda b, i: (tile_in(b, i), 2)),
            pl.BlockSpec((tt, LANES), lambda b, i: (tile_in(b, i), 0)),
            pl.BlockSpec((n_chunks, SUBLANES, nh * CHUNK), lambda b, i: (tile_in(b, i), 0, 0)),
        ],
        out_specs=pl.BlockSpec((tt, width), lambda b, i: (tile_out(b, i), 0)),
        out_shape=jax.ShapeDtypeStruct((batch * seq, width), F32),
        scratch_shapes=[pltpu.VMEM((nh, HEAD_DIM, HEAD_DIM), F32)] + scratch_set + scratch_set,
        compiler_params=_cparams(("parallel", "arbitrary")),
        name="gdn_bwd" if reverse else "gdn_fwd",
    )(qkv, qkv, qkv, col, row)


def _outproj_kernel(yrg_ref, of_ref, ob_ref, z_ref, gn_ref, w_ref, x_ref, o_ref):
    o = of_ref[...] + ob_ref[...]
    z = z_ref[...]
    gn = gn_ref[...]
    parts = [yrg_ref[...].astype(BF16)]
    for h in range(GDN_HEADS):
        hs = slice(h * HEAD_DIM, (h + 1) * HEAD_DIM)
        zh = z[:, hs]
        parts.append((_rms(o[:, hs]) * gn * (zh * _sigmoid(zh))).astype(BF16))
    y = jnp.concatenate(parts, axis=-1)
    o_ref[...] = x_ref[...] + jnp.dot(y, w_ref[...], preferred_element_type=F32)


def _out_proj(y_rg, o_f, o_b, p, gdn_norm, w_out, x, *, z_col, tm=512):
    m, d = x.shape
    wr = y_rg.shape[1]
    wg = o_f.shape[1]
    return pl.pallas_call(
        _outproj_kernel,
        grid=(m // tm,),
        in_specs=[
            pl.BlockSpec((tm, wr), lambda i: (i, 0)),
            pl.BlockSpec((tm, wg), lambda i: (i, 0)),
            pl.BlockSpec((tm, wg), lambda i: (i, 0)),
            pl.BlockSpec((tm, wg), lambda i: (i, z_col)),
            pl.BlockSpec((1, HEAD_DIM), lambda i: (0, 0)),
            pl.BlockSpec((wr + wg, d), lambda i: (0, 0)),
            pl.BlockSpec((tm, d), lambda i: (i, 0)),
        ],
        out_specs=pl.BlockSpec((tm, d), lambda i: (i, 0)),
        out_shape=jax.ShapeDtypeStruct((m, d), F32),
        compiler_params=_cparams(("parallel",)),
        name="out_proj",
    )(y_rg, o_f, o_b, p, gdn_norm.reshape(1, -1), w_out, x)


def _mixer(x1, batch, seq, mix_norm, w_in, w_out, rg_conv_w, rg_conv_b, rg_gate_a_w, rg_gate_a_b,
           rg_gate_x_w, rg_gate_x_b, rg_lambda, gdn_conv_w, gdn_a_log, gdn_dt_bias, gdn_norm):
    rg_w = rg_conv_w.shape[1]
    qkv_w = gdn_conv_w.shape[1]
    gdn_vw = GDN_HEADS * HEAD_DIM
    n_main = 2 * rg_w + qkv_w + gdn_vw
    n_gate = w_in.shape[1] - n_main
    w_main = w_in[:, :n_main].astype(BF16)
    w_ba = jnp.zeros((w_in.shape[0], LANES), F32).at[:, :n_gate].set(w_in[:, n_main:]).astype(BF16)
    p, ba = _in_proj(x1, mix_norm, w_main, w_ba)

    w_gates, b_gates = _rg_gate_weights(rg_gate_a_w, rg_gate_a_b, rg_gate_x_w, rg_gate_x_b)
    y_rg = _rglru(p, rg_conv_w, rg_conv_b, w_gates, b_gates, rg_lambda.astype(F32),
                  batch=batch, seq=seq, gate_col0=rg_w // LANES)

    qkv = _qkv_prep(p, gdn_conv_w, batch=batch, seq=seq, col0=2 * rg_w // LANES)
    col, row = _gate_prep(ba, gdn_a_log, gdn_dt_bias)
    o_f = _gdn(qkv, col, row, batch=batch, seq=seq, d=0)
    o_b = _gdn(qkv, col, row, batch=batch, seq=seq, d=1)
    return _out_proj(y_rg, o_f, o_b, p, gdn_norm, w_out.astype(BF16), x1,
                     z_col=(2 * rg_w + qkv_w) // gdn_vw)


def kernel(x, ffn1_norm, ffn1_w_gate, ffn1_w_up, ffn1_w_down, mix_norm, w_in, w_out, rg_conv_w, rg_conv_b, rg_gate_a_w, rg_gate_a_b, rg_gate_x_w, rg_gate_x_b, rg_lambda, gdn_conv_w, gdn_a_log, gdn_dt_bias, gdn_norm, ffn2_norm, ffn2_w_gate, ffn2_w_up, ffn2_w_down, final_norm):
    batch, seq, d_model = x.shape
    depth = ffn1_norm.shape[0]
    h = x.reshape(batch * seq, d_model)
    for l in range(depth):
        last = l == depth - 1
        h = _ffn(h, ffn1_norm[l], ffn1_w_gate[l].astype(BF16), ffn1_w_up[l].astype(BF16),
                 ffn1_w_down[l].astype(BF16), final_norm, final_norm=False)
        h = _mixer(h, batch, seq, mix_norm[l], w_in[l], w_out[l], rg_conv_w[l], rg_conv_b[l],
                   rg_gate_a_w[l], rg_gate_a_b[l], rg_gate_x_w[l], rg_gate_x_b[l], rg_lambda[l],
                   gdn_conv_w[l], gdn_a_log[l], gdn_dt_bias[l], gdn_norm[l])
        h = _ffn(h, ffn2_norm[l], ffn2_w_gate[l].astype(BF16), ffn2_w_up[l].astype(BF16),
                 ffn2_w_down[l].astype(BF16), final_norm, final_norm=last)
    return h.reshape(batch, seq, d_model)
```

```python
import functools
import itertools

import jax
import jax.numpy as jnp
from jax import lax
from jax.experimental import pallas as pl
from jax.experimental.pallas import tpu as pltpu

F32 = jnp.float32
BF16 = jnp.bfloat16

EPS = 1e-6
RG_C = 8.0
RG_BLOCKS = 8
GDN_HEADS = 4
HEAD_DIM = 128
CHUNK = 64
LANES = 128
SUBLANES = 8
VMEM_LIMIT = 56 * 1024 * 1024


def _cparams(sem):
    return pltpu.CompilerParams(dimension_semantics=sem, vmem_limit_bytes=VMEM_LIMIT)


def _rms(x):
    return x * lax.rsqrt(jnp.mean(x * x, axis=-1, keepdims=True) + EPS)


def _sigmoid(x):
    return 1.0 / (1.0 + jnp.exp(-x))


def _softplus(x):
    return jnp.maximum(x, 0.0) + jnp.log(1.0 + jnp.exp(-jnp.abs(x)))


def _ffn_kernel(x_ref, g_ref, wg_ref, wu_ref, wd_ref, fg_ref, o_ref, h_scr, acc_scr, *, final_norm):
    j = pl.program_id(1)

    @pl.when(j == 0)
    def _():
        h_scr[...] = (_rms(x_ref[...]) * g_ref[...]).astype(BF16)
        acc_scr[...] = jnp.zeros_like(acc_scr)

    h = h_scr[...]
    gate = jnp.dot(h, wg_ref[...], preferred_element_type=F32)
    up = jnp.dot(h, wu_ref[...], preferred_element_type=F32)
    act = (gate * _sigmoid(gate) * up).astype(BF16)
    acc_scr[...] += jnp.dot(act, wd_ref[...], preferred_element_type=F32)

    @pl.when(j == pl.num_programs(1) - 1)
    def _():
        y = x_ref[...] + 0.5 * acc_scr[...]
        if final_norm:
            y = _rms(y) * fg_ref[...]
        o_ref[...] = y


def _ffn(x, g, wg, wu, wd, fg, *, final_norm, tm=1024, tf=256):
    m, d = x.shape
    f = wg.shape[1]
    assert m % tm == 0 and f % tf == 0
    return pl.pallas_call(
        functools.partial(_ffn_kernel, final_norm=final_norm),
        grid=(m // tm, f // tf),
        in_specs=[
            pl.BlockSpec((tm, d), lambda i, j: (i, 0)),
            pl.BlockSpec((1, d), lambda i, j: (0, 0)),
            pl.BlockSpec((d, tf), lambda i, j: (0, j)),
            pl.BlockSpec((d, tf), lambda i, j: (0, j)),
            pl.BlockSpec((tf, d), lambda i, j: (j, 0)),
            pl.BlockSpec((1, d), lambda i, j: (0, 0)),
        ],
        out_specs=pl.BlockSpec((tm, d), lambda i, j: (i, 0)),
        out_shape=jax.ShapeDtypeStruct((m, d), F32),
        scratch_shapes=[pltpu.VMEM((tm, d), BF16), pltpu.VMEM((tm, d), F32)],
        compiler_params=_cparams(("parallel", "arbitrary")),
        name="ffn_final" if final_norm else "ffn",
    )(x, g.reshape(1, d), wg, wu, wd, fg.reshape(1, d))


def _inproj_kernel(x_ref, g_ref, w_ref, p_ref, ba_ref, *, n_sub):
    tm, n = p_ref.shape
    sub = tm // n_sub
    g = g_ref[...]
    for r in range(n_sub):
        rows = slice(r * sub, (r + 1) * sub)
        h = (_rms(x_ref[rows, :]) * g).astype(BF16)
        p = jnp.dot(h, w_ref[...], preferred_element_type=F32)
        p_ref[rows, :] = p[:, :n]
        ba_ref[rows, :] = p[:, n:]


def _in_proj(x, g, w_main, w_ba, *, tm=512, n_sub=2):
    m, d = x.shape
    n = w_main.shape[1]
    w_all = jnp.concatenate([w_main, w_ba], axis=1)
    assert m % tm == 0 and tm % n_sub == 0
    return pl.pallas_call(
        functools.partial(_inproj_kernel, n_sub=n_sub),
        grid=(m // tm,),
        in_specs=[
            pl.BlockSpec((tm, d), lambda i: (i, 0)),
            pl.BlockSpec((1, d), lambda i: (0, 0)),
            pl.BlockSpec((d, n + LANES), lambda i: (0, 0)),
        ],
        out_specs=[
            pl.BlockSpec((tm, n), lambda i: (i, 0)),
            pl.BlockSpec((tm, LANES), lambda i: (i, 0)),
        ],
        out_shape=[jax.ShapeDtypeStruct((m, n), F32), jax.ShapeDtypeStruct((m, LANES), F32)],
        compiler_params=_cparams(("parallel",)),
        name="in_proj",
    )(x, g.reshape(1, d), w_all)


def _conv_tile(x_ref, cw, r0, rt, first, last, seq):
    x0 = x_ref[pl.ds(r0, rt), :]
    prev = x_ref[pl.ds(jnp.maximum(r0 - SUBLANES, 0), SUBLANES), :]
    prev = jnp.where(first, 0.0, prev)
    nxt = x_ref[pl.ds(jnp.minimum(r0 + rt, seq - SUBLANES), SUBLANES), :]
    nxt = jnp.where(last, 0.0, nxt)
    w = jnp.concatenate([prev, x0, nxt], axis=0)
    acc = cw[0:1, :] * w[SUBLANES - 2:SUBLANES - 2 + rt]
    acc = acc + cw[1:2, :] * w[SUBLANES - 1:SUBLANES - 1 + rt]
    acc = acc + cw[2:3, :] * x0
    acc = acc + cw[3:4, :] * w[SUBLANES + 1:SUBLANES + 1 + rt]
    return acc


RG_SEGS = 2 * SUBLANES
RG_PAD = 8


def _rglru_kernel(x_ref, gate_ref, cw_ref, cb_ref, wg_ref, bg_ref, lam_ref, o_ref,
                  af_scr, bf_scr, ab_scr, bb_scr, *, seq, rt):
    seg_len = seq // RG_SEGS
    pitch = seg_len + RG_PAD
    tiles_per_seg = seg_len // rt
    n_tiles = seq // rt
    a_scr = (af_scr, ab_scr)
    b_scr = (bf_scr, bb_scr)

    cw = cw_ref[...]
    cb = cb_ref[...]
    bg = bg_ref[...]
    neg_c_sp = -RG_C * _softplus(-lam_ref[...])

    def gates(i, _):
        r0 = pl.multiple_of(i * rt, rt)
        xc = _conv_tile(x_ref, cw, r0, rt, i == 0, i == n_tiles - 1, seq) + cb
        pre = jnp.dot(xc.astype(BF16), wg_ref[...], preferred_element_type=F32) + bg
        seg = i // tiles_per_seg
        off = pl.multiple_of(seg * pitch + (i - seg * tiles_per_seg) * rt, SUBLANES)
        for d in range(2):
            r = _sigmoid(pre[:, (2 * d) * LANES:(2 * d + 1) * LANES])
            ig = _sigmoid(pre[:, (2 * d + 1) * LANES:(2 * d + 2) * LANES])
            log_a = r * neg_c_sp[d:d + 1, :]
            a = jnp.exp(log_a)
            om = 1.0 - a * a
            b = jnp.where(om > 0.0, om * lax.rsqrt(om), 0.0) * (ig * xc)
            a_scr[d][pl.ds(off, rt), :] = a
            b_scr[d][pl.ds(off, rt), :] = b
        return 0

    lax.fori_loop(0, n_tiles, gates, 0)

    def scan(t, carry):
        hf, pf, hb, pb = carry
        idx = pl.ds(t, RG_SEGS, stride=pitch)
        a = af_scr[idx, :]
        hf = a * hf + bf_scr[idx, :]
        pf = a * pf
        bf_scr[idx, :] = hf
        af_scr[idx, :] = pf
        idx = pl.ds(seg_len - 1 - t, RG_SEGS, stride=pitch)
        a = ab_scr[idx, :]
        hb = a * hb + bb_scr[idx, :]
        pb = a * pb
        bb_scr[idx, :] = hb
        ab_scr[idx, :] = pb
        return hf, pf, hb, pb

    zeros = jnp.zeros((RG_SEGS, LANES), F32)
    ones = jnp.ones((RG_SEGS, LANES), F32)
    hf, pf, hb, pb = lax.fori_loop(0, seg_len, scan, (zeros, ones, zeros, ones), unroll=8)

    cf = [jnp.zeros((1, LANES), F32)]
    for s in range(1, RG_SEGS):
        cf.append(hf[s - 1:s, :] + pf[s - 1:s, :] * cf[s - 1])
    cbk = [None] * RG_SEGS
    cbk[RG_SEGS - 1] = jnp.zeros((1, LANES), F32)
    for s in range(RG_SEGS - 2, -1, -1):
        cbk[s] = hb[s + 1:s + 2, :] + pb[s + 1:s + 2, :] * cbk[s + 1]

    for s in range(RG_SEGS):
        def fix(k, _, s=s):
            off = pl.multiple_of(s * pitch + k * rt, SUBLANES)
            rows = pl.multiple_of(s * seg_len + k * rt, SUBLANES)
            h = bf_scr[pl.ds(off, rt), :] + af_scr[pl.ds(off, rt), :] * cf[s]
            h = h + (bb_scr[pl.ds(off, rt), :] + ab_scr[pl.ds(off, rt), :] * cbk[s])
            o_ref[pl.ds(rows, rt), :] = h * jax.nn.gelu(gate_ref[pl.ds(rows, rt), :], approximate=True)
            return 0

        lax.fori_loop(0, tiles_per_seg, fix, 0)


def _rglru(p, conv_w, conv_b, w_gates, b_gates, lam, *, batch, seq, gate_col0, rt=512):
    ngrp = conv_w.shape[1] // LANES
    seg_len = seq // RG_SEGS
    rt = min(rt, seg_len)
    assert seq % RG_SEGS == 0 and seg_len % rt == 0
    scr = pltpu.VMEM((RG_SEGS * (seg_len + RG_PAD), LANES), F32)
    return pl.pallas_call(
        functools.partial(_rglru_kernel, seq=seq, rt=rt),
        grid=(batch, ngrp),
        in_specs=[
            pl.BlockSpec((seq, LANES), lambda b, c: (b, c)),
            pl.BlockSpec((seq, LANES), lambda b, c: (b, gate_col0 + c)),
            pl.BlockSpec((4, LANES), lambda b, c: (0, c)),
            pl.BlockSpec((1, LANES), lambda b, c: (0, c)),
            pl.BlockSpec((None, LANES, 4 * LANES), lambda b, c: (c, 0, 0)),
            pl.BlockSpec((None, 1, 4 * LANES), lambda b, c: (c, 0, 0)),
            pl.BlockSpec((2, LANES), lambda b, c: (0, c)),
        ],
        out_specs=pl.BlockSpec((seq, LANES), lambda b, c: (b, c)),
        out_shape=jax.ShapeDtypeStruct((batch * seq, ngrp * LANES), F32),
        scratch_shapes=[scr, scr, scr, scr],
        compiler_params=_cparams(("parallel", "parallel")),
        name="rglru",
    )(p, p, conv_w, conv_b.reshape(1, -1), w_gates, b_gates, lam)


def _rg_gate_weights(wa, ba, wx, bx):
    ndir, nblk, blk, _ = wa.shape
    ngrp = nblk * blk // LANES
    per = LANES // blk

    def bd(w):
        w = w.reshape(ndir, ngrp, per, blk, blk)
        eye = jnp.eye(per, dtype=w.dtype)
        return jnp.einsum("dgpij,pq->dgpiqj", w, eye).reshape(ndir, ngrp, LANES, LANES)

    a, x = bd(wa), bd(wx)
    w = jnp.concatenate([a[0], x[0], a[1], x[1]], axis=-1)
    ba = ba.reshape(ndir, ngrp, 1, LANES)
    bx = bx.reshape(ndir, ngrp, 1, LANES)
    b = jnp.concatenate([ba[0], bx[0], ba[1], bx[1]], axis=-1)
    return w.astype(BF16), b.astype(F32)


def _qkv_kernel(x_ref, cw_ref, o_ref, *, seq, rt, n_qk, n_q, q_scale):
    c = pl.program_id(1)
    cw = cw_ref[...]
    n_tiles = seq // rt
    is_qk = c < n_qk
    scale = jnp.where(c < n_q, q_scale, 1.0).astype(F32)

    def body(i, _):
        r0 = pl.multiple_of(i * rt, rt)
        y = _conv_tile(x_ref, cw, r0, rt, i == 0, i == n_tiles - 1, seq)
        y = y * _sigmoid(y)
        nrm = y * (lax.rsqrt(jnp.sum(y * y, axis=-1, keepdims=True) + EPS) * scale)
        o_ref[pl.ds(r0, rt), :] = jnp.where(is_qk, nrm, y)
        return 0

    lax.fori_loop(0, n_tiles, body, 0)


def _qkv_prep(p, conv_w, *, batch, seq, col0, rt=512):
    ngrp = conv_w.shape[1] // LANES
    return pl.pallas_call(
        functools.partial(_qkv_kernel, seq=seq, rt=rt, n_qk=2 * GDN_HEADS, n_q=GDN_HEADS,
                          q_scale=HEAD_DIM ** -0.5),
        grid=(batch, ngrp),
        in_specs=[
            pl.BlockSpec((seq, LANES), lambda b, c: (b, col0 + c)),
            pl.BlockSpec((4, LANES), lambda b, c: (0, c)),
        ],
        out_specs=pl.BlockSpec((seq, LANES), lambda b, c: (b, c)),
        out_shape=jax.ShapeDtypeStruct((batch * seq, ngrp * LANES), F32),
        compiler_params=_cparams(("parallel", "parallel")),
        name="qkv_prep",
    )(p, conv_w)


def _gate_kernel(ba_ref, alog_ref, dtb_ref, col_ref, row_ref, *, tm):
    nh = GDN_HEADS
    ri = lax.broadcasted_iota(jnp.int32, (LANES, LANES), 0)
    ci = lax.broadcasted_iota(jnp.int32, (LANES, LANES), 1)
    same = (ri // CHUNK) == (ci // CHUNK)
    lower = jnp.where(same & (ri >= ci), 1.0, 0.0).astype(F32)
    upper = jnp.where(same & (ri <= ci), 1.0, 0.0).astype(F32)
    lane = ci
    neg_a = -jnp.exp(alog_ref[...])
    dtb = dtb_ref[...]
    for k in range(tm // LANES):
        sl = slice(k * LANES, (k + 1) * LANES)
        raw = ba_ref[sl, :]
        beta = _sigmoid(raw)
        gk = neg_a * _softplus(raw + dtb)
        gk = jnp.where((lane >= 2 * nh) & (lane < 4 * nh), gk, 0.0)
        cf = jnp.dot(lower, gk, preferred_element_type=F32, precision=lax.Precision.HIGHEST)
        cb = jnp.dot(upper, gk, preferred_element_type=F32, precision=lax.Precision.HIGHEST)
        gc = jnp.where(lane < 3 * nh, cf, cb)
        col = jnp.where(lane < 2 * nh, beta, gc)
        col_ref[sl, :] = col
        colt = col.T
        for q in range(LANES // CHUNK):
            rows = [jnp.concatenate([colt[(2 + dr) * nh + h:(2 + dr) * nh + h + 1, q * CHUNK:(q + 1) * CHUNK]
                                     for h in range(nh)], axis=1) for dr in range(2)]
            rows.append(jnp.zeros((SUBLANES - 2, nh * CHUNK), F32))
            row_ref[k * (LANES // CHUNK) + q] = jnp.concatenate(rows, axis=0)


def _gate_prep(ba, a_log, dt_bias, *, tm=512):
    m = ba.shape[0]
    nh = GDN_HEADS
    pad = lambda v: jnp.zeros((1, LANES), F32).at[0, 2 * nh:4 * nh].set(v.reshape(-1).astype(F32))
    return pl.pallas_call(
        functools.partial(_gate_kernel, tm=tm),
        grid=(m // tm,),
        in_specs=[
            pl.BlockSpec((tm, LANES), lambda i: (i, 0)),
            pl.BlockSpec((1, LANES), lambda i: (0, 0)),
            pl.BlockSpec((1, LANES), lambda i: (0, 0)),
        ],
        out_specs=[
            pl.BlockSpec((tm, LANES), lambda i: (i, 0)),
            pl.BlockSpec((tm // CHUNK, SUBLANES, nh * CHUNK), lambda i: (i, 0, 0)),
        ],
        out_shape=[jax.ShapeDtypeStruct((m, LANES), F32),
                   jax.ShapeDtypeStruct((m // CHUNK, SUBLANES, nh * CHUNK), F32)],
        compiler_params=_cparams(("parallel",)),
        name="gate_prep",
    )(ba, pad(a_log), pad(dt_bias))


def _bmm(a, b):
    return lax.dot_general(a.astype(BF16), b.astype(BF16), (((2,), (1,)), ((0,), (0,))),
                           preferred_element_type=F32)


def _bmm_nt(a, b):
    return lax.dot_general(a.astype(BF16), b.astype(BF16), (((2,), (2,)), ((0,), (0,))),
                           preferred_element_type=F32)


def _bmm_tn(a, b):
    return lax.dot_general(a.astype(BF16), b.astype(BF16), (((1,), (1,)), ((0,), (0,))),
                           preferred_element_type=F32)


def _unit_tri_inverse_levels(lm, ri, ci, bdmask, nb):
    def blockdiag(a):
        ab = a.astype(BF16)
        return jnp.where(bdmask, jnp.concatenate([ab] * nb, axis=1), jnp.zeros((), BF16))

    eye = jnp.where(ri == ci, 1.0, 0.0).astype(F32)
    x = eye - jnp.where((ri // 2) == (ci // 2), lm, 0.0)
    s = 2
    while s < CHUNK:
        cm = jnp.where(((ri // (2 * s)) == (ci // (2 * s))) & ((ri // s) != (ci // s)), lm, 0.0)
        x = x - _bmm(_bmm(x, blockdiag(cm)), blockdiag(x))
        yield x
        s *= 2


def _gdn_chunk_local(q_ref, k_ref, v_ref, col_ref, row_ref, dst, *, reverse, g, d):
    u_scr, wq_scr, kd_scr, at_scr, cd_scr = dst
    nh, c, hd = GDN_HEADS, CHUNK, HEAD_DIM
    pw = nh * c
    fw = nh * hd
    ri = lax.broadcasted_iota(jnp.int32, (1, c, pw), 1)
    ci = lax.broadcasted_iota(jnp.int32, (1, c, pw), 2) % c
    if reverse:
        incl, strict, last = ri <= ci, ri < ci, 0
    else:
        incl, strict, last = ri >= ci, ri > ci, c - 1
    bdmask = (lax.broadcasted_iota(jnp.int32, (1, pw, pw), 1) // c
              == lax.broadcasted_iota(jnp.int32, (1, pw, pw), 2) // c)
    kmask = (lax.broadcasted_iota(jnp.int32, (1, pw, fw), 1) // c
             == lax.broadcasted_iota(jnp.int32, (1, pw, fw), 2) // hd)
    low_half = lax.broadcasted_iota(jnp.int32, (g * c, hd), 1) < c

    col = col_ref[...]
    bcast = lambda j: jnp.broadcast_to(col[:, j:j + 1], (g * c, hd))
    beta = jnp.concatenate([bcast(d * nh + h) for h in range(nh)], axis=1).reshape(g, c, fw)
    gcs = [bcast((2 + d) * nh + h) for h in range(nh)]
    gc = jnp.concatenate(gcs, axis=1).reshape(g, c, fw)
    gc_col = jnp.concatenate([jnp.where(low_half, gcs[h], gcs[h + 1]) for h in range(0, nh, 2)],
                             axis=1).reshape(g, c, pw)
    gc_row = row_ref[:, d:d + 1, :]
    g_last = gc[:, last:last + 1, :]
    eg = jnp.exp(gc)
    decay = jnp.where(incl, jnp.exp(gc_col - gc_row), 0.0)
    q = q_ref[...].reshape(g, c, fw)
    k = k_ref[...].reshape(g, c, fw)
    v = v_ref[...].reshape(g, c, fw)
    k_beta = k * beta
    v_beta = v * beta
    kb16 = k.astype(BF16)
    k_bd = jnp.where(kmask, jnp.concatenate([kb16] * nh, axis=1), jnp.zeros((), BF16))
    kq = _bmm_nt(jnp.concatenate([k_beta, q], axis=1), k_bd)
    lm = jnp.where(strict, kq[:, :c] * decay, 0.0)
    attn = kq[:, c:] * decay
    yield
    for t in _unit_tri_inverse_levels(lm, ri, ci, bdmask, nh):
        yield
    kbe = k_beta * eg
    qd = q * eg
    kd = k * jnp.exp(g_last - gc)
    cd = jnp.exp(g_last)
    for h in range(nh):
        hs = slice(h * hd, (h + 1) * hd)
        ps = slice(h * c, (h + 1) * c)
        uw = _bmm(t[:, :, ps], jnp.concatenate([v_beta[:, :, hs], kbe[:, :, hs]], axis=-1))
        u_scr[h] = uw[:, :, :hd]
        wq_scr[h] = jnp.concatenate([uw[:, :, hd:], qd[:, :, hs]], axis=1).astype(BF16)
        at_scr[h] = attn[:, :, ps].astype(BF16)
        kd_scr[h] = kd[:, :, hs].astype(BF16)
        cd_scr[h] = cd[:, :, hs]
        if h % 2 == 1:
            yield


def _gdn_recurrence(src, s_scr, o_ref, *, reverse, g):
    u_scr, wq_scr, kd_scr, at_scr, cd_scr = src
    nh, c, hd = GDN_HEADS, CHUNK, HEAD_DIM
    for step in range(g):
        n = g - 1 - step if reverse else step
        state = s_scr[...]
        sb = state.astype(BF16)
        ws = _bmm(wq_scr[:, n], sb)
        v_new = u_scr[:, n] - ws[:, :c]
        vb = v_new.astype(BF16)
        o = ws[:, c:] + _bmm(at_scr[:, n], vb)
        s_scr[...] = state * cd_scr[:, n] + _bmm_tn(kd_scr[:, n], vb)
        for h in range(nh):
            o_ref[n * c:(n + 1) * c, h * hd:(h + 1) * hd] = o[h]
        yield


def _gdn_kernel(q_ref, k_ref, v_ref, col_ref, row_ref, o_ref, s_scr, *sets, reverse, n_chunks, d):
    set0, set1 = sets[:len(sets) // 2], sets[len(sets) // 2:]
    i = pl.program_id(1)

    @pl.when(i == 0)
    def _():
        s_scr[...] = jnp.zeros_like(s_scr)
        for r in set1:
            r[...] = jnp.zeros_like(r)

    def step(dst, src):
        local = _gdn_chunk_local(q_ref, k_ref, v_ref, col_ref, row_ref, dst, reverse=reverse, g=n_chunks, d=d)
        recur = _gdn_recurrence(src, s_scr, o_ref, reverse=reverse, g=n_chunks)
        for _ in itertools.zip_longest(local, recur):
            pass

    @pl.when(i % 2 == 0)
    def _():
        step(set0, set1)

    @pl.when(i % 2 == 1)
    def _():
        step(set1, set0)


def _gdn(qkv, col, row, *, batch, seq, d, tt=512):
    nh = GDN_HEADS
    width = nh * HEAD_DIM
    nt = seq // tt
    n_chunks = tt // CHUNK
    reverse = d == 1
    order = (lambda t: nt - 1 - t) if reverse else (lambda t: t)
    tile_in = lambda b, i: b * nt + order(jnp.minimum(i, nt - 1))
    tile_out = lambda b, i: b * nt + order(jnp.maximum(i - 1, 0))
    scratch_set = [
        pltpu.VMEM((nh, n_chunks, CHUNK, HEAD_DIM), F32),
        pltpu.VMEM((nh, n_chunks, 2 * CHUNK, HEAD_DIM), BF16),
        pltpu.VMEM((nh, n_chunks, CHUNK, HEAD_DIM), BF16),
        pltpu.VMEM((nh, n_chunks, CHUNK, CHUNK), BF16),
        pltpu.VMEM((nh, n_chunks, 1, HEAD_DIM), F32),
    ]
    return pl.pallas_call(
        functools.partial(_gdn_kernel, reverse=reverse, n_chunks=n_chunks, d=d),
        grid=(batch, nt + 1),
        in_specs=[
            pl.BlockSpec((tt, width), lambda b, i: (tile_in(b, i), 0)),
            pl.BlockSpec((tt, width), lambda b, i: (tile_in(b, i), 1)),
            pl.BlockSpec((tt, width), lambda b, i: (tile_in(b, i), 2)),
            pl.BlockSpec((tt, LANES), lambda b, i: (tile_in(b, i), 0)),
            pl.BlockSpec((n_chunks, SUBLANES, nh * CHUNK), lambda b, i: (tile_in(b, i), 0, 0)),
        ],
        out_specs=pl.BlockSpec((tt, width), lambda b, i: (tile_out(b, i), 0)),
        out_shape=jax.ShapeDtypeStruct((batch * seq, width), F32),
        scratch_shapes=[pltpu.VMEM((nh, HEAD_DIM, HEAD_DIM), F32)] + scratch_set + scratch_set,
        compiler_params=_cparams(("parallel", "arbitrary")),
        name="gdn_bwd" if reverse else "gdn_fwd",
    )(qkv, qkv, qkv, col, row)


def _outproj_kernel(yrg_ref, of_ref, ob_ref, z_ref, gn_ref, w_ref, x_ref, o_ref):
    o = of_ref[...] + ob_ref[...]
    z = z_ref[...]
    gn = gn_ref[...]
    parts = [yrg_ref[...].astype(BF16)]
    for h in range(GDN_HEADS):
        hs = slice(h * HEAD_DIM, (h + 1) * HEAD_DIM)
        zh = z[:, hs]
        parts.append((_rms(o[:, hs]) * gn * (zh * _sigmoid(zh))).astype(BF16))
    y = jnp.concatenate(parts, axis=-1)
    o_ref[...] = x_ref[...] + jnp.dot(y, w_ref[...], preferred_element_type=F32)


def _out_proj(y_rg, o_f, o_b, p, gdn_norm, w_out, x, *, z_col, tm=512):
    m, d = x.shape
    wr = y_rg.shape[1]
    wg = o_f.shape[1]
    return pl.pallas_call(
        _outproj_kernel,
        grid=(m // tm,),
        in_specs=[
            pl.BlockSpec((tm, wr), lambda i: (i, 0)),
            pl.BlockSpec((tm, wg), lambda i: (i, 0)),
            pl.BlockSpec((tm, wg), lambda i: (i, 0)),
            pl.BlockSpec((tm, wg), lambda i: (i, z_col)),
            pl.BlockSpec((1, HEAD_DIM), lambda i: (0, 0)),
            pl.BlockSpec((wr + wg, d), lambda i: (0, 0)),
            pl.BlockSpec((tm, d), lambda i: (i, 0)),
        ],
        out_specs=pl.BlockSpec((tm, d), lambda i: (i, 0)),
        out_shape=jax.ShapeDtypeStruct((m, d), F32),
        compiler_params=_cparams(("parallel",)),
        name="out_proj",
    )(y_rg, o_f, o_b, p, gdn_norm.reshape(1, -1), w_out, x)


def _mixer(x1, batch, seq, mix_norm, w_in, w_out, rg_conv_w, rg_conv_b, rg_gate_a_w, rg_gate_a_b,
           rg_gate_x_w, rg_gate_x_b, rg_lambda, gdn_conv_w, gdn_a_log, gdn_dt_bias, gdn_norm):
    rg_w = rg_conv_w.shape[1]
    qkv_w = gdn_conv_w.shape[1]
    gdn_vw = GDN_HEADS * HEAD_DIM
    n_main = 2 * rg_w + qkv_w + gdn_vw
    n_gate = w_in.shape[1] - n_main
    w_main = w_in[:, :n_main].astype(BF16)
    w_ba = jnp.zeros((w_in.shape[0], LANES), F32).at[:, :n_gate].set(w_in[:, n_main:]).astype(BF16)
    p, ba = _in_proj(x1, mix_norm, w_main, w_ba)

    w_gates, b_gates = _rg_gate_weights(rg_gate_a_w, rg_gate_a_b, rg_gate_x_w, rg_gate_x_b)
    y_rg = _rglru(p, rg_conv_w, rg_conv_b, w_gates, b_gates, rg_lambda.astype(F32),
                  batch=batch, seq=seq, gate_col0=rg_w // LANES)

    qkv = _qkv_prep(p, gdn_conv_w, batch=batch, seq=seq, col0=2 * rg_w // LANES)
    col, row = _gate_prep(ba, gdn_a_log, gdn_dt_bias)
    o_f = _gdn(qkv, col, row, batch=batch, seq=seq, d=0)
    o_b = _gdn(qkv, col, row, batch=batch, seq=seq, d=1)
    return _out_proj(y_rg, o_f, o_b, p, gdn_norm, w_out.astype(BF16), x1,
                     z_col=(2 * rg_w + qkv_w) // gdn_vw)


def kernel(x, ffn1_norm, ffn1_w_gate, ffn1_w_up, ffn1_w_down, mix_norm, w_in, w_out, rg_conv_w, rg_conv_b, rg_gate_a_w, rg_gate_a_b, rg_gate_x_w, rg_gate_x_b, rg_lambda, gdn_conv_w, gdn_a_log, gdn_dt_bias, gdn_norm, ffn2_norm, ffn2_w_gate, ffn2_w_up, ffn2_w_down, final_norm):
    batch, seq, d_model = x.shape
    depth = ffn1_norm.shape[0]
    h = x.reshape(batch * seq, d_model)
    for l in range(depth):
        last = l == depth - 1
        h = _ffn(h, ffn1_norm[l], ffn1_w_gate[l].astype(BF16), ffn1_w_up[l].astype(BF16),
                 ffn1_w_down[l].astype(BF16), final_norm, final_norm=False)
        h = _mixer(h, batch, seq, mix_norm[l], w_in[l], w_out[l], rg_conv_w[l], rg_conv_b[l],
                   rg_gate_a_w[l], rg_gate_a_b[l], rg_gate_x_w[l], rg_gate_x_b[l], rg_lambda[l],
                   gdn_conv_w[l], gdn_a_log[l], gdn_dt_bias[l], gdn_norm[l])
        h = _ffn(h, ffn2_norm[l], ffn2_w_gate[l].astype(BF16), ffn2_w_up[l].astype(BF16),
                 ffn2_w_down[l].astype(BF16), final_norm, final_norm=last)
    return h.reshape(batch, seq, d_model)
```

```python
import functools
import itertools

import jax
import jax.numpy as jnp
from jax import lax
from jax.experimental import pallas as pl
from jax.experimental.pallas import tpu as pltpu

F32 = jnp.float32
BF16 = jnp.bfloat16

EPS = 1e-6
RG_C = 8.0
RG_BLOCKS = 8
GDN_HEADS = 4
HEAD_DIM = 128
CHUNK = 64
LANES = 128
SUBLANES = 8
VMEM_LIMIT = 56 * 1024 * 1024


def _cparams(sem):
    return pltpu.CompilerParams(dimension_semantics=sem, vmem_limit_bytes=VMEM_LIMIT)


def _rms(x):
    return x * lax.rsqrt(jnp.mean(x * x, axis=-1, keepdims=True) + EPS)


def _sigmoid(x):
    return 1.0 / (1.0 + jnp.exp(-x))


def _softplus(x):
    return jnp.maximum(x, 0.0) + jnp.log(1.0 + jnp.exp(-jnp.abs(x)))


def _ffn_kernel(x_ref, g_ref, wg_ref, wu_ref, wd_ref, fg_ref, o_ref, *, final_norm, n_sub):
    sub = x_ref.shape[0] // n_sub
    g = g_ref[...]
    for r in range(n_sub):
        rows = slice(r * sub, (r + 1) * sub)
        x = x_ref[rows, :]
        h = (_rms(x) * g).astype(BF16)
        gate = jnp.dot(h, wg_ref[...], preferred_element_type=F32)
        up = jnp.dot(h, wu_ref[...], preferred_element_type=F32)
        act = (gate * _sigmoid(gate) * up).astype(BF16)
        y = x + 0.5 * jnp.dot(act, wd_ref[...], preferred_element_type=F32)
        if final_norm:
            y = _rms(y) * fg_ref[...]
        o_ref[rows, :] = y


def _ffn(x, g, wg, wu, wd, fg, *, final_norm, tm=512, n_sub=2):
    m, d = x.shape
    f = wg.shape[1]
    assert m % tm == 0 and tm % n_sub == 0
    resident = dict(pipeline_mode=pl.Buffered(1))
    return pl.pallas_call(
        functools.partial(_ffn_kernel, final_norm=final_norm, n_sub=n_sub),
        grid=(m // tm,),
        in_specs=[
            pl.BlockSpec((tm, d), lambda i: (i, 0)),
            pl.BlockSpec((1, d), lambda i: (0, 0)),
            pl.BlockSpec((d, f), lambda i: (0, 0), **resident),
            pl.BlockSpec((d, f), lambda i: (0, 0), **resident),
            pl.BlockSpec((f, d), lambda i: (0, 0), **resident),
            pl.BlockSpec((1, d), lambda i: (0, 0)),
        ],
        out_specs=pl.BlockSpec((tm, d), lambda i: (i, 0)),
        out_shape=jax.ShapeDtypeStruct((m, d), F32),
        compiler_params=_cparams(("parallel",)),
        name="ffn_final" if final_norm else "ffn",
    )(x, g.reshape(1, d), wg, wu, wd, fg.reshape(1, d))


def _inproj_kernel(x_ref, g_ref, w_ref, p_ref, ba_ref, *, n_sub):
    tm, n = p_ref.shape
    sub = tm // n_sub
    g = g_ref[...]
    for r in range(n_sub):
        rows = slice(r * sub, (r + 1) * sub)
        h = (_rms(x_ref[rows, :]) * g).astype(BF16)
        p = jnp.dot(h, w_ref[...], preferred_element_type=F32)
        p_ref[rows, :] = p[:, :n]
        ba_ref[rows, :] = p[:, n:]


def _in_proj(x, g, w_main, w_ba, *, tm=512, n_sub=2):
    m, d = x.shape
    n = w_main.shape[1]
    w_all = jnp.concatenate([w_main, w_ba], axis=1)
    assert m % tm == 0 and tm % n_sub == 0
    return pl.pallas_call(
        functools.partial(_inproj_kernel, n_sub=n_sub),
        grid=(m // tm,),
        in_specs=[
            pl.BlockSpec((tm, d), lambda i: (i, 0)),
            pl.BlockSpec((1, d), lambda i: (0, 0)),
            pl.BlockSpec((d, n + LANES), lambda i: (0, 0)),
        ],
        out_specs=[
            pl.BlockSpec((tm, n), lambda i: (i, 0)),
            pl.BlockSpec((tm, LANES), lambda i: (i, 0)),
        ],
        out_shape=[jax.ShapeDtypeStruct((m, n), F32), jax.ShapeDtypeStruct((m, LANES), F32)],
        compiler_params=_cparams(("parallel",)),
        name="in_proj",
    )(x, g.reshape(1, d), w_all)


def _conv_tile(x_ref, cw, r0, rt, first, last, seq):
    x0 = x_ref[pl.ds(r0, rt), :]
    prev = x_ref[pl.ds(jnp.maximum(r0 - SUBLANES, 0), SUBLANES), :]
    prev = jnp.where(first, 0.0, prev)
    nxt = x_ref[pl.ds(jnp.minimum(r0 + rt, seq - SUBLANES), SUBLANES), :]
    nxt = jnp.where(last, 0.0, nxt)
    w = jnp.concatenate([prev, x0, nxt], axis=0)
    acc = cw[0:1, :] * w[SUBLANES - 2:SUBLANES - 2 + rt]
    acc = acc + cw[1:2, :] * w[SUBLANES - 1:SUBLANES - 1 + rt]
    acc = acc + cw[2:3, :] * x0
    acc = acc + cw[3:4, :] * w[SUBLANES + 1:SUBLANES + 1 + rt]
    return acc


RG_SEGS = 2 * SUBLANES
RG_PAD = 8


def _rglru_kernel(x_ref, gate_ref, cw_ref, cb_ref, wg_ref, bg_ref, lam_ref, o_ref,
                  af_scr, bf_scr, ab_scr, bb_scr, *, seq, rt):
    seg_len = seq // RG_SEGS
    pitch = seg_len + RG_PAD
    tiles_per_seg = seg_len // rt
    n_tiles = seq // rt
    a_scr = (af_scr, ab_scr)
    b_scr = (bf_scr, bb_scr)

    cw = cw_ref[...]
    cb = cb_ref[...]
    bg = bg_ref[...]
    neg_c_sp = -RG_C * _softplus(-lam_ref[...])

    def gates(i, _):
        r0 = pl.multiple_of(i * rt, rt)
        xc = _conv_tile(x_ref, cw, r0, rt, i == 0, i == n_tiles - 1, seq) + cb
        pre = jnp.dot(xc.astype(BF16), wg_ref[...], preferred_element_type=F32) + bg
        seg = i // tiles_per_seg
        off = pl.multiple_of(seg * pitch + (i - seg * tiles_per_seg) * rt, SUBLANES)
        for d in range(2):
            r = _sigmoid(pre[:, (2 * d) * LANES:(2 * d + 1) * LANES])
            ig = _sigmoid(pre[:, (2 * d + 1) * LANES:(2 * d + 2) * LANES])
            log_a = r * neg_c_sp[d:d + 1, :]
            a = jnp.exp(log_a)
            om = 1.0 - a * a
            b = jnp.where(om > 0.0, om * lax.rsqrt(om), 0.0) * (ig * xc)
            a_scr[d][pl.ds(off, rt), :] = a
            b_scr[d][pl.ds(off, rt), :] = b
        return 0

    lax.fori_loop(0, n_tiles, gates, 0)

    def scan(t, carry):
        hf, pf, hb, pb = carry
        idx = pl.ds(t, RG_SEGS, stride=pitch)
        a = af_scr[idx, :]
        hf = a * hf + bf_scr[idx, :]
        pf = a * pf
        bf_scr[idx, :] = hf
        af_scr[idx, :] = pf
        idx = pl.ds(seg_len - 1 - t, RG_SEGS, stride=pitch)
        a = ab_scr[idx, :]
        hb = a * hb + bb_scr[idx, :]
        pb = a * pb
        bb_scr[idx, :] = hb
        ab_scr[idx, :] = pb
        return hf, pf, hb, pb

    zeros = jnp.zeros((RG_SEGS, LANES), F32)
    ones = jnp.ones((RG_SEGS, LANES), F32)
    hf, pf, hb, pb = lax.fori_loop(0, seg_len, scan, (zeros, ones, zeros, ones), unroll=8)

    cf = [jnp.zeros((1, LANES), F32)]
    for s in range(1, RG_SEGS):
        cf.append(hf[s - 1:s, :] + pf[s - 1:s, :] * cf[s - 1])
    cbk = [None] * RG_SEGS
    cbk[RG_SEGS - 1] = jnp.zeros((1, LANES), F32)
    for s in range(RG_SEGS - 2, -1, -1):
        cbk[s] = hb[s + 1:s + 2, :] + pb[s + 1:s + 2, :] * cbk[s + 1]

    for s in range(RG_SEGS):
        def fix(k, _, s=s):
            off = pl.multiple_of(s * pitch + k * rt, SUBLANES)
            rows = pl.multiple_of(s * seg_len + k * rt, SUBLANES)
            h = bf_scr[pl.ds(off, rt), :] + af_scr[pl.ds(off, rt), :] * cf[s]
            h = h + (bb_scr[pl.ds(off, rt), :] + ab_scr[pl.ds(off, rt), :] * cbk[s])
            o_ref[pl.ds(rows, rt), :] = h * jax.nn.gelu(gate_ref[pl.ds(rows, rt), :], approximate=True)
            return 0

        lax.fori_loop(0, tiles_per_seg, fix, 0)


def _rglru(p, conv_w, conv_b, w_gates, b_gates, lam, *, batch, seq, gate_col0, rt=512):
    ngrp = conv_w.shape[1] // LANES
    seg_len = seq // RG_SEGS
    rt = min(rt, seg_len)
    assert seq % RG_SEGS == 0 and seg_len % rt == 0
    scr = pltpu.VMEM((RG_SEGS * (seg_len + RG_PAD), LANES), F32)
    return pl.pallas_call(
        functools.partial(_rglru_kernel, seq=seq, rt=rt),
        grid=(batch, ngrp),
        in_specs=[
            pl.BlockSpec((seq, LANES), lambda b, c: (b, c)),
            pl.BlockSpec((seq, LANES), lambda b, c: (b, gate_col0 + c)),
            pl.BlockSpec((4, LANES), lambda b, c: (0, c)),
            pl.BlockSpec((1, LANES), lambda b, c: (0, c)),
            pl.BlockSpec((None, LANES, 4 * LANES), lambda b, c: (c, 0, 0)),
            pl.BlockSpec((None, 1, 4 * LANES), lambda b, c: (c, 0, 0)),
            pl.BlockSpec((2, LANES), lambda b, c: (0, c)),
        ],
        out_specs=pl.BlockSpec((seq, LANES), lambda b, c: (b, c)),
        out_shape=jax.ShapeDtypeStruct((batch * seq, ngrp * LANES), F32),
        scratch_shapes=[scr, scr, scr, scr],
        compiler_params=_cparams(("parallel", "parallel")),
        name="rglru",
    )(p, p, conv_w, conv_b.reshape(1, -1), w_gates, b_gates, lam)


def _rg_gate_weights(wa, ba, wx, bx):
    ndir, nblk, blk, _ = wa.shape
    ngrp = nblk * blk // LANES
    per = LANES // blk

    def bd(w):
        w = w.reshape(ndir, ngrp, per, blk, blk)
        eye = jnp.eye(per, dtype=w.dtype)
        return jnp.einsum("dgpij,pq->dgpiqj", w, eye).reshape(ndir, ngrp, LANES, LANES)

    a, x = bd(wa), bd(wx)
    w = jnp.concatenate([a[0], x[0], a[1], x[1]], axis=-1)
    ba = ba.reshape(ndir, ngrp, 1, LANES)
    bx = bx.reshape(ndir, ngrp, 1, LANES)
    b = jnp.concatenate([ba[0], bx[0], ba[1], bx[1]], axis=-1)
    return w.astype(BF16), b.astype(F32)


def _qkv_kernel(x_ref, cw_ref, o_ref, *, seq, rt, n_qk, n_q, q_scale):
    c = pl.program_id(1)
    cw = cw_ref[...]
    n_tiles = seq // rt
    is_qk = c < n_qk
    scale = jnp.where(c < n_q, q_scale, 1.0).astype(F32)

    def body(i, _):
        r0 = pl.multiple_of(i * rt, rt)
        y = _conv_tile(x_ref, cw, r0, rt, i == 0, i == n_tiles - 1, seq)
        y = y * _sigmoid(y)
        nrm = y * (lax.rsqrt(jnp.sum(y * y, axis=-1, keepdims=True) + EPS) * scale)
        o_ref[pl.ds(r0, rt), :] = jnp.where(is_qk, nrm, y)
        return 0

    lax.fori_loop(0, n_tiles, body, 0)


def _qkv_prep(p, conv_w, *, batch, seq, col0, rt=512):
    ngrp = conv_w.shape[1] // LANES
    return pl.pallas_call(
        functools.partial(_qkv_kernel, seq=seq, rt=rt, n_qk=2 * GDN_HEADS, n_q=GDN_HEADS,
                          q_scale=HEAD_DIM ** -0.5),
        grid=(batch, ngrp),
        in_specs=[
            pl.BlockSpec((seq, LANES), lambda b, c: (b, col0 + c)),
            pl.BlockSpec((4, LANES), lambda b, c: (0, c)),
        ],
        out_specs=pl.BlockSpec((seq, LANES), lambda b, c: (b, c)),
        out_shape=jax.ShapeDtypeStruct((batch * seq, ngrp * LANES), F32),
        compiler_params=_cparams(("parallel", "parallel")),
        name="qkv_prep",
    )(p, conv_w)


def _gate_kernel(ba_ref, alog_ref, dtb_ref, col_ref, row_ref, *, tm):
    nh = GDN_HEADS
    ri = lax.broadcasted_iota(jnp.int32, (LANES, LANES), 0)
    ci = lax.broadcasted_iota(jnp.int32, (LANES, LANES), 1)
    same = (ri // CHUNK) == (ci // CHUNK)
    lower = jnp.where(same & (ri >= ci), 1.0, 0.0).astype(F32)
    upper = jnp.where(same & (ri <= ci), 1.0, 0.0).astype(F32)
    lane = ci
    neg_a = -jnp.exp(alog_ref[...])
    dtb = dtb_ref[...]
    for k in range(tm // LANES):
        sl = slice(k * LANES, (k + 1) * LANES)
        raw = ba_ref[sl, :]
        beta = _sigmoid(raw)
        gk = neg_a * _softplus(raw + dtb)
        gk = jnp.where((lane >= 2 * nh) & (lane < 4 * nh), gk, 0.0)
        cf = jnp.dot(lower, gk, preferred_element_type=F32, precision=lax.Precision.HIGHEST)
        cb = jnp.dot(upper, gk, preferred_element_type=F32, precision=lax.Precision.HIGHEST)
        gc = jnp.where(lane < 3 * nh, cf, cb)
        col = jnp.where(lane < 2 * nh, beta, gc)
        col_ref[sl, :] = col
        colt = col.T
        for q in range(LANES // CHUNK):
            rows = [jnp.concatenate([colt[(2 + dr) * nh + h:(2 + dr) * nh + h + 1, q * CHUNK:(q + 1) * CHUNK]
                                     for h in range(nh)], axis=1) for dr in range(2)]
            rows.append(jnp.zeros((SUBLANES - 2, nh * CHUNK), F32))
            row_ref[k * (LANES // CHUNK) + q] = jnp.concatenate(rows, axis=0)


def _gate_prep(ba, a_log, dt_bias, *, tm=512):
    m = ba.shape[0]
    nh = GDN_HEADS
    pad = lambda v: jnp.zeros((1, LANES), F32).at[0, 2 * nh:4 * nh].set(v.reshape(-1).astype(F32))
    return pl.pallas_call(
        functools.partial(_gate_kernel, tm=tm),
        grid=(m // tm,),
        in_specs=[
            pl.BlockSpec((tm, LANES), lambda i: (i, 0)),
            pl.BlockSpec((1, LANES), lambda i: (0, 0)),
            pl.BlockSpec((1, LANES), lambda i: (0, 0)),
        ],
        out_specs=[
            pl.BlockSpec((tm, LANES), lambda i: (i, 0)),
            pl.BlockSpec((tm // CHUNK, SUBLANES, nh * CHUNK), lambda i: (i, 0, 0)),
        ],
        out_shape=[jax.ShapeDtypeStruct((m, LANES), F32),
                   jax.ShapeDtypeStruct((m // CHUNK, SUBLANES, nh * CHUNK), F32)],
        compiler_params=_cparams(("parallel",)),
        name="gate_prep",
    )(ba, pad(a_log), pad(dt_bias))


def _bmm(a, b):
    return lax.dot_general(a.astype(BF16), b.astype(BF16), (((2,), (1,)), ((0,), (0,))),
                           preferred_element_type=F32)


def _bmm_nt(a, b):
    return lax.dot_general(a.astype(BF16), b.astype(BF16), (((2,), (2,)), ((0,), (0,))),
                           preferred_element_type=F32)


def _bmm_tn(a, b):
    return lax.dot_general(a.astype(BF16), b.astype(BF16), (((1,), (1,)), ((0,), (0,))),
                           preferred_element_type=F32)


def _unit_tri_inverse_levels(lm, ri, ci, bdmask, nb):
    def blockdiag(a):
        ab = a.astype(BF16)
        return jnp.where(bdmask, jnp.concatenate([ab] * nb, axis=1), jnp.zeros((), BF16))

    eye = jnp.where(ri == ci, 1.0, 0.0).astype(F32)
    x = eye - jnp.where((ri // 2) == (ci // 2), lm, 0.0)
    s = 2
    while s < CHUNK:
        cm = jnp.where(((ri // (2 * s)) == (ci // (2 * s))) & ((ri // s) != (ci // s)), lm, 0.0)
        x = x - _bmm(_bmm(x, blockdiag(cm)), blockdiag(x))
        yield x
        s *= 2


def _gdn_chunk_local(q_ref, k_ref, v_ref, col_ref, row_ref, dst, *, reverse, g, d):
    u_scr, wq_scr, kd_scr, at_scr, cd_scr = dst
    nh, c, hd = GDN_HEADS, CHUNK, HEAD_DIM
    pw = nh * c
    fw = nh * hd
    ri = lax.broadcasted_iota(jnp.int32, (1, c, pw), 1)
    ci = lax.broadcasted_iota(jnp.int32, (1, c, pw), 2) % c
    if reverse:
        incl, strict, last = ri <= ci, ri < ci, 0
    else:
        incl, strict, last = ri >= ci, ri > ci, c - 1
    bdmask = (lax.broadcasted_iota(jnp.int32, (1, pw, pw), 1) // c
              == lax.broadcasted_iota(jnp.int32, (1, pw, pw), 2) // c)
    kmask = (lax.broadcasted_iota(jnp.int32, (1, pw, fw), 1) // c
             == lax.broadcasted_iota(jnp.int32, (1, pw, fw), 2) // hd)
    low_half = lax.broadcasted_iota(jnp.int32, (g * c, hd), 1) < c

    col = col_ref[...]
    bcast = lambda j: jnp.broadcast_to(col[:, j:j + 1], (g * c, hd))
    beta = jnp.concatenate([bcast(d * nh + h) for h in range(nh)], axis=1).reshape(g, c, fw)
    gcs = [bcast((2 + d) * nh + h) for h in range(nh)]
    gc = jnp.concatenate(gcs, axis=1).reshape(g, c, fw)
    gc_col = jnp.concatenate([jnp.where(low_half, gcs[h], gcs[h + 1]) for h in range(0, nh, 2)],
                             axis=1).reshape(g, c, pw)
    gc_row = row_ref[:, d:d + 1, :]
    g_last = gc[:, last:last + 1, :]
    eg = jnp.exp(gc)
    decay = jnp.where(incl, jnp.exp(gc_col - gc_row), 0.0)
    q = q_ref[...].reshape(g, c, fw)
    k = k_ref[...].reshape(g, c, fw)
    v = v_ref[...].reshape(g, c, fw)
    k_beta = k * beta
    v_beta = v * beta
    kb16 = k.astype(BF16)
    k_bd = jnp.where(kmask, jnp.concatenate([kb16] * nh, axis=1), jnp.zeros((), BF16))
    kq = _bmm_nt(jnp.concatenate([k_beta, q], axis=1), k_bd)
    lm = jnp.where(strict, kq[:, :c] * decay, 0.0)
    attn = kq[:, c:] * decay
    yield
    for t in _unit_tri_inverse_levels(lm, ri, ci, bdmask, nh):
        yield
    kbe = k_beta * eg
    qd = q * eg
    kd = k * jnp.exp(g_last - gc)
    cd = jnp.exp(g_last)
    for h in range(nh):
        hs = slice(h * hd, (h + 1) * hd)
        ps = slice(h * c, (h + 1) * c)
        uw = _bmm(t[:, :, ps], jnp.concatenate([v_beta[:, :, hs], kbe[:, :, hs]], axis=-1))
        u_scr[h] = uw[:, :, :hd]
        wq_scr[h] = jnp.concatenate([uw[:, :, hd:], qd[:, :, hs]], axis=1).astype(BF16)
        at_scr[h] = attn[:, :, ps].astype(BF16)
        kd_scr[h] = kd[:, :, hs].astype(BF16)
        cd_scr[h] = cd[:, :, hs]
        if h % 2 == 1:
            yield


def _gdn_recurrence(src, s_scr, o_ref, *, reverse, g):
    u_scr, wq_scr, kd_scr, at_scr, cd_scr = src
    nh, c, hd = GDN_HEADS, CHUNK, HEAD_DIM
    for step in range(g):
        n = g - 1 - step if reverse else step
        state = s_scr[...]
        sb = state.astype(BF16)
        ws = _bmm(wq_scr[:, n], sb)
        v_new = u_scr[:, n] - ws[:, :c]
        vb = v_new.astype(BF16)
        o = ws[:, c:] + _bmm(at_scr[:, n], vb)
        s_scr[...] = state * cd_scr[:, n] + _bmm_tn(kd_scr[:, n], vb)
        for h in range(nh):
            o_ref[n * c:(n + 1) * c, h * hd:(h + 1) * hd] = o[h]
        yield


def _gdn_kernel(q_ref, k_ref, v_ref, col_ref, row_ref, o_ref, s_scr, *sets, reverse, n_chunks, d):
    set0, set1 = sets[:len(sets) // 2], sets[len(sets) // 2:]
    i = pl.program_id(1)

    @pl.when(i == 0)
    def _():
        s_scr[...] = jnp.zeros_like(s_scr)
        for r in set1:
            r[...] = jnp.zeros_like(r)

    def step(dst, src):
        local = _gdn_chunk_local(q_ref, k_ref, v_ref, col_ref, row_ref, dst, reverse=reverse, g=n_chunks, d=d)
        recur = _gdn_recurrence(src, s_scr, o_ref, reverse=reverse, g=n_chunks)
        for _ in itertools.zip_longest(local, recur):
            pass

    @pl.when(i % 2 == 0)
    def _():
        step(set0, set1)

    @pl.when(i % 2 == 1)
    def _():
        step(set1, set0)


def _gdn(qkv, col, row, *, batch, seq, d, tt=512):
    nh = GDN_HEADS
    width = nh * HEAD_DIM
    nt = seq // tt
    n_chunks = tt // CHUNK
    reverse = d == 1
    order = (lambda t: nt - 1 - t) if reverse else (lambda t: t)
    tile_in = lambda b, i: b * nt + order(jnp.minimum(i, nt - 1))
    tile_out = lambda b, i: b * nt + order(jnp.maximum(i - 1, 0))
    scratch_set = [
        pltpu.VMEM((nh, n_chunks, CHUNK, HEAD_DIM), F32),
        pltpu.VMEM((nh, n_chunks, 2 * CHUNK, HEAD_DIM), BF16),
        pltpu.VMEM((nh, n_chunks, CHUNK, HEAD_DIM), BF16),
        pltpu.VMEM((nh, n_chunks, CHUNK, CHUNK), BF16),
        pltpu.VMEM((nh, n_chunks, 1, HEAD_DIM), F32),
    ]
    return pl.pallas_call(
        functools.partial(_gdn_kernel, reverse=reverse, n_chunks=n_chunks, d=d),
        grid=(batch, nt + 1),
        in_specs=[
            pl.BlockSpec((tt, width), lambda b, i: (tile_in(b, i), 0)),
            pl.BlockSpec((tt, width), lambda b, i: (tile_in(b, i), 1)),
            pl.BlockSpec((tt, width), lambda b, i: (tile_in(b, i), 2)),
            pl.BlockSpec((tt, LANES), lambda b, i: (tile_in(b, i), 0)),
            pl.BlockSpec((n_chunks, SUBLANES, nh * CHUNK), lambda b, i: (tile_in(b, i), 0, 0)),
        ],
        out_specs=pl.BlockSpec((tt, width), lambda b, i: (tile_out(b, i), 0)),
        out_shape=jax.ShapeDtypeStruct((batch * seq, width), F32),
        scratch_shapes=[pltpu.VMEM((nh, HEAD_DIM, HEAD_DIM), F32)] + scratch_set + scratch_set,
        compiler_params=_cparams(("parallel", "arbitrary")),
        name="gdn_bwd" if reverse else "gdn_fwd",
    )(qkv, qkv, qkv, col, row)


def _outproj_kernel(yrg_ref, of_ref, ob_ref, z_ref, gn_ref, w_ref, x_ref, o_ref):
    o = of_ref[...] + ob_ref[...]
    z = z_ref[...]
    gn = gn_ref[...]
    parts = [yrg_ref[...].astype(BF16)]
    for h in range(GDN_HEADS):
        hs = slice(h * HEAD_DIM, (h + 1) * HEAD_DIM)
        zh = z[:, hs]
        parts.append((_rms(o[:, hs]) * gn * (zh * _sigmoid(zh))).astype(BF16))
    y = jnp.concatenate(parts, axis=-1)
    o_ref[...] = x_ref[...] + jnp.dot(y, w_ref[...], preferred_element_type=F32)


def _out_proj(y_rg, o_f, o_b, p, gdn_norm, w_out, x, *, z_col, tm=512):
    m, d = x.shape
    wr = y_rg.shape[1]
    wg = o_f.shape[1]
    return pl.pallas_call(
        _outproj_kernel,
        grid=(m // tm,),
        in_specs=[
            pl.BlockSpec((tm, wr), lambda i: (i, 0)),
            pl.BlockSpec((tm, wg), lambda i: (i, 0)),
            pl.BlockSpec((tm, wg), lambda i: (i, 0)),
            pl.BlockSpec((tm, wg), lambda i: (i, z_col)),
            pl.BlockSpec((1, HEAD_DIM), lambda i: (0, 0)),
            pl.BlockSpec((wr + wg, d), lambda i: (0, 0)),
            pl.BlockSpec((tm, d), lambda i: (i, 0)),
        ],
        out_specs=pl.BlockSpec((tm, d), lambda i: (i, 0)),
        out_shape=jax.ShapeDtypeStruct((m, d), F32),
        compiler_params=_cparams(("parallel",)),
        name="out_proj",
    )(y_rg, o_f, o_b, p, gdn_norm.reshape(1, -1), w_out, x)


def _mixer(x1, batch, seq, mix_norm, w_in, w_out, rg_conv_w, rg_conv_b, rg_gate_a_w, rg_gate_a_b,
           rg_gate_x_w, rg_gate_x_b, rg_lambda, gdn_conv_w, gdn_a_log, gdn_dt_bias, gdn_norm):
    rg_w = rg_conv_w.shape[1]
    qkv_w = gdn_conv_w.shape[1]
    gdn_vw = GDN_HEADS * HEAD_DIM
    n_main = 2 * rg_w + qkv_w + gdn_vw
    n_gate = w_in.shape[1] - n_main
    w_main = w_in[:, :n_main].astype(BF16)
    w_ba = jnp.zeros((w_in.shape[0], LANES), F32).at[:, :n_gate].set(w_in[:, n_main:]).astype(BF16)
    p, ba = _in_proj(x1, mix_norm, w_main, w_ba)

    w_gates, b_gates = _rg_gate_weights(rg_gate_a_w, rg_gate_a_b, rg_gate_x_w, rg_gate_x_b)
    y_rg = _rglru(p, rg_conv_w, rg_conv_b, w_gates, b_gates, rg_lambda.astype(F32),
                  batch=batch, seq=seq, gate_col0=rg_w // LANES)

    qkv = _qkv_prep(p, gdn_conv_w, batch=batch, seq=seq, col0=2 * rg_w // LANES)
    col, row = _gate_prep(ba, gdn_a_log, gdn_dt_bias)
    o_f = _gdn(qkv, col, row, batch=batch, seq=seq, d=0)
    o_b = _gdn(qkv, col, row, batch=batch, seq=seq, d=1)
    return _out_proj(y_rg, o_f, o_b, p, gdn_norm, w_out.astype(BF16), x1,
                     z_col=(2 * rg_w + qkv_w) // gdn_vw)


def kernel(x, ffn1_norm, ffn1_w_gate, ffn1_w_up, ffn1_w_down, mix_norm, w_in, w_out, rg_conv_w, rg_conv_b, rg_gate_a_w, rg_gate_a_b, rg_gate_x_w, rg_gate_x_b, rg_lambda, gdn_conv_w, gdn_a_log, gdn_dt_bias, gdn_norm, ffn2_norm, ffn2_w_gate, ffn2_w_up, ffn2_w_down, final_norm):
    batch, seq, d_model = x.shape
    depth = ffn1_norm.shape[0]
    h = x.reshape(batch * seq, d_model)
    for l in range(depth):
        last = l == depth - 1
        h = _ffn(h, ffn1_norm[l], ffn1_w_gate[l].astype(BF16), ffn1_w_up[l].astype(BF16),
                 ffn1_w_down[l].astype(BF16), final_norm, final_norm=False)
        h = _mixer(h, batch, seq, mix_norm[l], w_in[l], w_out[l], rg_conv_w[l], rg_conv_b[l],
                   rg_gate_a_w[l], rg_gate_a_b[l], rg_gate_x_w[l], rg_gate_x_b[l], rg_lambda[l],
                   gdn_conv_w[l], gdn_a_log[l], gdn_dt_bias[l], gdn_norm[l])
        h = _ffn(h, ffn2_norm[l], ffn2_w_gate[l].astype(BF16), ffn2_w_up[l].astype(BF16),
                 ffn2_w_down[l].astype(BF16), final_norm, final_norm=last)
    return h.reshape(batch, seq, d_model)
```

```python
import functools
import itertools

import jax
import jax.numpy as jnp
from jax import lax
from jax.experimental import pallas as pl
from jax.experimental.pallas import tpu as pltpu

F32 = jnp.float32
BF16 = jnp.bfloat16

EPS = 1e-6
RG_C = 8.0
RG_BLOCKS = 8
GDN_HEADS = 4
HEAD_DIM = 128
CHUNK = 64
LANES = 128
SUBLANES = 8
VMEM_LIMIT = 56 * 1024 * 1024


def _cparams(sem):
    return pltpu.CompilerParams(dimension_semantics=sem, vmem_limit_bytes=VMEM_LIMIT)


def _rms(x):
    return x * lax.rsqrt(jnp.mean(x * x, axis=-1, keepdims=True) + EPS)


def _sigmoid(x):
    return 1.0 / (1.0 + jnp.exp(-x))


def _softplus(x):
    return jnp.maximum(x, 0.0) + jnp.log(1.0 + jnp.exp(-jnp.abs(x)))


def _gated_mixer_rows(yrg_ref, of_ref, ob_ref, z_ref, gn_ref, wo_ref, rows):
    o = of_ref[rows, :] + ob_ref[rows, :]
    z = z_ref[rows, :]
    gn = gn_ref[...]
    parts = [yrg_ref[rows, :].astype(BF16)]
    for h in range(GDN_HEADS):
        hs = slice(h * HEAD_DIM, (h + 1) * HEAD_DIM)
        zh = z[:, hs]
        parts.append((_rms(o[:, hs]) * gn * (zh * _sigmoid(zh))).astype(BF16))
    return jnp.dot(jnp.concatenate(parts, axis=-1), wo_ref[...], preferred_element_type=F32)


def _ffn_kernel(*refs, final_norm, n_sub, with_mixer):
    if with_mixer:
        x_ref, mixer_refs, (g_ref, wg_ref, wu_ref, wd_ref, fg_ref, o_ref) = refs[0], refs[1:7], refs[7:]
    else:
        x_ref, g_ref, wg_ref, wu_ref, wd_ref, fg_ref, o_ref = refs
    sub = x_ref.shape[0] // n_sub
    g = g_ref[...]
    for r in range(n_sub):
        rows = slice(r * sub, (r + 1) * sub)
        x = x_ref[rows, :]
        if with_mixer:
            x = x + _gated_mixer_rows(*mixer_refs, rows)
        h = (_rms(x) * g).astype(BF16)
        gate = jnp.dot(h, wg_ref[...], preferred_element_type=F32)
        up = jnp.dot(h, wu_ref[...], preferred_element_type=F32)
        act = (gate * _sigmoid(gate) * up).astype(BF16)
        y = x + 0.5 * jnp.dot(act, wd_ref[...], preferred_element_type=F32)
        if final_norm:
            y = _rms(y) * fg_ref[...]
        o_ref[rows, :] = y


def _ffn(x, g, wg, wu, wd, fg, *, final_norm, mixer=None, tm=512, n_sub=2):
    m, d = x.shape
    f = wg.shape[1]
    assert m % tm == 0 and tm % n_sub == 0
    resident = dict(pipeline_mode=pl.Buffered(1))
    row_spec = lambda w, col=0: pl.BlockSpec((tm, w), lambda i: (i, col))
    const_spec = lambda shape, **kw: pl.BlockSpec(shape, lambda i: (0, 0), **kw)
    operands, in_specs = [x], [row_spec(d)]
    if mixer is not None:
        y_rg, o_f, o_b, p, z_col, gdn_norm, w_out = mixer
        wr, wv = y_rg.shape[1], o_f.shape[1]
        operands += [y_rg, o_f, o_b, p, gdn_norm.reshape(1, -1), w_out]
        in_specs += [row_spec(wr), row_spec(wv), row_spec(wv), row_spec(wv, z_col),
                     const_spec((1, HEAD_DIM)), const_spec((wr + wv, d), **resident)]
    operands += [g.reshape(1, d), wg, wu, wd, fg.reshape(1, d)]
    in_specs += [const_spec((1, d)), const_spec((d, f), **resident), const_spec((d, f), **resident),
                 const_spec((f, d), **resident), const_spec((1, d))]
    return pl.pallas_call(
        functools.partial(_ffn_kernel, final_norm=final_norm, n_sub=n_sub, with_mixer=mixer is not None),
        grid=(m // tm,),
        in_specs=in_specs,
        out_specs=row_spec(d),
        out_shape=jax.ShapeDtypeStruct((m, d), F32),
        compiler_params=_cparams(("parallel",)),
        name="ffn_final" if final_norm else "ffn",
    )(*operands)


def _inproj_kernel(x_ref, g_ref, w_ref, p_ref, ba_ref, *, n_sub):
    tm, n = p_ref.shape
    sub = tm // n_sub
    g = g_ref[...]
    for r in range(n_sub):
        rows = slice(r * sub, (r + 1) * sub)
        h = (_rms(x_ref[rows, :]) * g).astype(BF16)
        p = jnp.dot(h, w_ref[...], preferred_element_type=F32)
        p_ref[rows, :] = p[:, :n]
        ba_ref[rows, :] = p[:, n:]


def _in_proj(x, g, w_main, w_ba, *, tm=512, n_sub=2):
    m, d = x.shape
    n = w_main.shape[1]
    w_all = jnp.concatenate([w_main, w_ba], axis=1)
    assert m % tm == 0 and tm % n_sub == 0
    return pl.pallas_call(
        functools.partial(_inproj_kernel, n_sub=n_sub),
        grid=(m // tm,),
        in_specs=[
            pl.BlockSpec((tm, d), lambda i: (i, 0)),
            pl.BlockSpec((1, d), lambda i: (0, 0)),
            pl.BlockSpec((d, n + LANES), lambda i: (0, 0)),
        ],
        out_specs=[
            pl.BlockSpec((tm, n), lambda i: (i, 0)),
            pl.BlockSpec((tm, LANES), lambda i: (i, 0)),
        ],
        out_shape=[jax.ShapeDtypeStruct((m, n), F32), jax.ShapeDtypeStruct((m, LANES), F32)],
        compiler_params=_cparams(("parallel",)),
        name="in_proj",
    )(x, g.reshape(1, d), w_all)


def _conv_tile(x_ref, cw, r0, rt, first, last, seq):
    x0 = x_ref[pl.ds(r0, rt), :]
    prev = x_ref[pl.ds(jnp.maximum(r0 - SUBLANES, 0), SUBLANES), :]
    prev = jnp.where(first, 0.0, prev)
    nxt = x_ref[pl.ds(jnp.minimum(r0 + rt, seq - SUBLANES), SUBLANES), :]
    nxt = jnp.where(last, 0.0, nxt)
    w = jnp.concatenate([prev, x0, nxt], axis=0)
    acc = cw[0:1, :] * w[SUBLANES - 2:SUBLANES - 2 + rt]
    acc = acc + cw[1:2, :] * w[SUBLANES - 1:SUBLANES - 1 + rt]
    acc = acc + cw[2:3, :] * x0
    acc = acc + cw[3:4, :] * w[SUBLANES + 1:SUBLANES + 1 + rt]
    return acc


RG_SEGS = 2 * SUBLANES
RG_PAD = 8


def _rglru_kernel(x_ref, gate_ref, cw_ref, cb_ref, wg_ref, bg_ref, lam_ref, o_ref,
                  af_scr, bf_scr, ab_scr, bb_scr, *, seq, rt):
    seg_len = seq // RG_SEGS
    pitch = seg_len + RG_PAD
    tiles_per_seg = seg_len // rt
    n_tiles = seq // rt
    a_scr = (af_scr, ab_scr)
    b_scr = (bf_scr, bb_scr)

    cw = cw_ref[...]
    cb = cb_ref[...]
    bg = bg_ref[...]
    neg_c_sp = -RG_C * _softplus(-lam_ref[...])

    def gates(i, _):
        r0 = pl.multiple_of(i * rt, rt)
        xc = _conv_tile(x_ref, cw, r0, rt, i == 0, i == n_tiles - 1, seq) + cb
        pre = jnp.dot(xc.astype(BF16), wg_ref[...], preferred_element_type=F32) + bg
        seg = i // tiles_per_seg
        off = pl.multiple_of(seg * pitch + (i - seg * tiles_per_seg) * rt, SUBLANES)
        for d in range(2):
            r = _sigmoid(pre[:, (2 * d) * LANES:(2 * d + 1) * LANES])
            ig = _sigmoid(pre[:, (2 * d + 1) * LANES:(2 * d + 2) * LANES])
            log_a = r * neg_c_sp[d:d + 1, :]
            a = jnp.exp(log_a)
            om = 1.0 - a * a
            b = jnp.where(om > 0.0, om * lax.rsqrt(om), 0.0) * (ig * xc)
            a_scr[d][pl.ds(off, rt), :] = a
            b_scr[d][pl.ds(off, rt), :] = b
        return 0

    lax.fori_loop(0, n_tiles, gates, 0)

    def scan(t, carry):
        hf, pf, hb, pb = carry
        idx = pl.ds(t, RG_SEGS, stride=pitch)
        a = af_scr[idx, :]
        hf = a * hf + bf_scr[idx, :]
        pf = a * pf
        bf_scr[idx, :] = hf
        af_scr[idx, :] = pf
        idx = pl.ds(seg_len - 1 - t, RG_SEGS, stride=pitch)
        a = ab_scr[idx, :]
        hb = a * hb + bb_scr[idx, :]
        pb = a * pb
        bb_scr[idx, :] = hb
        ab_scr[idx, :] = pb
        return hf, pf, hb, pb

    zeros = jnp.zeros((RG_SEGS, LANES), F32)
    ones = jnp.ones((RG_SEGS, LANES), F32)
    hf, pf, hb, pb = lax.fori_loop(0, seg_len, scan, (zeros, ones, zeros, ones), unroll=8)

    cf = [jnp.zeros((1, LANES), F32)]
    for s in range(1, RG_SEGS):
        cf.append(hf[s - 1:s, :] + pf[s - 1:s, :] * cf[s - 1])
    cbk = [None] * RG_SEGS
    cbk[RG_SEGS - 1] = jnp.zeros((1, LANES), F32)
    for s in range(RG_SEGS - 2, -1, -1):
        cbk[s] = hb[s + 1:s + 2, :] + pb[s + 1:s + 2, :] * cbk[s + 1]

    for s in range(RG_SEGS):
        def fix(k, _, s=s):
            off = pl.multiple_of(s * pitch + k * rt, SUBLANES)
            rows = pl.multiple_of(s * seg_len + k * rt, SUBLANES)
            h = bf_scr[pl.ds(off, rt), :] + af_scr[pl.ds(off, rt), :] * cf[s]
            h = h + (bb_scr[pl.ds(off, rt), :] + ab_scr[pl.ds(off, rt), :] * cbk[s])
            o_ref[pl.ds(rows, rt), :] = h * jax.nn.gelu(gate_ref[pl.ds(rows, rt), :], approximate=True)
            return 0

        lax.fori_loop(0, tiles_per_seg, fix, 0)


def _rglru(p, conv_w, conv_b, w_gates, b_gates, lam, *, batch, seq, gate_col0, rt=512):
    ngrp = conv_w.shape[1] // LANES
    seg_len = seq // RG_SEGS
    rt = min(rt, seg_len)
    assert seq % RG_SEGS == 0 and seg_len % rt == 0
    scr = pltpu.VMEM((RG_SEGS * (seg_len + RG_PAD), LANES), F32)
    return pl.pallas_call(
        functools.partial(_rglru_kernel, seq=seq, rt=rt),
        grid=(batch, ngrp),
        in_specs=[
            pl.BlockSpec((seq, LANES), lambda b, c: (b, c)),
            pl.BlockSpec((seq, LANES), lambda b, c: (b, gate_col0 + c)),
            pl.BlockSpec((4, LANES), lambda b, c: (0, c)),
            pl.BlockSpec((1, LANES), lambda b, c: (0, c)),
            pl.BlockSpec((None, LANES, 4 * LANES), lambda b, c: (c, 0, 0)),
            pl.BlockSpec((None, 1, 4 * LANES), lambda b, c: (c, 0, 0)),
            pl.BlockSpec((2, LANES), lambda b, c: (0, c)),
        ],
        out_specs=pl.BlockSpec((seq, LANES), lambda b, c: (b, c)),
        out_shape=jax.ShapeDtypeStruct((batch * seq, ngrp * LANES), F32),
        scratch_shapes=[scr, scr, scr, scr],
        compiler_params=_cparams(("parallel", "parallel")),
        name="rglru",
    )(p, p, conv_w, conv_b.reshape(1, -1), w_gates, b_gates, lam)


def _rg_gate_weights(wa, ba, wx, bx):
    ndir, nblk, blk, _ = wa.shape
    ngrp = nblk * blk // LANES
    per = LANES // blk

    def bd(w):
        w = w.reshape(ndir, ngrp, per, blk, blk)
        eye = jnp.eye(per, dtype=w.dtype)
        return jnp.einsum("dgpij,pq->dgpiqj", w, eye).reshape(ndir, ngrp, LANES, LANES)

    a, x = bd(wa), bd(wx)
    w = jnp.concatenate([a[0], x[0], a[1], x[1]], axis=-1)
    ba = ba.reshape(ndir, ngrp, 1, LANES)
    bx = bx.reshape(ndir, ngrp, 1, LANES)
    b = jnp.concatenate([ba[0], bx[0], ba[1], bx[1]], axis=-1)
    return w.astype(BF16), b.astype(F32)


def _qkv_kernel(x_ref, cw_ref, o_ref, *, seq, rt, n_qk, n_q, q_scale):
    c = pl.program_id(1)
    cw = cw_ref[...]
    n_tiles = seq // rt
    is_qk = c < n_qk
    scale = jnp.where(c < n_q, q_scale, 1.0).astype(F32)

    def body(i, _):
        r0 = pl.multiple_of(i * rt, rt)
        y = _conv_tile(x_ref, cw, r0, rt, i == 0, i == n_tiles - 1, seq)
        y = y * _sigmoid(y)
        nrm = y * (lax.rsqrt(jnp.sum(y * y, axis=-1, keepdims=True) + EPS) * scale)
        o_ref[pl.ds(r0, rt), :] = jnp.where(is_qk, nrm, y)
        return 0

    lax.fori_loop(0, n_tiles, body, 0)


def _qkv_prep(p, conv_w, *, batch, seq, col0, rt=512):
    ngrp = conv_w.shape[1] // LANES
    return pl.pallas_call(
        functools.partial(_qkv_kernel, seq=seq, rt=rt, n_qk=2 * GDN_HEADS, n_q=GDN_HEADS,
                          q_scale=HEAD_DIM ** -0.5),
        grid=(batch, ngrp),
        in_specs=[
            pl.BlockSpec((seq, LANES), lambda b, c: (b, col0 + c)),
            pl.BlockSpec((4, LANES), lambda b, c: (0, c)),
        ],
        out_specs=pl.BlockSpec((seq, LANES), lambda b, c: (b, c)),
        out_shape=jax.ShapeDtypeStruct((batch * seq, ngrp * LANES), F32),
        compiler_params=_cparams(("parallel", "parallel")),
        name="qkv_prep",
    )(p, conv_w)


def _gate_kernel(ba_ref, alog_ref, dtb_ref, col_ref, row_ref, *, tm):
    nh = GDN_HEADS
    ri = lax.broadcasted_iota(jnp.int32, (LANES, LANES), 0)
    ci = lax.broadcasted_iota(jnp.int32, (LANES, LANES), 1)
    same = (ri // CHUNK) == (ci // CHUNK)
    lower = jnp.where(same & (ri >= ci), 1.0, 0.0).astype(F32)
    upper = jnp.where(same & (ri <= ci), 1.0, 0.0).astype(F32)
    lane = ci
    neg_a = -jnp.exp(alog_ref[...])
    dtb = dtb_ref[...]
    for k in range(tm // LANES):
        sl = slice(k * LANES, (k + 1) * LANES)
        raw = ba_ref[sl, :]
        beta = _sigmoid(raw)
        gk = neg_a * _softplus(raw + dtb)
        gk = jnp.where((lane >= 2 * nh) & (lane < 4 * nh), gk, 0.0)
        cf = jnp.dot(lower, gk, preferred_element_type=F32, precision=lax.Precision.HIGHEST)
        cb = jnp.dot(upper, gk, preferred_element_type=F32, precision=lax.Precision.HIGHEST)
        gc = jnp.where(lane < 3 * nh, cf, cb)
        col = jnp.where(lane < 2 * nh, beta, gc)
        col_ref[sl, :] = col
        colt = col.T
        for q in range(LANES // CHUNK):
            rows = [jnp.concatenate([colt[(2 + dr) * nh + h:(2 + dr) * nh + h + 1, q * CHUNK:(q + 1) * CHUNK]
                                     for h in range(nh)], axis=1) for dr in range(2)]
            rows.append(jnp.zeros((SUBLANES - 2, nh * CHUNK), F32))
            row_ref[k * (LANES // CHUNK) + q] = jnp.concatenate(rows, axis=0)


def _gate_prep(ba, a_log, dt_bias, *, tm=512):
    m = ba.shape[0]
    nh = GDN_HEADS
    pad = lambda v: jnp.zeros((1, LANES), F32).at[0, 2 * nh:4 * nh].set(v.reshape(-1).astype(F32))
    return pl.pallas_call(
        functools.partial(_gate_kernel, tm=tm),
        grid=(m // tm,),
        in_specs=[
            pl.BlockSpec((tm, LANES), lambda i: (i, 0)),
            pl.BlockSpec((1, LANES), lambda i: (0, 0)),
            pl.BlockSpec((1, LANES), lambda i: (0, 0)),
        ],
        out_specs=[
            pl.BlockSpec((tm, LANES), lambda i: (i, 0)),
            pl.BlockSpec((tm // CHUNK, SUBLANES, nh * CHUNK), lambda i: (i, 0, 0)),
        ],
        out_shape=[jax.ShapeDtypeStruct((m, LANES), F32),
                   jax.ShapeDtypeStruct((m // CHUNK, SUBLANES, nh * CHUNK), F32)],
        compiler_params=_cparams(("parallel",)),
        name="gate_prep",
    )(ba, pad(a_log), pad(dt_bias))


def _bmm(a, b):
    return lax.dot_general(a.astype(BF16), b.astype(BF16), (((2,), (1,)), ((0,), (0,))),
                           preferred_element_type=F32)


def _bmm_nt(a, b):
    return lax.dot_general(a.astype(BF16), b.astype(BF16), (((2,), (2,)), ((0,), (0,))),
                           preferred_element_type=F32)


def _bmm_tn(a, b):
    return lax.dot_general(a.astype(BF16), b.astype(BF16), (((1,), (1,)), ((0,), (0,))),
                           preferred_element_type=F32)


def _unit_tri_inverse_levels(lm, ri, ci, bdmask, nb):
    def blockdiag(a):
        ab = a.astype(BF16)
        return jnp.where(bdmask, jnp.concatenate([ab] * nb, axis=1), jnp.zeros((), BF16))

    eye = jnp.where(ri == ci, 1.0, 0.0).astype(F32)
    x = eye - jnp.where((ri // 2) == (ci // 2), lm, 0.0)
    s = 2
    while s < CHUNK:
        cm = jnp.where(((ri // (2 * s)) == (ci // (2 * s))) & ((ri // s) != (ci // s)), lm, 0.0)
        x = x - _bmm(_bmm(x, blockdiag(cm)), blockdiag(x))
        yield x
        s *= 2


def _gdn_chunk_local(q_ref, k_ref, v_ref, col_ref, row_ref, dst, *, reverse, g, d):
    u_scr, wq_scr, kd_scr, at_scr, cd_scr = dst
    nh, c, hd = GDN_HEADS, CHUNK, HEAD_DIM
    pw = nh * c
    fw = nh * hd
    ri = lax.broadcasted_iota(jnp.int32, (1, c, pw), 1)
    ci = lax.broadcasted_iota(jnp.int32, (1, c, pw), 2) % c
    if reverse:
        incl, strict, last = ri <= ci, ri < ci, 0
    else:
        incl, strict, last = ri >= ci, ri > ci, c - 1
    bdmask = (lax.broadcasted_iota(jnp.int32, (1, pw, pw), 1) // c
              == lax.broadcasted_iota(jnp.int32, (1, pw, pw), 2) // c)
    kmask = (lax.broadcasted_iota(jnp.int32, (1, pw, fw), 1) // c
             == lax.broadcasted_iota(jnp.int32, (1, pw, fw), 2) // hd)
    low_half = lax.broadcasted_iota(jnp.int32, (g * c, hd), 1) < c

    col = col_ref[...]
    bcast = lambda j: jnp.broadcast_to(col[:, j:j + 1], (g * c, hd))
    beta = jnp.concatenate([bcast(d * nh + h) for h in range(nh)], axis=1).reshape(g, c, fw)
    gcs = [bcast((2 + d) * nh + h) for h in range(nh)]
    gc = jnp.concatenate(gcs, axis=1).reshape(g, c, fw)
    gc_col = jnp.concatenate([jnp.where(low_half, gcs[h], gcs[h + 1]) for h in range(0, nh, 2)],
                             axis=1).reshape(g, c, pw)
    gc_row = row_ref[:, d:d + 1, :]
    g_last = gc[:, last:last + 1, :]
    eg = jnp.exp(gc)
    decay = jnp.where(incl, jnp.exp(gc_col - gc_row), 0.0)
    q = q_ref[...].reshape(g, c, fw)
    k = k_ref[...].reshape(g, c, fw)
    v = v_ref[...].reshape(g, c, fw)
    k_beta = k * beta
    v_beta = v * beta
    kb16 = k.astype(BF16)
    k_bd = jnp.where(kmask, jnp.concatenate([kb16] * nh, axis=1), jnp.zeros((), BF16))
    kq = _bmm_nt(jnp.concatenate([k_beta, q], axis=1), k_bd)
    lm = jnp.where(strict, kq[:, :c] * decay, 0.0)
    attn = kq[:, c:] * decay
    yield
    for t in _unit_tri_inverse_levels(lm, ri, ci, bdmask, nh):
        yield
    kbe = k_beta * eg
    qd = q * eg
    kd = k * jnp.exp(g_last - gc)
    cd = jnp.exp(g_last)
    for h in range(nh):
        hs = slice(h * hd, (h + 1) * hd)
        ps = slice(h * c, (h + 1) * c)
        uw = _bmm(t[:, :, ps], jnp.concatenate([v_beta[:, :, hs], kbe[:, :, hs]], axis=-1))
        u_scr[h] = uw[:, :, :hd]
        wq_scr[h] = jnp.concatenate([uw[:, :, hd:], qd[:, :, hs]], axis=1).astype(BF16)
        at_scr[h] = attn[:, :, ps].astype(BF16)
        kd_scr[h] = kd[:, :, hs].astype(BF16)
        cd_scr[h] = cd[:, :, hs]
        if h % 2 == 1:
            yield


def _gdn_recurrence(src, s_scr, o_ref, *, reverse, g):
    u_scr, wq_scr, kd_scr, at_scr, cd_scr = src
    nh, c, hd = GDN_HEADS, CHUNK, HEAD_DIM
    for step in range(g):
        n = g - 1 - step if reverse else step
        state = s_scr[...]
        sb = state.astype(BF16)
        ws = _bmm(wq_scr[:, n], sb)
        v_new = u_scr[:, n] - ws[:, :c]
        vb = v_new.astype(BF16)
        o = ws[:, c:] + _bmm(at_scr[:, n], vb)
        s_scr[...] = state * cd_scr[:, n] + _bmm_tn(kd_scr[:, n], vb)
        for h in range(nh):
            o_ref[n * c:(n + 1) * c, h * hd:(h + 1) * hd] = o[h]
        yield


def _gdn_kernel(q_ref, k_ref, v_ref, col_ref, row_ref, o_ref, s_scr, *sets, reverse, n_chunks, d):
    set0, set1 = sets[:len(sets) // 2], sets[len(sets) // 2:]
    i = pl.program_id(1)

    @pl.when(i == 0)
    def _():
        s_scr[...] = jnp.zeros_like(s_scr)
        for r in set1:
            r[...] = jnp.zeros_like(r)

    def step(dst, src):
        local = _gdn_chunk_local(q_ref, k_ref, v_ref, col_ref, row_ref, dst, reverse=reverse, g=n_chunks, d=d)
        recur = _gdn_recurrence(src, s_scr, o_ref, reverse=reverse, g=n_chunks)
        for _ in itertools.zip_longest(local, recur):
            pass

    @pl.when(i % 2 == 0)
    def _():
        step(set0, set1)

    @pl.when(i % 2 == 1)
    def _():
        step(set1, set0)


def _gdn(qkv, col, row, *, batch, seq, d, tt=512):
    nh = GDN_HEADS
    width = nh * HEAD_DIM
    nt = seq // tt
    n_chunks = tt // CHUNK
    reverse = d == 1
    order = (lambda t: nt - 1 - t) if reverse else (lambda t: t)
    tile_in = lambda b, i: b * nt + order(jnp.minimum(i, nt - 1))
    tile_out = lambda b, i: b * nt + order(jnp.maximum(i - 1, 0))
    scratch_set = [
        pltpu.VMEM((nh, n_chunks, CHUNK, HEAD_DIM), F32),
        pltpu.VMEM((nh, n_chunks, 2 * CHUNK, HEAD_DIM), BF16),
        pltpu.VMEM((nh, n_chunks, CHUNK, HEAD_DIM), BF16),
        pltpu.VMEM((nh, n_chunks, CHUNK, CHUNK), BF16),
        pltpu.VMEM((nh, n_chunks, 1, HEAD_DIM), F32),
    ]
    return pl.pallas_call(
        functools.partial(_gdn_kernel, reverse=reverse, n_chunks=n_chunks, d=d),
        grid=(batch, nt + 1),
        in_specs=[
            pl.BlockSpec((tt, width), lambda b, i: (tile_in(b, i), 0)),
            pl.BlockSpec((tt, width), lambda b, i: (tile_in(b, i), 1)),
            pl.BlockSpec((tt, width), lambda b, i: (tile_in(b, i), 2)),
            pl.BlockSpec((tt, LANES), lambda b, i: (tile_in(b, i), 0)),
            pl.BlockSpec((n_chunks, SUBLANES, nh * CHUNK), lambda b, i: (tile_in(b, i), 0, 0)),
        ],
        out_specs=pl.BlockSpec((tt, width), lambda b, i: (tile_out(b, i), 0)),
        out_shape=jax.ShapeDtypeStruct((batch * seq, width), F32),
        scratch_shapes=[pltpu.VMEM((nh, HEAD_DIM, HEAD_DIM), F32)] + scratch_set + scratch_set,
        compiler_params=_cparams(("parallel", "arbitrary")),
        name="gdn_bwd" if reverse else "gdn_fwd",
    )(qkv, qkv, qkv, col, row)


def _mixer(x1, batch, seq, mix_norm, w_in, w_out, rg_conv_w, rg_conv_b, rg_gate_a_w, rg_gate_a_b,
           rg_gate_x_w, rg_gate_x_b, rg_lambda, gdn_conv_w, gdn_a_log, gdn_dt_bias, gdn_norm):
    rg_w = rg_conv_w.shape[1]
    qkv_w = gdn_conv_w.shape[1]
    gdn_vw = GDN_HEADS * HEAD_DIM
    n_main = 2 * rg_w + qkv_w + gdn_vw
    n_gate = w_in.shape[1] - n_main
    w_main = w_in[:, :n_main].astype(BF16)
    w_ba = jnp.zeros((w_in.shape[0], LANES), F32).at[:, :n_gate].set(w_in[:, n_main:]).astype(BF16)
    p, ba = _in_proj(x1, mix_norm, w_main, w_ba)

    w_gates, b_gates = _rg_gate_weights(rg_gate_a_w, rg_gate_a_b, rg_gate_x_w, rg_gate_x_b)
    y_rg = _rglru(p, rg_conv_w, rg_conv_b, w_gates, b_gates, rg_lambda.astype(F32),
                  batch=batch, seq=seq, gate_col0=rg_w // LANES)

    qkv = _qkv_prep(p, gdn_conv_w, batch=batch, seq=seq, col0=2 * rg_w // LANES)
    col, row = _gate_prep(ba, gdn_a_log, gdn_dt_bias)
    o_f = _gdn(qkv, col, row, batch=batch, seq=seq, d=0)
    o_b = _gdn(qkv, col, row, batch=batch, seq=seq, d=1)
    return y_rg, o_f, o_b, p, (2 * rg_w + qkv_w) // gdn_vw, gdn_norm, w_out.astype(BF16)


def kernel(x, ffn1_norm, ffn1_w_gate, ffn1_w_up, ffn1_w_down, mix_norm, w_in, w_out, rg_conv_w, rg_conv_b, rg_gate_a_w, rg_gate_a_b, rg_gate_x_w, rg_gate_x_b, rg_lambda, gdn_conv_w, gdn_a_log, gdn_dt_bias, gdn_norm, ffn2_norm, ffn2_w_gate, ffn2_w_up, ffn2_w_down, final_norm):
    batch, seq, d_model = x.shape
    depth = ffn1_norm.shape[0]
    h = x.reshape(batch * seq, d_model)
    for l in range(depth):
        last = l == depth - 1
        h = _ffn(h, ffn1_norm[l], ffn1_w_gate[l].astype(BF16), ffn1_w_up[l].astype(BF16),
                 ffn1_w_down[l].astype(BF16), final_norm, final_norm=False)
        mixer = _mixer(h, batch, seq, mix_norm[l], w_in[l], w_out[l], rg_conv_w[l], rg_conv_b[l],
                   rg_gate_a_w[l], rg_gate_a_b[l], rg_gate_x_w[l], rg_gate_x_b[l], rg_lambda[l],
                   gdn_conv_w[l], gdn_a_log[l], gdn_dt_bias[l], gdn_norm[l])
        h = _ffn(h, ffn2_norm[l], ffn2_w_gate[l].astype(BF16), ffn2_w_up[l].astype(BF16),
                 ffn2_w_down[l].astype(BF16), final_norm, final_norm=last, mixer=mixer)
    return h.reshape(batch, seq, d_model)
```

```python
import functools
import itertools

import jax
import jax.numpy as jnp
from jax import lax
from jax.experimental import pallas as pl
from jax.experimental.pallas import tpu as pltpu

F32 = jnp.float32
BF16 = jnp.bfloat16

EPS = 1e-6
RG_C = 8.0
RG_BLOCKS = 8
GDN_HEADS = 4
HEAD_DIM = 128
CHUNK = 64
LANES = 128
SUBLANES = 8
VMEM_LIMIT = 56 * 1024 * 1024


def _cparams(sem):
    return pltpu.CompilerParams(dimension_semantics=sem, vmem_limit_bytes=VMEM_LIMIT)


def _rms(x):
    return x * lax.rsqrt(jnp.mean(x * x, axis=-1, keepdims=True) + EPS)


def _sigmoid(x):
    return 1.0 / (1.0 + jnp.exp(-x))


def _softplus(x):
    return jnp.maximum(x, 0.0) + jnp.log(1.0 + jnp.exp(-jnp.abs(x)))


def _gated_mixer_rows(yrg_ref, of_ref, ob_ref, z_ref, gn_ref, wo_ref, rows):
    o = of_ref[rows, :] + ob_ref[rows, :]
    z = z_ref[rows, :]
    gn = gn_ref[...]
    parts = [yrg_ref[rows, :].astype(BF16)]
    for h in range(GDN_HEADS):
        hs = slice(h * HEAD_DIM, (h + 1) * HEAD_DIM)
        zh = z[:, hs]
        parts.append((_rms(o[:, hs]) * gn * (zh * _sigmoid(zh))).astype(BF16))
    return jnp.dot(jnp.concatenate(parts, axis=-1), wo_ref[...], preferred_element_type=F32)


def _ffn_kernel(*refs, final_norm, n_sub, with_mixer):
    if with_mixer:
        x_ref, mixer_refs, (g_ref, wg_ref, wu_ref, wd_ref, fg_ref, o_ref) = refs[0], refs[1:7], refs[7:]
    else:
        x_ref, g_ref, wg_ref, wu_ref, wd_ref, fg_ref, o_ref = refs
    sub = x_ref.shape[0] // n_sub
    g = g_ref[...]
    for r in range(n_sub):
        rows = slice(r * sub, (r + 1) * sub)
        x = x_ref[rows, :]
        if with_mixer:
            x = x + _gated_mixer_rows(*mixer_refs, rows)
        h = (_rms(x) * g).astype(BF16)
        gate = jnp.dot(h, wg_ref[...], preferred_element_type=F32)
        up = jnp.dot(h, wu_ref[...], preferred_element_type=F32)
        act = (gate * _sigmoid(gate) * up).astype(BF16)
        y = x + 0.5 * jnp.dot(act, wd_ref[...], preferred_element_type=F32)
        if final_norm:
            y = _rms(y) * fg_ref[...]
        o_ref[rows, :] = y


def _ffn(x, g, wg, wu, wd, fg, *, final_norm, mixer=None, tm=512, n_sub=2):
    m, d = x.shape
    f = wg.shape[1]
    assert m % tm == 0 and tm % n_sub == 0
    resident = dict(pipeline_mode=pl.Buffered(1))
    row_spec = lambda w, col=0: pl.BlockSpec((tm, w), lambda i: (i, col))
    const_spec = lambda shape, **kw: pl.BlockSpec(shape, lambda i: (0, 0), **kw)
    operands, in_specs = [x], [row_spec(d)]
    if mixer is not None:
        y_rg, o_f, o_b, p, z_col, gdn_norm, w_out = mixer
        wr, wv = y_rg.shape[1], o_f.shape[1]
        operands += [y_rg, o_f, o_b, p, gdn_norm.reshape(1, -1), w_out]
        in_specs += [row_spec(wr), row_spec(wv), row_spec(wv), row_spec(wv, z_col),
                     const_spec((1, HEAD_DIM)), const_spec((wr + wv, d), **resident)]
    operands += [g.reshape(1, d), wg, wu, wd, fg.reshape(1, d)]
    in_specs += [const_spec((1, d)), const_spec((d, f), **resident), const_spec((d, f), **resident),
                 const_spec((f, d), **resident), const_spec((1, d))]
    return pl.pallas_call(
        functools.partial(_ffn_kernel, final_norm=final_norm, n_sub=n_sub, with_mixer=mixer is not None),
        grid=(m // tm,),
        in_specs=in_specs,
        out_specs=row_spec(d),
        out_shape=jax.ShapeDtypeStruct((m, d), F32),
        compiler_params=_cparams(("parallel",)),
        name="ffn_final" if final_norm else "ffn",
    )(*operands)


def _inproj_kernel(x_ref, xp_ref, xn_ref, g_ref, w_ref, cwr_ref, cbr_ref, cwq_ref, p_ref, ba_ref, *,
                   n_sub, tiles_per_seq, rg_w, qkv_w, n_qk, q_scale):
    tm, n = p_ref.shape
    sub = tm // n_sub
    hal = SUBLANES
    i = pl.program_id(0)
    pos = i % tiles_per_seq
    prev = jnp.where(pos == 0, 0.0, xp_ref[...])
    nxt = jnp.where(pos == tiles_per_seq - 1, 0.0, xn_ref[...])
    g = g_ref[...]
    cwr = cwr_ref[...]
    mid = slice(hal, hal + sub)

    def conv(pw, cw):
        acc = cw[0:1, :] * pw[hal - 2:hal - 2 + sub]
        acc = acc + cw[1:2, :] * pw[hal - 1:hal - 1 + sub]
        acc = acc + cw[2:3, :] * pw[mid]
        return acc + cw[3:4, :] * pw[hal + 1:hal + 1 + sub]

    for r in range(n_sub):
        lo, hi = r * sub - hal, (r + 1) * sub + hal
        parts = ([prev] if lo < 0 else []) + [x_ref[max(lo, 0):min(hi, tm), :]] + ([nxt] if hi > tm else [])
        xw = jnp.concatenate(parts, axis=0)
        h = (_rms(xw) * g).astype(BF16)
        p = jnp.dot(h, w_ref[...], preferred_element_type=F32)
        rows = slice(r * sub, (r + 1) * sub)
        p_ref[rows, 0:rg_w] = conv(p[:, 0:rg_w], cwr) + cbr_ref[...]
        p_ref[rows, rg_w:2 * rg_w] = p[mid, rg_w:2 * rg_w]
        for c in range(qkv_w // LANES):
            cols = slice(2 * rg_w + c * LANES, 2 * rg_w + (c + 1) * LANES)
            y = conv(p[:, cols], cwq_ref[:, c * LANES:(c + 1) * LANES])
            y = y * _sigmoid(y)
            if c < n_qk:
                scale = q_scale if c < n_qk // 2 else 1.0
                y = y * (lax.rsqrt(jnp.sum(y * y, axis=-1, keepdims=True) + EPS) * scale)
            p_ref[rows, cols] = y
        p_ref[rows, 2 * rg_w + qkv_w:n] = p[mid, 2 * rg_w + qkv_w:n]
        ba_ref[rows, :] = p[mid, n:]


def _in_proj(x, g, w_main, w_ba, rg_conv_w, rg_conv_b, qkv_conv_w, *, seq, tm=512, n_sub=2):
    m, d = x.shape
    n = w_main.shape[1]
    rg_w, qkv_w = rg_conv_w.shape[1], qkv_conv_w.shape[1]
    w_all = jnp.concatenate([w_main, w_ba], axis=1)
    assert m % tm == 0 and tm % n_sub == 0 and seq % tm == 0
    hb = tm // SUBLANES
    const_spec = lambda shape, **kw: pl.BlockSpec(shape, lambda i: (0, 0), **kw)
    return pl.pallas_call(
        functools.partial(_inproj_kernel, n_sub=n_sub, tiles_per_seq=seq // tm, rg_w=rg_w, qkv_w=qkv_w,
                          n_qk=2 * GDN_HEADS, q_scale=HEAD_DIM ** -0.5),
        grid=(m // tm,),
        in_specs=[
            pl.BlockSpec((tm, d), lambda i: (i, 0)),
            pl.BlockSpec((SUBLANES, d), lambda i: (jnp.maximum(i * hb - 1, 0), 0)),
            pl.BlockSpec((SUBLANES, d), lambda i: (jnp.minimum((i + 1) * hb, m // SUBLANES - 1), 0)),
            const_spec((1, d)),
            const_spec((d, n + LANES), pipeline_mode=pl.Buffered(1)),
            const_spec((4, rg_w)),
            const_spec((1, rg_w)),
            const_spec((4, qkv_w)),
        ],
        out_specs=[
            pl.BlockSpec((tm, n), lambda i: (i, 0)),
            pl.BlockSpec((tm, LANES), lambda i: (i, 0)),
        ],
        out_shape=[jax.ShapeDtypeStruct((m, n), F32), jax.ShapeDtypeStruct((m, LANES), F32)],
        compiler_params=_cparams(("parallel",)),
        name="in_proj",
    )(x, x, x, g.reshape(1, d), w_all, rg_conv_w, rg_conv_b.reshape(1, -1), qkv_conv_w)


RG_SEGS = 2 * SUBLANES
RG_PAD = 8


def _rglru_kernel(x_ref, gate_ref, wg_ref, bg_ref, lam_ref, o_ref,
                  af_scr, bf_scr, ab_scr, bb_scr, *, seq, rt):
    seg_len = seq // RG_SEGS
    pitch = seg_len + RG_PAD
    tiles_per_seg = seg_len // rt
    n_tiles = seq // rt
    a_scr = (af_scr, ab_scr)
    b_scr = (bf_scr, bb_scr)

    bg = bg_ref[...]
    neg_c_sp = -RG_C * _softplus(-lam_ref[...])

    def gates(i, _):
        r0 = pl.multiple_of(i * rt, rt)
        xc = x_ref[pl.ds(r0, rt), :]
        pre = jnp.dot(xc.astype(BF16), wg_ref[...], preferred_element_type=F32) + bg
        seg = i // tiles_per_seg
        off = pl.multiple_of(seg * pitch + (i - seg * tiles_per_seg) * rt, SUBLANES)
        for d in range(2):
            r = _sigmoid(pre[:, (2 * d) * LANES:(2 * d + 1) * LANES])
            ig = _sigmoid(pre[:, (2 * d + 1) * LANES:(2 * d + 2) * LANES])
            log_a = r * neg_c_sp[d:d + 1, :]
            a = jnp.exp(log_a)
            om = 1.0 - a * a
            b = jnp.where(om > 0.0, om * lax.rsqrt(om), 0.0) * (ig * xc)
            a_scr[d][pl.ds(off, rt), :] = a
            b_scr[d][pl.ds(off, rt), :] = b
        return 0

    lax.fori_loop(0, n_tiles, gates, 0)

    def scan(t, carry):
        hf, pf, hb, pb = carry
        idx = pl.ds(t, RG_SEGS, stride=pitch)
        a = af_scr[idx, :]
        hf = a * hf + bf_scr[idx, :]
        pf = a * pf
        bf_scr[idx, :] = hf
        af_scr[idx, :] = pf
        idx = pl.ds(seg_len - 1 - t, RG_SEGS, stride=pitch)
        a = ab_scr[idx, :]
        hb = a * hb + bb_scr[idx, :]
        pb = a * pb
        bb_scr[idx, :] = hb
        ab_scr[idx, :] = pb
        return hf, pf, hb, pb

    zeros = jnp.zeros((RG_SEGS, LANES), F32)
    ones = jnp.ones((RG_SEGS, LANES), F32)
    hf, pf, hb, pb = lax.fori_loop(0, seg_len, scan, (zeros, ones, zeros, ones), unroll=8)

    cf = [jnp.zeros((1, LANES), F32)]
    for s in range(1, RG_SEGS):
        cf.append(hf[s - 1:s, :] + pf[s - 1:s, :] * cf[s - 1])
    cbk = [None] * RG_SEGS
    cbk[RG_SEGS - 1] = jnp.zeros((1, LANES), F32)
    for s in range(RG_SEGS - 2, -1, -1):
        cbk[s] = hb[s + 1:s + 2, :] + pb[s + 1:s + 2, :] * cbk[s + 1]

    for s in range(RG_SEGS):
        def fix(k, _, s=s):
            off = pl.multiple_of(s * pitch + k * rt, SUBLANES)
            rows = pl.multiple_of(s * seg_len + k * rt, SUBLANES)
            h = bf_scr[pl.ds(off, rt), :] + af_scr[pl.ds(off, rt), :] * cf[s]
            h = h + (bb_scr[pl.ds(off, rt), :] + ab_scr[pl.ds(off, rt), :] * cbk[s])
            o_ref[pl.ds(rows, rt), :] = h * jax.nn.gelu(gate_ref[pl.ds(rows, rt), :], approximate=True)
            return 0

        lax.fori_loop(0, tiles_per_seg, fix, 0)


def _rglru(p, w_gates, b_gates, lam, *, batch, seq, gate_col0, rt=512):
    ngrp = w_gates.shape[0]
    seg_len = seq // RG_SEGS
    rt = min(rt, seg_len)
    assert seq % RG_SEGS == 0 and seg_len % rt == 0
    scr = pltpu.VMEM((RG_SEGS * (seg_len + RG_PAD), LANES), F32)
    return pl.pallas_call(
        functools.partial(_rglru_kernel, seq=seq, rt=rt),
        grid=(batch, ngrp),
        in_specs=[
            pl.BlockSpec((seq, LANES), lambda b, c: (b, c)),
            pl.BlockSpec((seq, LANES), lambda b, c: (b, gate_col0 + c)),
            pl.BlockSpec((None, LANES, 4 * LANES), lambda b, c: (c, 0, 0)),
            pl.BlockSpec((None, 1, 4 * LANES), lambda b, c: (c, 0, 0)),
            pl.BlockSpec((2, LANES), lambda b, c: (0, c)),
        ],
        out_specs=pl.BlockSpec((seq, LANES), lambda b, c: (b, c)),
        out_shape=jax.ShapeDtypeStruct((batch * seq, ngrp * LANES), F32),
        scratch_shapes=[scr, scr, scr, scr],
        compiler_params=_cparams(("parallel", "parallel")),
        name="rglru",
    )(p, p, w_gates, b_gates, lam)


def _rg_gate_weights(wa, ba, wx, bx):
    ndir, nblk, blk, _ = wa.shape
    ngrp = nblk * blk // LANES
    per = LANES // blk

    def bd(w):
        w = w.reshape(ndir, ngrp, per, blk, blk)
        eye = jnp.eye(per, dtype=w.dtype)
        return jnp.einsum("dgpij,pq->dgpiqj", w, eye).reshape(ndir, ngrp, LANES, LANES)

    a, x = bd(wa), bd(wx)
    w = jnp.concatenate([a[0], x[0], a[1], x[1]], axis=-1)
    ba = ba.reshape(ndir, ngrp, 1, LANES)
    bx = bx.reshape(ndir, ngrp, 1, LANES)
    b = jnp.concatenate([ba[0], bx[0], ba[1], bx[1]], axis=-1)
    return w.astype(BF16), b.astype(F32)


def _gate_kernel(ba_ref, alog_ref, dtb_ref, col_ref, row_ref, *, tm):
    nh = GDN_HEADS
    ri = lax.broadcasted_iota(jnp.int32, (LANES, LANES), 0)
    ci = lax.broadcasted_iota(jnp.int32, (LANES, LANES), 1)
    same = (ri // CHUNK) == (ci // CHUNK)
    lower = jnp.where(same & (ri >= ci), 1.0, 0.0).astype(F32)
    upper = jnp.where(same & (ri <= ci), 1.0, 0.0).astype(F32)
    lane = ci
    neg_a = -jnp.exp(alog_ref[...])
    dtb = dtb_ref[...]
    for k in range(tm // LANES):
        sl = slice(k * LANES, (k + 1) * LANES)
        raw = ba_ref[sl, :]
        beta = _sigmoid(raw)
        gk = neg_a * _softplus(raw + dtb)
        gk = jnp.where((lane >= 2 * nh) & (lane < 4 * nh), gk, 0.0)
        cf = jnp.dot(lower, gk, preferred_element_type=F32, precision=lax.Precision.HIGHEST)
        cb = jnp.dot(upper, gk, preferred_element_type=F32, precision=lax.Precision.HIGHEST)
        gc = jnp.where(lane < 3 * nh, cf, cb)
        col = jnp.where(lane < 2 * nh, beta, gc)
        col_ref[sl, :] = col
        colt = col.T
        for q in range(LANES // CHUNK):
            rows = [jnp.concatenate([colt[(2 + dr) * nh + h:(2 + dr) * nh + h + 1, q * CHUNK:(q + 1) * CHUNK]
                                     for h in range(nh)], axis=1) for dr in range(2)]
            rows.append(jnp.zeros((SUBLANES - 2, nh * CHUNK), F32))
            row_ref[k * (LANES // CHUNK) + q] = jnp.concatenate(rows, axis=0)


def _gate_prep(ba, a_log, dt_bias, *, tm=512):
    m = ba.shape[0]
    nh = GDN_HEADS
    pad = lambda v: jnp.zeros((1, LANES), F32).at[0, 2 * nh:4 * nh].set(v.reshape(-1).astype(F32))
    return pl.pallas_call(
        functools.partial(_gate_kernel, tm=tm),
        grid=(m // tm,),
        in_specs=[
            pl.BlockSpec((tm, LANES), lambda i: (i, 0)),
            pl.BlockSpec((1, LANES), lambda i: (0, 0)),
            pl.BlockSpec((1, LANES), lambda i: (0, 0)),
        ],
        out_specs=[
            pl.BlockSpec((tm, LANES), lambda i: (i, 0)),
            pl.BlockSpec((tm // CHUNK, SUBLANES, nh * CHUNK), lambda i: (i, 0, 0)),
        ],
        out_shape=[jax.ShapeDtypeStruct((m, LANES), F32),
                   jax.ShapeDtypeStruct((m // CHUNK, SUBLANES, nh * CHUNK), F32)],
        compiler_params=_cparams(("parallel",)),
        name="gate_prep",
    )(ba, pad(a_log), pad(dt_bias))


def _bmm(a, b):
    return lax.dot_general(a.astype(BF16), b.astype(BF16), (((2,), (1,)), ((0,), (0,))),
                           preferred_element_type=F32)


def _bmm_nt(a, b):
    return lax.dot_general(a.astype(BF16), b.astype(BF16), (((2,), (2,)), ((0,), (0,))),
                           preferred_element_type=F32)


def _bmm_tn(a, b):
    return lax.dot_general(a.astype(BF16), b.astype(BF16), (((1,), (1,)), ((0,), (0,))),
                           preferred_element_type=F32)


def _unit_tri_inverse_levels(lm, ri, ci, bdmask, nb):
    def blockdiag(a):
        ab = a.astype(BF16)
        return jnp.where(bdmask, jnp.concatenate([ab] * nb, axis=1), jnp.zeros((), BF16))

    eye = jnp.where(ri == ci, 1.0, 0.0).astype(F32)
    x = eye - jnp.where((ri // 2) == (ci // 2), lm, 0.0)
    s = 2
    while s < CHUNK:
        cm = jnp.where(((ri // (2 * s)) == (ci // (2 * s))) & ((ri // s) != (ci // s)), lm, 0.0)
        x = x - _bmm(_bmm(x, blockdiag(cm)), blockdiag(x))
        yield x
        s *= 2


def _gdn_chunk_local(q_ref, k_ref, v_ref, col_ref, row_ref, dst, *, reverse, g, d):
    u_scr, wq_scr, kd_scr, at_scr, cd_scr = dst
    nh, c, hd = GDN_HEADS, CHUNK, HEAD_DIM
    pw = nh * c
    fw = nh * hd
    ri = lax.broadcasted_iota(jnp.int32, (1, c, pw), 1)
    ci = lax.broadcasted_iota(jnp.int32, (1, c, pw), 2) % c
    if reverse:
        incl, strict, last = ri <= ci, ri < ci, 0
    else:
        incl, strict, last = ri >= ci, ri > ci, c - 1
    bdmask = (lax.broadcasted_iota(jnp.int32, (1, pw, pw), 1) // c
              == lax.broadcasted_iota(jnp.int32, (1, pw, pw), 2) // c)
    kmask = (lax.broadcasted_iota(jnp.int32, (1, pw, fw), 1) // c
             == lax.broadcasted_iota(jnp.int32, (1, pw, fw), 2) // hd)
    low_half = lax.broadcasted_iota(jnp.int32, (g * c, hd), 1) < c

    col = col_ref[...]
    bcast = lambda j: jnp.broadcast_to(col[:, j:j + 1], (g * c, hd))
    beta = jnp.concatenate([bcast(d * nh + h) for h in range(nh)], axis=1).reshape(g, c, fw)
    gcs = [bcast((2 + d) * nh + h) for h in range(nh)]
    gc = jnp.concatenate(gcs, axis=1).reshape(g, c, fw)
    gc_col = jnp.concatenate([jnp.where(low_half, gcs[h], gcs[h + 1]) for h in range(0, nh, 2)],
                             axis=1).reshape(g, c, pw)
    gc_row = row_ref[:, d:d + 1, :]
    g_last = gc[:, last:last + 1, :]
    eg = jnp.exp(gc)
    decay = jnp.where(incl, jnp.exp(gc_col - gc_row), 0.0)
    q = q_ref[...].reshape(g, c, fw)
    k = k_ref[...].reshape(g, c, fw)
    v = v_ref[...].reshape(g, c, fw)
    k_beta = k * beta
    v_beta = v * beta
    kb16 = k.astype(BF16)
    k_bd = jnp.where(kmask, jnp.concatenate([kb16] * nh, axis=1), jnp.zeros((), BF16))
    kq = _bmm_nt(jnp.concatenate([k_beta, q], axis=1), k_bd)
    lm = jnp.where(strict, kq[:, :c] * decay, 0.0)
    attn = kq[:, c:] * decay
    yield
    for t in _unit_tri_inverse_levels(lm, ri, ci, bdmask, nh):
        yield
    kbe = k_beta * eg
    qd = q * eg
    kd = k * jnp.exp(g_last - gc)
    cd = jnp.exp(g_last)
    for h in range(nh):
        hs = slice(h * hd, (h + 1) * hd)
        ps = slice(h * c, (h + 1) * c)
        uw = _bmm(t[:, :, ps], jnp.concatenate([v_beta[:, :, hs], kbe[:, :, hs]], axis=-1))
        u_scr[h] = uw[:, :, :hd]
        wq_scr[h] = jnp.concatenate([uw[:, :, hd:], qd[:, :, hs]], axis=1).astype(BF16)
        at_scr[h] = attn[:, :, ps].astype(BF16)
        kd_scr[h] = kd[:, :, hs].astype(BF16)
        cd_scr[h] = cd[:, :, hs]
        if h % 2 == 1:
            yield


def _gdn_recurrence(src, s_scr, o_ref, *, reverse, g):
    u_scr, wq_scr, kd_scr, at_scr, cd_scr = src
    nh, c, hd = GDN_HEADS, CHUNK, HEAD_DIM
    for step in range(g):
        n = g - 1 - step if reverse else step
        state = s_scr[...]
        sb = state.astype(BF16)
        ws = _bmm(wq_scr[:, n], sb)
        v_new = u_scr[:, n] - ws[:, :c]
        vb = v_new.astype(BF16)
        o = ws[:, c:] + _bmm(at_scr[:, n], vb)
        s_scr[...] = state * cd_scr[:, n] + _bmm_tn(kd_scr[:, n], vb)
        for h in range(nh):
            o_ref[n * c:(n + 1) * c, h * hd:(h + 1) * hd] = o[h]
        yield


def _gdn_kernel(q_ref, k_ref, v_ref, col_ref, row_ref, o_ref, s_scr, *sets, reverse, n_chunks, d):
    set0, set1 = sets[:len(sets) // 2], sets[len(sets) // 2:]
    i = pl.program_id(1)

    @pl.when(i == 0)
    def _():
        s_scr[...] = jnp.zeros_like(s_scr)
        for r in set1:
            r[...] = jnp.zeros_like(r)

    def step(dst, src):
        local = _gdn_chunk_local(q_ref, k_ref, v_ref, col_ref, row_ref, dst, reverse=reverse, g=n_chunks, d=d)
        recur = _gdn_recurrence(src, s_scr, o_ref, reverse=reverse, g=n_chunks)
        for _ in itertools.zip_longest(local, recur):
            pass

    @pl.when(i % 2 == 0)
    def _():
        step(set0, set1)

    @pl.when(i % 2 == 1)
    def _():
        step(set1, set0)


def _gdn(p, col, row, *, q_col, batch, seq, d, tt=512):
    nh = GDN_HEADS
    width = nh * HEAD_DIM
    nt = seq // tt
    n_chunks = tt // CHUNK
    reverse = d == 1
    order = (lambda t: nt - 1 - t) if reverse else (lambda t: t)
    tile_in = lambda b, i: b * nt + order(jnp.minimum(i, nt - 1))
    tile_out = lambda b, i: b * nt + order(jnp.maximum(i - 1, 0))
    scratch_set = [
        pltpu.VMEM((nh, n_chunks, CHUNK, HEAD_DIM), F32),
        pltpu.VMEM((nh, n_chunks, 2 * CHUNK, HEAD_DIM), BF16),
        pltpu.VMEM((nh, n_chunks, CHUNK, HEAD_DIM), BF16),
        pltpu.VMEM((nh, n_chunks, CHUNK, CHUNK), BF16),
        pltpu.VMEM((nh, n_chunks, 1, HEAD_DIM), F32),
    ]
    return pl.pallas_call(
        functools.partial(_gdn_kernel, reverse=reverse, n_chunks=n_chunks, d=d),
        grid=(batch, nt + 1),
        in_specs=[
            pl.BlockSpec((tt, width), lambda b, i: (tile_in(b, i), q_col)),
            pl.BlockSpec((tt, width), lambda b, i: (tile_in(b, i), q_col + 1)),
            pl.BlockSpec((tt, width), lambda b, i: (tile_in(b, i), q_col + 2)),
            pl.BlockSpec((tt, LANES), lambda b, i: (tile_in(b, i), 0)),
            pl.BlockSpec((n_chunks, SUBLANES, nh * CHUNK), lambda b, i: (tile_in(b, i), 0, 0)),
        ],
        out_specs=pl.BlockSpec((tt, width), lambda b, i: (tile_out(b, i), 0)),
        out_shape=jax.ShapeDtypeStruct((batch * seq, width), F32),
        scratch_shapes=[pltpu.VMEM((nh, HEAD_DIM, HEAD_DIM), F32)] + scratch_set + scratch_set,
        compiler_params=_cparams(("parallel", "arbitrary")),
        name="gdn_bwd" if reverse else "gdn_fwd",
    )(p, p, p, col, row)


def _mixer(x1, batch, seq, mix_norm, w_in, w_out, rg_conv_w, rg_conv_b, rg_gate_a_w, rg_gate_a_b,
           rg_gate_x_w, rg_gate_x_b, rg_lambda, gdn_conv_w, gdn_a_log, gdn_dt_bias, gdn_norm):
    rg_w = rg_conv_w.shape[1]
    qkv_w = gdn_conv_w.shape[1]
    gdn_vw = GDN_HEADS * HEAD_DIM
    n_main = 2 * rg_w + qkv_w + gdn_vw
    n_gate = w_in.shape[1] - n_main
    w_main = w_in[:, :n_main].astype(BF16)
    w_ba = jnp.zeros((w_in.shape[0], LANES), F32).at[:, :n_gate].set(w_in[:, n_main:]).astype(BF16)
    p, ba = _in_proj(x1, mix_norm, w_main, w_ba, rg_conv_w, rg_conv_b, gdn_conv_w, seq=seq)

    w_gates, b_gates = _rg_gate_weights(rg_gate_a_w, rg_gate_a_b, rg_gate_x_w, rg_gate_x_b)
    y_rg = _rglru(p, w_gates, b_gates, rg_lambda.astype(F32), batch=batch, seq=seq, gate_col0=rg_w // LANES)

    col, row = _gate_prep(ba, gdn_a_log, gdn_dt_bias)
    q_col = 2 * rg_w // gdn_vw
    o_f = _gdn(p, col, row, q_col=q_col, batch=batch, seq=seq, d=0)
    o_b = _gdn(p, col, row, q_col=q_col, batch=batch, seq=seq, d=1)
    return y_rg, o_f, o_b, p, (2 * rg_w + qkv_w) // gdn_vw, gdn_norm, w_out.astype(BF16)


def kernel(x, ffn1_norm, ffn1_w_gate, ffn1_w_up, ffn1_w_down, mix_norm, w_in, w_out, rg_conv_w, rg_conv_b, rg_gate_a_w, rg_gate_a_b, rg_gate_x_w, rg_gate_x_b, rg_lambda, gdn_conv_w, gdn_a_log, gdn_dt_bias, gdn_norm, ffn2_norm, ffn2_w_gate, ffn2_w_up, ffn2_w_down, final_norm):
    batch, seq, d_model = x.shape
    depth = ffn1_norm.shape[0]
    h = x.reshape(batch * seq, d_model)
    for l in range(depth):
        last = l == depth - 1
        h = _ffn(h, ffn1_norm[l], ffn1_w_gate[l].astype(BF16), ffn1_w_up[l].astype(BF16),
                 ffn1_w_down[l].astype(BF16), final_norm, final_norm=False)
        mixer = _mixer(h, batch, seq, mix_norm[l], w_in[l], w_out[l], rg_conv_w[l], rg_conv_b[l],
                   rg_gate_a_w[l], rg_gate_a_b[l], rg_gate_x_w[l], rg_gate_x_b[l], rg_lambda[l],
                   gdn_conv_w[l], gdn_a_log[l], gdn_dt_bias[l], gdn_norm[l])
        h = _ffn(h, ffn2_norm[l], ffn2_w_gate[l].astype(BF16), ffn2_w_up[l].astype(BF16),
                 ffn2_w_down[l].astype(BF16), final_norm, final_norm=last, mixer=mixer)
    return h.reshape(batch, seq, d_model)
```

```python
import functools
import itertools

import jax
import jax.numpy as jnp
from jax import lax
from jax.experimental import pallas as pl
from jax.experimental.pallas import tpu as pltpu

F32 = jnp.float32
BF16 = jnp.bfloat16

EPS = 1e-6
RG_C = 8.0
RG_BLOCKS = 8
GDN_HEADS = 4
HEAD_DIM = 128
CHUNK = 64
LANES = 128
SUBLANES = 8
VMEM_LIMIT = 56 * 1024 * 1024


def _cparams(sem):
    return pltpu.CompilerParams(dimension_semantics=sem, vmem_limit_bytes=VMEM_LIMIT)


def _rms(x):
    return x * lax.rsqrt(jnp.mean(x * x, axis=-1, keepdims=True) + EPS)


def _sigmoid(x):
    return 1.0 / (1.0 + jnp.exp(-x))


def _softplus(x):
    return jnp.maximum(x, 0.0) + jnp.log(1.0 + jnp.exp(-jnp.abs(x)))


def _gated_mixer_rows(yrg_ref, of_ref, ob_ref, z_ref, gn_ref, wo_ref, rows):
    o = of_ref[rows, :] + ob_ref[rows, :]
    z = z_ref[rows, :]
    gn = gn_ref[...]
    parts = [yrg_ref[rows, :].astype(BF16)]
    for h in range(GDN_HEADS):
        hs = slice(h * HEAD_DIM, (h + 1) * HEAD_DIM)
        zh = z[:, hs]
        parts.append((_rms(o[:, hs]) * gn * (zh * _sigmoid(zh))).astype(BF16))
    return jnp.dot(jnp.concatenate(parts, axis=-1), wo_ref[...], preferred_element_type=F32)


def _ffn_kernel(*refs, final_norm, n_sub, with_mixer):
    if with_mixer:
        x_ref, mixer_refs, (g_ref, wg_ref, wu_ref, wd_ref, fg_ref, o_ref) = refs[0], refs[1:7], refs[7:]
    else:
        x_ref, g_ref, wg_ref, wu_ref, wd_ref, fg_ref, o_ref = refs
    sub = x_ref.shape[0] // n_sub
    g = g_ref[...]
    for r in range(n_sub):
        rows = slice(r * sub, (r + 1) * sub)
        x = x_ref[rows, :]
        if with_mixer:
            x = x + _gated_mixer_rows(*mixer_refs, rows)
        h = (_rms(x) * g).astype(BF16)
        gate = jnp.dot(h, wg_ref[...], preferred_element_type=F32)
        up = jnp.dot(h, wu_ref[...], preferred_element_type=F32)
        act = (gate * _sigmoid(gate) * up).astype(BF16)
        y = x + 0.5 * jnp.dot(act, wd_ref[...], preferred_element_type=F32)
        if final_norm:
            y = _rms(y) * fg_ref[...]
        o_ref[rows, :] = y


def _ffn(x, g, wg, wu, wd, fg, *, final_norm, mixer=None, tm=512, n_sub=2):
    m, d = x.shape
    f = wg.shape[1]
    assert m % tm == 0 and tm % n_sub == 0
    resident = dict(pipeline_mode=pl.Buffered(1))
    row_spec = lambda w, col=0: pl.BlockSpec((tm, w), lambda i: (i, col))
    const_spec = lambda shape, **kw: pl.BlockSpec(shape, lambda i: (0, 0), **kw)
    operands, in_specs = [x], [row_spec(d)]
    if mixer is not None:
        y_rg, o_f, o_b, p, z_col, gdn_norm, w_out = mixer
        wr, wv = y_rg.shape[1], o_f.shape[1]
        operands += [y_rg, o_f, o_b, p, gdn_norm.reshape(1, -1), w_out]
        in_specs += [row_spec(wr), row_spec(wv), row_spec(wv), row_spec(wv, z_col),
                     const_spec((1, HEAD_DIM)), const_spec((wr + wv, d), **resident)]
    operands += [g.reshape(1, d), wg, wu, wd, fg.reshape(1, d)]
    in_specs += [const_spec((1, d)), const_spec((d, f), **resident), const_spec((d, f), **resident),
                 const_spec((f, d), **resident), const_spec((1, d))]
    return pl.pallas_call(
        functools.partial(_ffn_kernel, final_norm=final_norm, n_sub=n_sub, with_mixer=mixer is not None),
        grid=(m // tm,),
        in_specs=in_specs,
        out_specs=row_spec(d),
        out_shape=jax.ShapeDtypeStruct((m, d), F32),
        compiler_params=_cparams(("parallel",)),
        name="ffn_final" if final_norm else "ffn",
    )(*operands)


def _inproj_kernel(x_ref, xp_ref, xn_ref, g_ref, w_ref, cwr_ref, cbr_ref, cwq_ref, p_ref, ba_ref, stage_scr, *,
                   n_sub, tiles_per_seq, rg_w, qkv_w, n_qk, q_scale):
    tm, n = p_ref.shape
    sub = tm // n_sub
    hal = SUBLANES
    i = pl.program_id(0)
    pos = i % tiles_per_seq
    prev = jnp.where(pos == 0, 0.0, xp_ref[...])
    nxt = jnp.where(pos == tiles_per_seq - 1, 0.0, xn_ref[...])
    g = g_ref[...]
    cwr = cwr_ref[...]
    mid = slice(hal, hal + sub)
    blk = GDN_HEADS * HEAD_DIM

    def stage(slot, pw):
        for j in range(blk // LANES):
            stage_scr[slot * (blk // LANES) + j] = pw[:, j * LANES:(j + 1) * LANES]

    def conv(slab, cw):
        tap = lambda off: stage_scr[slab, hal + off:hal + off + sub, :]
        acc = cw[0:1, :] * tap(-2) + cw[1:2, :] * tap(-1)
        return acc + cw[2:3, :] * tap(0) + cw[3:4, :] * tap(1)

    for r in range(n_sub):
        lo, hi = r * sub - hal, (r + 1) * sub + hal
        parts = ([prev] if lo < 0 else []) + [x_ref[max(lo, 0):min(hi, tm), :]] + ([nxt] if hi > tm else [])
        xw = jnp.concatenate(parts, axis=0)
        h = (_rms(xw) * g).astype(BF16)
        rows = slice(r * sub, (r + 1) * sub)
        proj = lambda c0, c1: jnp.dot(h, w_ref[:, c0:c1], preferred_element_type=F32)
        stage(0, proj(0, rg_w))
        for c in range(0, rg_w, LANES):
            p_ref[rows, c:c + LANES] = conv(c // LANES, cwr[:, c:c + LANES]) + cbr_ref[:, c:c + LANES]
        p_ref[rows, rg_w:2 * rg_w] = proj(rg_w, 2 * rg_w)[mid]
        for c0 in range(0, qkv_w, blk):
            slot = 1 + c0 // blk
            stage(slot, proj(2 * rg_w + c0, 2 * rg_w + c0 + blk))
            for c in range(c0, c0 + blk, LANES):
                y = conv((blk + c) // LANES, cwq_ref[:, c:c + LANES])
                y = y * _sigmoid(y)
                if c < n_qk * LANES:
                    scale = q_scale if c < n_qk * LANES // 2 else 1.0
                    y = y * (lax.rsqrt(jnp.sum(y * y, axis=-1, keepdims=True) + EPS) * scale)
                p_ref[rows, 2 * rg_w + c:2 * rg_w + c + LANES] = y
        tail = proj(2 * rg_w + qkv_w, n + LANES)[mid]
        p_ref[rows, 2 * rg_w + qkv_w:n] = tail[:, :n - 2 * rg_w - qkv_w]
        ba_ref[rows, :] = tail[:, n - 2 * rg_w - qkv_w:]


def _in_proj(x, g, w_main, w_ba, rg_conv_w, rg_conv_b, qkv_conv_w, *, seq, tm=512, n_sub=2):
    m, d = x.shape
    n = w_main.shape[1]
    rg_w, qkv_w = rg_conv_w.shape[1], qkv_conv_w.shape[1]
    w_all = jnp.concatenate([w_main, w_ba], axis=1)
    assert m % tm == 0 and tm % n_sub == 0 and seq % tm == 0
    assert rg_w == GDN_HEADS * HEAD_DIM and qkv_w % rg_w == 0
    hb = tm // SUBLANES
    const_spec = lambda shape, **kw: pl.BlockSpec(shape, lambda i: (0, 0), **kw)
    return pl.pallas_call(
        functools.partial(_inproj_kernel, n_sub=n_sub, tiles_per_seq=seq // tm, rg_w=rg_w, qkv_w=qkv_w,
                          n_qk=2 * GDN_HEADS, q_scale=HEAD_DIM ** -0.5),
        grid=(m // tm,),
        in_specs=[
            pl.BlockSpec((tm, d), lambda i: (i, 0)),
            pl.BlockSpec((SUBLANES, d), lambda i: (jnp.maximum(i * hb - 1, 0), 0)),
            pl.BlockSpec((SUBLANES, d), lambda i: (jnp.minimum((i + 1) * hb, m // SUBLANES - 1), 0)),
            const_spec((1, d)),
            const_spec((d, n + LANES), pipeline_mode=pl.Buffered(1)),
            const_spec((4, rg_w)),
            const_spec((1, rg_w)),
            const_spec((4, qkv_w)),
        ],
        out_specs=[
            pl.BlockSpec((tm, n), lambda i: (i, 0)),
            pl.BlockSpec((tm, LANES), lambda i: (i, 0)),
        ],
        out_shape=[jax.ShapeDtypeStruct((m, n), F32), jax.ShapeDtypeStruct((m, LANES), F32)],
        scratch_shapes=[pltpu.VMEM(((rg_w + qkv_w) // LANES, tm // n_sub + 2 * SUBLANES, LANES), F32)],
        compiler_params=_cparams(("parallel",)),
        name="in_proj",
    )(x, x, x, g.reshape(1, d), w_all, rg_conv_w, rg_conv_b.reshape(1, -1), qkv_conv_w)


RG_SEGS = 2 * SUBLANES
RG_PAD = 8


def _rglru_kernel(x_ref, gate_ref, wg_ref, bg_ref, lam_ref, o_ref,
                  af_scr, bf_scr, ab_scr, bb_scr, *, seq, rt):
    seg_len = seq // RG_SEGS
    pitch = seg_len + RG_PAD
    tiles_per_seg = seg_len // rt
    n_tiles = seq // rt
    a_scr = (af_scr, ab_scr)
    b_scr = (bf_scr, bb_scr)

    bg = bg_ref[...]
    neg_c_sp = -RG_C * _softplus(-lam_ref[...])

    def gates(i, _):
        r0 = pl.multiple_of(i * rt, rt)
        xc = x_ref[pl.ds(r0, rt), :]
        pre = jnp.dot(xc.astype(BF16), wg_ref[...], preferred_element_type=F32) + bg
        seg = i // tiles_per_seg
        off = pl.multiple_of(seg * pitch + (i - seg * tiles_per_seg) * rt, SUBLANES)
        for d in range(2):
            r = _sigmoid(pre[:, (2 * d) * LANES:(2 * d + 1) * LANES])
            ig = _sigmoid(pre[:, (2 * d + 1) * LANES:(2 * d + 2) * LANES])
            log_a = r * neg_c_sp[d:d + 1, :]
            a = jnp.exp(log_a)
            om = 1.0 - a * a
            b = jnp.where(om > 0.0, om * lax.rsqrt(om), 0.0) * (ig * xc)
            a_scr[d][pl.ds(off, rt), :] = a
            b_scr[d][pl.ds(off, rt), :] = b
        return 0

    lax.fori_loop(0, n_tiles, gates, 0)

    def scan(t, carry):
        hf, pf, hb, pb = carry
        idx = pl.ds(t, RG_SEGS, stride=pitch)
        a = af_scr[idx, :]
        hf = a * hf + bf_scr[idx, :]
        pf = a * pf
        bf_scr[idx, :] = hf
        af_scr[idx, :] = pf
        idx = pl.ds(seg_len - 1 - t, RG_SEGS, stride=pitch)
        a = ab_scr[idx, :]
        hb = a * hb + bb_scr[idx, :]
        pb = a * pb
        bb_scr[idx, :] = hb
        ab_scr[idx, :] = pb
        return hf, pf, hb, pb

    zeros = jnp.zeros((RG_SEGS, LANES), F32)
    ones = jnp.ones((RG_SEGS, LANES), F32)
    hf, pf, hb, pb = lax.fori_loop(0, seg_len, scan, (zeros, ones, zeros, ones), unroll=8)

    cf = [jnp.zeros((1, LANES), F32)]
    for s in range(1, RG_SEGS):
        cf.append(hf[s - 1:s, :] + pf[s - 1:s, :] * cf[s - 1])
    cbk = [None] * RG_SEGS
    cbk[RG_SEGS - 1] = jnp.zeros((1, LANES), F32)
    for s in range(RG_SEGS - 2, -1, -1):
        cbk[s] = hb[s + 1:s + 2, :] + pb[s + 1:s + 2, :] * cbk[s + 1]

    for s in range(RG_SEGS):
        def fix(k, _, s=s):
            off = pl.multiple_of(s * pitch + k * rt, SUBLANES)
            rows = pl.multiple_of(s * seg_len + k * rt, SUBLANES)
            h = bf_scr[pl.ds(off, rt), :] + af_scr[pl.ds(off, rt), :] * cf[s]
            h = h + (bb_scr[pl.ds(off, rt), :] + ab_scr[pl.ds(off, rt), :] * cbk[s])
            o_ref[pl.ds(rows, rt), :] = h * jax.nn.gelu(gate_ref[pl.ds(rows, rt), :], approximate=True)
            return 0

        lax.fori_loop(0, tiles_per_seg, fix, 0)


def _rglru(p, w_gates, b_gates, lam, *, batch, seq, gate_col0, rt=512):
    ngrp = w_gates.shape[0]
    seg_len = seq // RG_SEGS
    rt = min(rt, seg_len)
    assert seq % RG_SEGS == 0 and seg_len % rt == 0
    scr = pltpu.VMEM((RG_SEGS * (seg_len + RG_PAD), LANES), F32)
    return pl.pallas_call(
        functools.partial(_rglru_kernel, seq=seq, rt=rt),
        grid=(batch, ngrp),
        in_specs=[
            pl.BlockSpec((seq, LANES), lambda b, c: (b, c)),
            pl.BlockSpec((seq, LANES), lambda b, c: (b, gate_col0 + c)),
            pl.BlockSpec((None, LANES, 4 * LANES), lambda b, c: (c, 0, 0)),
            pl.BlockSpec((None, 1, 4 * LANES), lambda b, c: (c, 0, 0)),
            pl.BlockSpec((2, LANES), lambda b, c: (0, c)),
        ],
        out_specs=pl.BlockSpec((seq, LANES), lambda b, c: (b, c)),
        out_shape=jax.ShapeDtypeStruct((batch * seq, ngrp * LANES), F32),
        scratch_shapes=[scr, scr, scr, scr],
        compiler_params=_cparams(("parallel", "parallel")),
        name="rglru",
    )(p, p, w_gates, b_gates, lam)


def _rg_gate_weights(wa, ba, wx, bx):
    ndir, nblk, blk, _ = wa.shape
    ngrp = nblk * blk // LANES
    per = LANES // blk

    def bd(w):
        w = w.reshape(ndir, ngrp, per, blk, blk)
        eye = jnp.eye(per, dtype=w.dtype)
        return jnp.einsum("dgpij,pq->dgpiqj", w, eye).reshape(ndir, ngrp, LANES, LANES)

    a, x = bd(wa), bd(wx)
    w = jnp.concatenate([a[0], x[0], a[1], x[1]], axis=-1)
    ba = ba.reshape(ndir, ngrp, 1, LANES)
    bx = bx.reshape(ndir, ngrp, 1, LANES)
    b = jnp.concatenate([ba[0], bx[0], ba[1], bx[1]], axis=-1)
    return w.astype(BF16), b.astype(F32)


def _gate_kernel(ba_ref, alog_ref, dtb_ref, col_ref, row_ref, *, tm):
    nh = GDN_HEADS
    ri = lax.broadcasted_iota(jnp.int32, (LANES, LANES), 0)
    ci = lax.broadcasted_iota(jnp.int32, (LANES, LANES), 1)
    same = (ri // CHUNK) == (ci // CHUNK)
    lower = jnp.where(same & (ri >= ci), 1.0, 0.0).astype(F32)
    upper = jnp.where(same & (ri <= ci), 1.0, 0.0).astype(F32)
    lane = ci
    neg_a = -jnp.exp(alog_ref[...])
    dtb = dtb_ref[...]
    for k in range(tm // LANES):
        sl = slice(k * LANES, (k + 1) * LANES)
        raw = ba_ref[sl, :]
        beta = _sigmoid(raw)
        gk = neg_a * _softplus(raw + dtb)
        gk = jnp.where((lane >= 2 * nh) & (lane < 4 * nh), gk, 0.0)
        cf = jnp.dot(lower, gk, preferred_element_type=F32, precision=lax.Precision.HIGHEST)
        cb = jnp.dot(upper, gk, preferred_element_type=F32, precision=lax.Precision.HIGHEST)
        gc = jnp.where(lane < 3 * nh, cf, cb)
        col = jnp.where(lane < 2 * nh, beta, gc)
        col_ref[sl, :] = col
        colt = col.T
        for q in range(LANES // CHUNK):
            rows = [jnp.concatenate([colt[(2 + dr) * nh + h:(2 + dr) * nh + h + 1, q * CHUNK:(q + 1) * CHUNK]
                                     for h in range(nh)], axis=1) for dr in range(2)]
            rows.append(jnp.zeros((SUBLANES - 2, nh * CHUNK), F32))
            row_ref[k * (LANES // CHUNK) + q] = jnp.concatenate(rows, axis=0)


def _gate_prep(ba, a_log, dt_bias, *, tm=512):
    m = ba.shape[0]
    nh = GDN_HEADS
    pad = lambda v: jnp.zeros((1, LANES), F32).at[0, 2 * nh:4 * nh].set(v.reshape(-1).astype(F32))
    return pl.pallas_call(
        functools.partial(_gate_kernel, tm=tm),
        grid=(m // tm,),
        in_specs=[
            pl.BlockSpec((tm, LANES), lambda i: (i, 0)),
            pl.BlockSpec((1, LANES), lambda i: (0, 0)),
            pl.BlockSpec((1, LANES), lambda i: (0, 0)),
        ],
        out_specs=[
            pl.BlockSpec((tm, LANES), lambda i: (i, 0)),
            pl.BlockSpec((tm // CHUNK, SUBLANES, nh * CHUNK), lambda i: (i, 0, 0)),
        ],
        out_shape=[jax.ShapeDtypeStruct((m, LANES), F32),
                   jax.ShapeDtypeStruct((m // CHUNK, SUBLANES, nh * CHUNK), F32)],
        compiler_params=_cparams(("parallel",)),
        name="gate_prep",
    )(ba, pad(a_log), pad(dt_bias))


def _bmm(a, b):
    return lax.dot_general(a.astype(BF16), b.astype(BF16), (((2,), (1,)), ((0,), (0,))),
                           preferred_element_type=F32)


def _bmm_nt(a, b):
    return lax.dot_general(a.astype(BF16), b.astype(BF16), (((2,), (2,)), ((0,), (0,))),
                           preferred_element_type=F32)


def _bmm_tn(a, b):
    return lax.dot_general(a.astype(BF16), b.astype(BF16), (((1,), (1,)), ((0,), (0,))),
                           preferred_element_type=F32)


def _unit_tri_inverse_levels(lm, ri, ci, bdmask, nb):
    def blockdiag(a):
        ab = a.astype(BF16)
        return jnp.where(bdmask, jnp.concatenate([ab] * nb, axis=1), jnp.zeros((), BF16))

    eye = jnp.where(ri == ci, 1.0, 0.0).astype(F32)
    x = eye - jnp.where((ri // 2) == (ci // 2), lm, 0.0)
    s = 2
    while s < CHUNK:
        cm = jnp.where(((ri // (2 * s)) == (ci // (2 * s))) & ((ri // s) != (ci // s)), lm, 0.0)
        x = x - _bmm(_bmm(x, blockdiag(cm)), blockdiag(x))
        yield x
        s *= 2


def _gdn_chunk_local(q_ref, k_ref, v_ref, col_ref, row_ref, dst, *, reverse, g, d):
    u_scr, wq_scr, kd_scr, at_scr, cd_scr = dst
    nh, c, hd = GDN_HEADS, CHUNK, HEAD_DIM
    pw = nh * c
    fw = nh * hd
    ri = lax.broadcasted_iota(jnp.int32, (1, c, pw), 1)
    ci = lax.broadcasted_iota(jnp.int32, (1, c, pw), 2) % c
    if reverse:
        incl, strict, last = ri <= ci, ri < ci, 0
    else:
        incl, strict, last = ri >= ci, ri > ci, c - 1
    bdmask = (lax.broadcasted_iota(jnp.int32, (1, pw, pw), 1) // c
              == lax.broadcasted_iota(jnp.int32, (1, pw, pw), 2) // c)
    kmask = (lax.broadcasted_iota(jnp.int32, (1, pw, fw), 1) // c
             == lax.broadcasted_iota(jnp.int32, (1, pw, fw), 2) // hd)
    low_half = lax.broadcasted_iota(jnp.int32, (g * c, hd), 1) < c

    col = col_ref[...]
    bcast = lambda j: jnp.broadcast_to(col[:, j:j + 1], (g * c, hd))
    beta = jnp.concatenate([bcast(d * nh + h) for h in range(nh)], axis=1).reshape(g, c, fw)
    gcs = [bcast((2 + d) * nh + h) for h in range(nh)]
    gc = jnp.concatenate(gcs, axis=1).reshape(g, c, fw)
    gc_col = jnp.concatenate([jnp.where(low_half, gcs[h], gcs[h + 1]) for h in range(0, nh, 2)],
                             axis=1).reshape(g, c, pw)
    gc_row = row_ref[:, d:d + 1, :]
    g_last = gc[:, last:last + 1, :]
    eg = jnp.exp(gc)
    decay = jnp.where(incl, jnp.exp(gc_col - gc_row), 0.0)
    q = q_ref[...].reshape(g, c, fw)
    k = k_ref[...].reshape(g, c, fw)
    v = v_ref[...].reshape(g, c, fw)
    k_beta = k * beta
    v_beta = v * beta
    kb16 = k.astype(BF16)
    k_bd = jnp.where(kmask, jnp.concatenate([kb16] * nh, axis=1), jnp.zeros((), BF16))
    kq = _bmm_nt(jnp.concatenate([k_beta, q], axis=1), k_bd)
    lm = jnp.where(strict, kq[:, :c] * decay, 0.0)
    attn = kq[:, c:] * decay
    yield
    for t in _unit_tri_inverse_levels(lm, ri, ci, bdmask, nh):
        yield
    kbe = k_beta * eg
    qd = q * eg
    kd = k * jnp.exp(g_last - gc)
    cd = jnp.exp(g_last)
    for h in range(nh):
        hs = slice(h * hd, (h + 1) * hd)
        ps = slice(h * c, (h + 1) * c)
        uw = _bmm(t[:, :, ps], jnp.concatenate([v_beta[:, :, hs], kbe[:, :, hs]], axis=-1))
        u_scr[h] = uw[:, :, :hd]
        wq_scr[h] = jnp.concatenate([uw[:, :, hd:], qd[:, :, hs]], axis=1).astype(BF16)
        at_scr[h] = attn[:, :, ps].astype(BF16)
        kd_scr[h] = kd[:, :, hs].astype(BF16)
        cd_scr[h] = cd[:, :, hs]
        if h % 2 == 1:
            yield


def _gdn_recurrence(src, s_scr, o_ref, *, reverse, g):
    u_scr, wq_scr, kd_scr, at_scr, cd_scr = src
    nh, c, hd = GDN_HEADS, CHUNK, HEAD_DIM
    for step in range(g):
        n = g - 1 - step if reverse else step
        state = s_scr[...]
        sb = state.astype(BF16)
        ws = _bmm(wq_scr[:, n], sb)
        v_new = u_scr[:, n] - ws[:, :c]
        vb = v_new.astype(BF16)
        o = ws[:, c:] + _bmm(at_scr[:, n], vb)
        s_scr[...] = state * cd_scr[:, n] + _bmm_tn(kd_scr[:, n], vb)
        for h in range(nh):
            o_ref[n * c:(n + 1) * c, h * hd:(h + 1) * hd] = o[h]
        yield


def _gdn_kernel(q_ref, k_ref, v_ref, col_ref, row_ref, o_ref, s_scr, *sets, reverse, n_chunks, d):
    set0, set1 = sets[:len(sets) // 2], sets[len(sets) // 2:]
    i = pl.program_id(1)

    @pl.when(i == 0)
    def _():
        s_scr[...] = jnp.zeros_like(s_scr)
        for r in set1:
            r[...] = jnp.zeros_like(r)

    def step(dst, src):
        local = _gdn_chunk_local(q_ref, k_ref, v_ref, col_ref, row_ref, dst, reverse=reverse, g=n_chunks, d=d)
        recur = _gdn_recurrence(src, s_scr, o_ref, reverse=reverse, g=n_chunks)
        for _ in itertools.zip_longest(local, recur):
            pass

    @pl.when(i % 2 == 0)
    def _():
        step(set0, set1)

    @pl.when(i % 2 == 1)
    def _():
        step(set1, set0)


def _gdn(p, col, row, *, q_col, batch, seq, d, tt=512):
    nh = GDN_HEADS
    width = nh * HEAD_DIM
    nt = seq // tt
    n_chunks = tt // CHUNK
    reverse = d == 1
    order = (lambda t: nt - 1 - t) if reverse else (lambda t: t)
    tile_in = lambda b, i: b * nt + order(jnp.minimum(i, nt - 1))
    tile_out = lambda b, i: b * nt + order(jnp.maximum(i - 1, 0))
    scratch_set = [
        pltpu.VMEM((nh, n_chunks, CHUNK, HEAD_DIM), F32),
        pltpu.VMEM((nh, n_chunks, 2 * CHUNK, HEAD_DIM), BF16),
        pltpu.VMEM((nh, n_chunks, CHUNK, HEAD_DIM), BF16),
        pltpu.VMEM((nh, n_chunks, CHUNK, CHUNK), BF16),
        pltpu.VMEM((nh, n_chunks, 1, HEAD_DIM), F32),
    ]
    return pl.pallas_call(
        functools.partial(_gdn_kernel, reverse=reverse, n_chunks=n_chunks, d=d),
        grid=(batch, nt + 1),
        in_specs=[
            pl.BlockSpec((tt, width), lambda b, i: (tile_in(b, i), q_col)),
            pl.BlockSpec((tt, width), lambda b, i: (tile_in(b, i), q_col + 1)),
            pl.BlockSpec((tt, width), lambda b, i: (tile_in(b, i), q_col + 2)),
            pl.BlockSpec((tt, LANES), lambda b, i: (tile_in(b, i), 0)),
            pl.BlockSpec((n_chunks, SUBLANES, nh * CHUNK), lambda b, i: (tile_in(b, i), 0, 0)),
        ],
        out_specs=pl.BlockSpec((tt, width), lambda b, i: (tile_out(b, i), 0)),
        out_shape=jax.ShapeDtypeStruct((batch * seq, width), F32),
        scratch_shapes=[pltpu.VMEM((nh, HEAD_DIM, HEAD_DIM), F32)] + scratch_set + scratch_set,
        compiler_params=_cparams(("parallel", "arbitrary")),
        name="gdn_bwd" if reverse else "gdn_fwd",
    )(p, p, p, col, row)


def _mixer(x1, batch, seq, mix_norm, w_in, w_out, rg_conv_w, rg_conv_b, rg_gate_a_w, rg_gate_a_b,
           rg_gate_x_w, rg_gate_x_b, rg_lambda, gdn_conv_w, gdn_a_log, gdn_dt_bias, gdn_norm):
    rg_w = rg_conv_w.shape[1]
    qkv_w = gdn_conv_w.shape[1]
    gdn_vw = GDN_HEADS * HEAD_DIM
    n_main = 2 * rg_w + qkv_w + gdn_vw
    n_gate = w_in.shape[1] - n_main
    w_main = w_in[:, :n_main].astype(BF16)
    w_ba = jnp.zeros((w_in.shape[0], LANES), F32).at[:, :n_gate].set(w_in[:, n_main:]).astype(BF16)
    p, ba = _in_proj(x1, mix_norm, w_main, w_ba, rg_conv_w, rg_conv_b, gdn_conv_w, seq=seq)

    w_gates, b_gates = _rg_gate_weights(rg_gate_a_w, rg_gate_a_b, rg_gate_x_w, rg_gate_x_b)
    y_rg = _rglru(p, w_gates, b_gates, rg_lambda.astype(F32), batch=batch, seq=seq, gate_col0=rg_w // LANES)

    col, row = _gate_prep(ba, gdn_a_log, gdn_dt_bias)
    q_col = 2 * rg_w // gdn_vw
    o_f = _gdn(p, col, row, q_col=q_col, batch=batch, seq=seq, d=0)
    o_b = _gdn(p, col, row, q_col=q_col, batch=batch, seq=seq, d=1)
    return y_rg, o_f, o_b, p, (2 * rg_w + qkv_w) // gdn_vw, gdn_norm, w_out.astype(BF16)


def kernel(x, ffn1_norm, ffn1_w_gate, ffn1_w_up, ffn1_w_down, mix_norm, w_in, w_out, rg_conv_w, rg_conv_b, rg_gate_a_w, rg_gate_a_b, rg_gate_x_w, rg_gate_x_b, rg_lambda, gdn_conv_w, gdn_a_log, gdn_dt_bias, gdn_norm, ffn2_norm, ffn2_w_gate, ffn2_w_up, ffn2_w_down, final_norm):
    batch, seq, d_model = x.shape
    depth = ffn1_norm.shape[0]
    h = x.reshape(batch * seq, d_model)
    for l in range(depth):
        last = l == depth - 1
        h = _ffn(h, ffn1_norm[l], ffn1_w_gate[l].astype(BF16), ffn1_w_up[l].astype(BF16),
                 ffn1_w_down[l].astype(BF16), final_norm, final_norm=False)
        mixer = _mixer(h, batch, seq, mix_norm[l], w_in[l], w_out[l], rg_conv_w[l], rg_conv_b[l],
                   rg_gate_a_w[l], rg_gate_a_b[l], rg_gate_x_w[l], rg_gate_x_b[l], rg_lambda[l],
                   gdn_conv_w[l], gdn_a_log[l], gdn_dt_bias[l], gdn_norm[l])
        h = _ffn(h, ffn2_norm[l], ffn2_w_gate[l].astype(BF16), ffn2_w_up[l].astype(BF16),
                 ffn2_w_down[l].astype(BF16), final_norm, final_norm=last, mixer=mixer)
    return h.reshape(batch, seq, d_model)
```

```python
import functools
import itertools

import jax
import jax.numpy as jnp
from jax import lax
from jax.experimental import pallas as pl
from jax.experimental.pallas import tpu as pltpu

F32 = jnp.float32
BF16 = jnp.bfloat16

EPS = 1e-6
LOG2_E = 1.4426950408889634
RG_C = 8.0
RG_BLOCKS = 8
GDN_HEADS = 4
HEAD_DIM = 128
CHUNK = 64
LANES = 128
SUBLANES = 8
VMEM_LIMIT = 56 * 1024 * 1024


def _cparams(sem):
    return pltpu.CompilerParams(dimension_semantics=sem, vmem_limit_bytes=VMEM_LIMIT)


def _rms(x):
    return x * lax.rsqrt(jnp.mean(x * x, axis=-1, keepdims=True) + EPS)


def _sigmoid(x):
    return 1.0 / (1.0 + jnp.exp(-x))


def _softplus(x):
    return jnp.maximum(x, 0.0) + jnp.log(1.0 + jnp.exp(-jnp.abs(x)))


def _gated_mixer_rows(yrg_ref, of_ref, ob_ref, z_ref, gn_ref, wo_ref, rows):
    o = of_ref[rows, :] + ob_ref[rows, :]
    z = z_ref[rows, :]
    gn = gn_ref[...]
    parts = [yrg_ref[rows, :].astype(BF16)]
    for h in range(GDN_HEADS):
        hs = slice(h * HEAD_DIM, (h + 1) * HEAD_DIM)
        zh = z[:, hs]
        parts.append((_rms(o[:, hs]) * gn * (zh * _sigmoid(zh))).astype(BF16))
    return jnp.dot(jnp.concatenate(parts, axis=-1), wo_ref[...], preferred_element_type=F32)


def _ffn_kernel(*refs, final_norm, n_sub, with_mixer):
    if with_mixer:
        x_ref, mixer_refs, (g_ref, wg_ref, wu_ref, wd_ref, fg_ref, o_ref) = refs[0], refs[1:7], refs[7:]
    else:
        x_ref, g_ref, wg_ref, wu_ref, wd_ref, fg_ref, o_ref = refs
    sub = x_ref.shape[0] // n_sub
    g = g_ref[...]
    for r in range(n_sub):
        rows = slice(r * sub, (r + 1) * sub)
        x = x_ref[rows, :]
        if with_mixer:
            x = x + _gated_mixer_rows(*mixer_refs, rows)
        h = (_rms(x) * g).astype(BF16)
        gate = jnp.dot(h, wg_ref[...], preferred_element_type=F32)
        up = jnp.dot(h, wu_ref[...], preferred_element_type=F32)
        act = (gate * _sigmoid(gate) * up).astype(BF16)
        y = x + 0.5 * jnp.dot(act, wd_ref[...], preferred_element_type=F32)
        if final_norm:
            y = _rms(y) * fg_ref[...]
        o_ref[rows, :] = y


def _ffn(x, g, wg, wu, wd, fg, *, final_norm, mixer=None, tm=512, n_sub=2):
    m, d = x.shape
    f = wg.shape[1]
    assert m % tm == 0 and tm % n_sub == 0
    resident = dict(pipeline_mode=pl.Buffered(1))
    row_spec = lambda w, col=0: pl.BlockSpec((tm, w), lambda i: (i, col))
    const_spec = lambda shape, **kw: pl.BlockSpec(shape, lambda i: (0, 0), **kw)
    operands, in_specs = [x], [row_spec(d)]
    if mixer is not None:
        y_rg, o_f, o_b, p, z_col, gdn_norm, w_out = mixer
        wr, wv = y_rg.shape[1], o_f.shape[1]
        operands += [y_rg, o_f, o_b, p, gdn_norm.reshape(1, -1), w_out]
        in_specs += [row_spec(wr), row_spec(wv), row_spec(wv), row_spec(wv, z_col),
                     const_spec((1, HEAD_DIM)), const_spec((wr + wv, d), **resident)]
    operands += [g.reshape(1, d), wg, wu, wd, fg.reshape(1, d)]
    in_specs += [const_spec((1, d)), const_spec((d, f), **resident), const_spec((d, f), **resident),
                 const_spec((f, d), **resident), const_spec((1, d))]
    return pl.pallas_call(
        functools.partial(_ffn_kernel, final_norm=final_norm, n_sub=n_sub, with_mixer=mixer is not None),
        grid=(m // tm,),
        in_specs=in_specs,
        out_specs=row_spec(d),
        out_shape=jax.ShapeDtypeStruct((m, d), F32),
        compiler_params=_cparams(("parallel",)),
        name="ffn_final" if final_norm else "ffn",
    )(*operands)


def _inproj_kernel(x_ref, xp_ref, xn_ref, g_ref, w_ref, cwr_ref, cbr_ref, cwq_ref, p_ref, ba_ref, stage_scr, *,
                   n_sub, tiles_per_seq, rg_w, qkv_w, n_qk, q_scale):
    tm, n = p_ref.shape
    sub = tm // n_sub
    hal = SUBLANES
    i = pl.program_id(0)
    pos = i % tiles_per_seq
    prev = jnp.where(pos == 0, 0.0, xp_ref[...])
    nxt = jnp.where(pos == tiles_per_seq - 1, 0.0, xn_ref[...])
    g = g_ref[...]
    cwr = cwr_ref[...]
    mid = slice(hal, hal + sub)
    blk = GDN_HEADS * HEAD_DIM

    def stage(slot, pw):
        for j in range(blk // LANES):
            stage_scr[slot * (blk // LANES) + j] = pw[:, j * LANES:(j + 1) * LANES]

    def conv(slab, cw):
        tap = lambda off: stage_scr[slab, hal + off:hal + off + sub, :]
        acc = cw[0:1, :] * tap(-2) + cw[1:2, :] * tap(-1)
        return acc + cw[2:3, :] * tap(0) + cw[3:4, :] * tap(1)

    for r in range(n_sub):
        lo, hi = r * sub - hal, (r + 1) * sub + hal
        parts = ([prev] if lo < 0 else []) + [x_ref[max(lo, 0):min(hi, tm), :]] + ([nxt] if hi > tm else [])
        xw = jnp.concatenate(parts, axis=0)
        h = (_rms(xw) * g).astype(BF16)
        rows = slice(r * sub, (r + 1) * sub)
        proj = lambda c0, c1: jnp.dot(h, w_ref[:, c0:c1], preferred_element_type=F32)
        stage(0, proj(0, rg_w))
        for c in range(0, rg_w, LANES):
            p_ref[rows, c:c + LANES] = conv(c // LANES, cwr[:, c:c + LANES]) + cbr_ref[:, c:c + LANES]
        p_ref[rows, rg_w:2 * rg_w] = proj(rg_w, 2 * rg_w)[mid]
        for c0 in range(0, qkv_w, blk):
            slot = 1 + c0 // blk
            stage(slot, proj(2 * rg_w + c0, 2 * rg_w + c0 + blk))
            for c in range(c0, c0 + blk, LANES):
                y = conv((blk + c) // LANES, cwq_ref[:, c:c + LANES])
                y = y * _sigmoid(y)
                if c < n_qk * LANES:
                    scale = q_scale if c < n_qk * LANES // 2 else 1.0
                    y = y * (lax.rsqrt(jnp.sum(y * y, axis=-1, keepdims=True) + EPS) * scale)
                p_ref[rows, 2 * rg_w + c:2 * rg_w + c + LANES] = y
        tail = proj(2 * rg_w + qkv_w, n + LANES)[mid]
        p_ref[rows, 2 * rg_w + qkv_w:n] = tail[:, :n - 2 * rg_w - qkv_w]
        ba_ref[rows, :] = tail[:, n - 2 * rg_w - qkv_w:]


def _in_proj(x, g, w_main, w_ba, rg_conv_w, rg_conv_b, qkv_conv_w, *, seq, tm=512, n_sub=2):
    m, d = x.shape
    n = w_main.shape[1]
    rg_w, qkv_w = rg_conv_w.shape[1], qkv_conv_w.shape[1]
    w_all = jnp.concatenate([w_main, w_ba], axis=1)
    assert m % tm == 0 and tm % n_sub == 0 and seq % tm == 0
    assert rg_w == GDN_HEADS * HEAD_DIM and qkv_w % rg_w == 0
    hb = tm // SUBLANES
    const_spec = lambda shape, **kw: pl.BlockSpec(shape, lambda i: (0, 0), **kw)
    return pl.pallas_call(
        functools.partial(_inproj_kernel, n_sub=n_sub, tiles_per_seq=seq // tm, rg_w=rg_w, qkv_w=qkv_w,
                          n_qk=2 * GDN_HEADS, q_scale=HEAD_DIM ** -0.5),
        grid=(m // tm,),
        in_specs=[
            pl.BlockSpec((tm, d), lambda i: (i, 0)),
            pl.BlockSpec((SUBLANES, d), lambda i: (jnp.maximum(i * hb - 1, 0), 0)),
            pl.BlockSpec((SUBLANES, d), lambda i: (jnp.minimum((i + 1) * hb, m // SUBLANES - 1), 0)),
            const_spec((1, d)),
            const_spec((d, n + LANES), pipeline_mode=pl.Buffered(1)),
            const_spec((4, rg_w)),
            const_spec((1, rg_w)),
            const_spec((4, qkv_w)),
        ],
        out_specs=[
            pl.BlockSpec((tm, n), lambda i: (i, 0)),
            pl.BlockSpec((tm, LANES), lambda i: (i, 0)),
        ],
        out_shape=[jax.ShapeDtypeStruct((m, n), F32), jax.ShapeDtypeStruct((m, LANES), F32)],
        scratch_shapes=[pltpu.VMEM(((rg_w + qkv_w) // LANES, tm // n_sub + 2 * SUBLANES, LANES), F32)],
        compiler_params=_cparams(("parallel",)),
        name="in_proj",
    )(x, x, x, g.reshape(1, d), w_all, rg_conv_w, rg_conv_b.reshape(1, -1), qkv_conv_w)


RG_SEGS = 2 * SUBLANES
RG_PAD = 8


def _rglru_kernel(x_ref, gate_ref, wg_ref, bg_ref, lam_ref, o_ref,
                  af_scr, bf_scr, ab_scr, bb_scr, *, seq, rt):
    seg_len = seq // RG_SEGS
    pitch = seg_len + RG_PAD
    tiles_per_seg = seg_len // rt
    n_tiles = seq // rt
    a_scr = (af_scr, ab_scr)
    b_scr = (bf_scr, bb_scr)

    bg = bg_ref[...]
    half_log2_a = (-0.5 * RG_C * LOG2_E) * _softplus(-lam_ref[...])

    def gates(i, _):
        r0 = pl.multiple_of(i * rt, rt)
        xc = x_ref[pl.ds(r0, rt), :]
        pre = jnp.dot(xc.astype(BF16), wg_ref[...], preferred_element_type=F32) + bg
        hx = 0.5 * xc
        seg = i // tiles_per_seg
        off = pl.multiple_of(seg * pitch + (i - seg * tiles_per_seg) * rt, SUBLANES)
        for d in range(2):
            tr = jnp.tanh(pre[:, (2 * d) * LANES:(2 * d + 1) * LANES])
            ti = jnp.tanh(pre[:, (2 * d + 1) * LANES:(2 * d + 2) * LANES])
            hl = half_log2_a[d:d + 1, :]
            a = jnp.exp2(hl + hl * tr)
            om = 1.0 - a * a
            b = jnp.where(om > 0.0, om * lax.rsqrt(om), 0.0) * (hx + hx * ti)
            a_scr[d][pl.ds(off, rt), :] = a
            b_scr[d][pl.ds(off, rt), :] = b
        return 0

    lax.fori_loop(0, n_tiles, gates, 0)

    def scan(t, carry):
        hf, pf, hb, pb = carry
        idx = pl.ds(t, RG_SEGS, stride=pitch)
        a = af_scr[idx, :]
        hf = a * hf + bf_scr[idx, :]
        pf = a * pf
        bf_scr[idx, :] = hf
        af_scr[idx, :] = pf
        idx = pl.ds(seg_len - 1 - t, RG_SEGS, stride=pitch)
        a = ab_scr[idx, :]
        hb = a * hb + bb_scr[idx, :]
        pb = a * pb
        bb_scr[idx, :] = hb
        ab_scr[idx, :] = pb
        return hf, pf, hb, pb

    zeros = jnp.zeros((RG_SEGS, LANES), F32)
    ones = jnp.ones((RG_SEGS, LANES), F32)
    hf, pf, hb, pb = lax.fori_loop(0, seg_len, scan, (zeros, ones, zeros, ones), unroll=8)

    cf = [jnp.zeros((1, LANES), F32)]
    for s in range(1, RG_SEGS):
        cf.append(hf[s - 1:s, :] + pf[s - 1:s, :] * cf[s - 1])
    cbk = [None] * RG_SEGS
    cbk[RG_SEGS - 1] = jnp.zeros((1, LANES), F32)
    for s in range(RG_SEGS - 2, -1, -1):
        cbk[s] = hb[s + 1:s + 2, :] + pb[s + 1:s + 2, :] * cbk[s + 1]

    for s in range(RG_SEGS):
        def fix(k, _, s=s):
            off = pl.multiple_of(s * pitch + k * rt, SUBLANES)
            rows = pl.multiple_of(s * seg_len + k * rt, SUBLANES)
            h = bf_scr[pl.ds(off, rt), :] + af_scr[pl.ds(off, rt), :] * cf[s]
            h = h + (bb_scr[pl.ds(off, rt), :] + ab_scr[pl.ds(off, rt), :] * cbk[s])
            o_ref[pl.ds(rows, rt), :] = h * jax.nn.gelu(gate_ref[pl.ds(rows, rt), :], approximate=True)
            return 0

        lax.fori_loop(0, tiles_per_seg, fix, 0)


def _rglru(p, w_gates, b_gates, lam, *, batch, seq, gate_col0, rt=512):
    ngrp = w_gates.shape[0]
    seg_len = seq // RG_SEGS
    rt = min(rt, seg_len)
    assert seq % RG_SEGS == 0 and seg_len % rt == 0
    scr = pltpu.VMEM((RG_SEGS * (seg_len + RG_PAD), LANES), F32)
    return pl.pallas_call(
        functools.partial(_rglru_kernel, seq=seq, rt=rt),
        grid=(batch, ngrp),
        in_specs=[
            pl.BlockSpec((seq, LANES), lambda b, c: (b, c)),
            pl.BlockSpec((seq, LANES), lambda b, c: (b, gate_col0 + c)),
            pl.BlockSpec((None, LANES, 4 * LANES), lambda b, c: (c, 0, 0)),
            pl.BlockSpec((None, 1, 4 * LANES), lambda b, c: (c, 0, 0)),
            pl.BlockSpec((2, LANES), lambda b, c: (0, c)),
        ],
        out_specs=pl.BlockSpec((seq, LANES), lambda b, c: (b, c)),
        out_shape=jax.ShapeDtypeStruct((batch * seq, ngrp * LANES), F32),
        scratch_shapes=[scr, scr, scr, scr],
        compiler_params=_cparams(("parallel", "parallel")),
        name="rglru",
    )(p, p, w_gates, b_gates, lam)


def _rg_gate_weights(wa, ba, wx, bx):
    ndir, nblk, blk, _ = wa.shape
    ngrp = nblk * blk // LANES
    per = LANES // blk

    def bd(w):
        w = w.reshape(ndir, ngrp, per, blk, blk)
        eye = jnp.eye(per, dtype=w.dtype)
        return jnp.einsum("dgpij,pq->dgpiqj", w, eye).reshape(ndir, ngrp, LANES, LANES)

    a, x = bd(wa), bd(wx)
    w = jnp.concatenate([a[0], x[0], a[1], x[1]], axis=-1)
    ba = ba.reshape(ndir, ngrp, 1, LANES)
    bx = bx.reshape(ndir, ngrp, 1, LANES)
    b = jnp.concatenate([ba[0], bx[0], ba[1], bx[1]], axis=-1)
    return (0.5 * w).astype(BF16), (0.5 * b).astype(F32)


def _gate_kernel(ba_ref, alog_ref, dtb_ref, col_ref, row_ref, *, tm):
    nh = GDN_HEADS
    ri = lax.broadcasted_iota(jnp.int32, (LANES, LANES), 0)
    ci = lax.broadcasted_iota(jnp.int32, (LANES, LANES), 1)
    same = (ri // CHUNK) == (ci // CHUNK)
    lower = jnp.where(same & (ri >= ci), 1.0, 0.0).astype(F32)
    upper = jnp.where(same & (ri <= ci), 1.0, 0.0).astype(F32)
    lane = ci
    neg_a = -jnp.exp(alog_ref[...])
    dtb = dtb_ref[...]
    for k in range(tm // LANES):
        sl = slice(k * LANES, (k + 1) * LANES)
        raw = ba_ref[sl, :]
        beta = _sigmoid(raw)
        gk = neg_a * _softplus(raw + dtb)
        gk = jnp.where((lane >= 2 * nh) & (lane < 4 * nh), gk, 0.0)
        cf = jnp.dot(lower, gk, preferred_element_type=F32, precision=lax.Precision.HIGHEST)
        cb = jnp.dot(upper, gk, preferred_element_type=F32, precision=lax.Precision.HIGHEST)
        gc = jnp.where(lane < 3 * nh, cf, cb)
        col = jnp.where(lane < 2 * nh, beta, gc)
        col_ref[sl, :] = col
        colt = col.T
        for q in range(LANES // CHUNK):
            rows = [jnp.concatenate([colt[(2 + dr) * nh + h:(2 + dr) * nh + h + 1, q * CHUNK:(q + 1) * CHUNK]
                                     for h in range(nh)], axis=1) for dr in range(2)]
            rows.append(jnp.zeros((SUBLANES - 2, nh * CHUNK), F32))
            row_ref[k * (LANES // CHUNK) + q] = jnp.concatenate(rows, axis=0)


def _gate_prep(ba, a_log, dt_bias, *, tm=512):
    m = ba.shape[0]
    nh = GDN_HEADS
    pad = lambda v: jnp.zeros((1, LANES), F32).at[0, 2 * nh:4 * nh].set(v.reshape(-1).astype(F32))
    return pl.pallas_call(
        functools.partial(_gate_kernel, tm=tm),
        grid=(m // tm,),
        in_specs=[
            pl.BlockSpec((tm, LANES), lambda i: (i, 0)),
            pl.BlockSpec((1, LANES), lambda i: (0, 0)),
            pl.BlockSpec((1, LANES), lambda i: (0, 0)),
        ],
        out_specs=[
            pl.BlockSpec((tm, LANES), lambda i: (i, 0)),
            pl.BlockSpec((tm // CHUNK, SUBLANES, nh * CHUNK), lambda i: (i, 0, 0)),
        ],
        out_shape=[jax.ShapeDtypeStruct((m, LANES), F32),
                   jax.ShapeDtypeStruct((m // CHUNK, SUBLANES, nh * CHUNK), F32)],
        compiler_params=_cparams(("parallel",)),
        name="gate_prep",
    )(ba, pad(a_log), pad(dt_bias))


def _bmm(a, b):
    return lax.dot_general(a.astype(BF16), b.astype(BF16), (((2,), (1,)), ((0,), (0,))),
                           preferred_element_type=F32)


def _bmm_nt(a, b):
    return lax.dot_general(a.astype(BF16), b.astype(BF16), (((2,), (2,)), ((0,), (0,))),
                           preferred_element_type=F32)


def _bmm_tn(a, b):
    return lax.dot_general(a.astype(BF16), b.astype(BF16), (((1,), (1,)), ((0,), (0,))),
                           preferred_element_type=F32)


def _unit_tri_inverse_levels(lm, ri, ci, bdmask, nb):
    def blockdiag(a):
        ab = a.astype(BF16)
        return jnp.where(bdmask, jnp.concatenate([ab] * nb, axis=1), jnp.zeros((), BF16))

    eye = jnp.where(ri == ci, 1.0, 0.0).astype(F32)
    x = eye - jnp.where((ri // 2) == (ci // 2), lm, 0.0)
    s = 2
    while s < CHUNK:
        cm = jnp.where(((ri // (2 * s)) == (ci // (2 * s))) & ((ri // s) != (ci // s)), lm, 0.0)
        x = x - _bmm(_bmm(x, blockdiag(cm)), blockdiag(x))
        yield x
        s *= 2


def _gdn_chunk_local(q_ref, k_ref, v_ref, col_ref, row_ref, dst, *, reverse, g, d):
    u_scr, wq_scr, kd_scr, at_scr, cd_scr = dst
    nh, c, hd = GDN_HEADS, CHUNK, HEAD_DIM
    pw = nh * c
    fw = nh * hd
    ri = lax.broadcasted_iota(jnp.int32, (1, c, pw), 1)
    ci = lax.broadcasted_iota(jnp.int32, (1, c, pw), 2) % c
    if reverse:
        incl, strict, last = ri <= ci, ri < ci, 0
    else:
        incl, strict, last = ri >= ci, ri > ci, c - 1
    bdmask = (lax.broadcasted_iota(jnp.int32, (1, pw, pw), 1) // c
              == lax.broadcasted_iota(jnp.int32, (1, pw, pw), 2) // c)
    kmask = (lax.broadcasted_iota(jnp.int32, (1, pw, fw), 1) // c
             == lax.broadcasted_iota(jnp.int32, (1, pw, fw), 2) // hd)
    low_half = lax.broadcasted_iota(jnp.int32, (g * c, hd), 1) < c

    col = col_ref[...]
    bcast = lambda j: jnp.broadcast_to(col[:, j:j + 1], (g * c, hd))
    beta = jnp.concatenate([bcast(d * nh + h) for h in range(nh)], axis=1).reshape(g, c, fw)
    gcs = [bcast((2 + d) * nh + h) for h in range(nh)]
    gc = jnp.concatenate(gcs, axis=1).reshape(g, c, fw)
    gc_col = jnp.concatenate([jnp.where(low_half, gcs[h], gcs[h + 1]) for h in range(0, nh, 2)],
                             axis=1).reshape(g, c, pw)
    gc_row = row_ref[:, d:d + 1, :]
    g_last = gc[:, last:last + 1, :]
    eg = jnp.exp(gc)
    decay = jnp.where(incl, jnp.exp(gc_col - gc_row), 0.0)
    q = q_ref[...].reshape(g, c, fw)
    k = k_ref[...].reshape(g, c, fw)
    v = v_ref[...].reshape(g, c, fw)
    k_beta = k * beta
    v_beta = v * beta
    kb16 = k.astype(BF16)
    k_bd = jnp.where(kmask, jnp.concatenate([kb16] * nh, axis=1), jnp.zeros((), BF16))
    kq = _bmm_nt(jnp.concatenate([k_beta, q], axis=1), k_bd)
    lm = jnp.where(strict, kq[:, :c] * decay, 0.0)
    attn = kq[:, c:] * decay
    yield
    for t in _unit_tri_inverse_levels(lm, ri, ci, bdmask, nh):
        yield
    kbe = k_beta * eg
    qd = q * eg
    kd = k * jnp.exp(g_last - gc)
    cd = jnp.exp(g_last)
    for h in range(nh):
        hs = slice(h * hd, (h + 1) * hd)
        ps = slice(h * c, (h + 1) * c)
        uw = _bmm(t[:, :, ps], jnp.concatenate([v_beta[:, :, hs], kbe[:, :, hs]], axis=-1))
        u_scr[h] = uw[:, :, :hd]
        wq_scr[h] = jnp.concatenate([uw[:, :, hd:], qd[:, :, hs]], axis=1).astype(BF16)
        at_scr[h] = attn[:, :, ps].astype(BF16)
        kd_scr[h] = kd[:, :, hs].astype(BF16)
        cd_scr[h] = cd[:, :, hs]
        if h % 2 == 1:
            yield


def _gdn_recurrence(src, s_scr, o_ref, *, reverse, g):
    u_scr, wq_scr, kd_scr, at_scr, cd_scr = src
    nh, c, hd = GDN_HEADS, CHUNK, HEAD_DIM
    for step in range(g):
        n = g - 1 - step if reverse else step
        state = s_scr[...]
        sb = state.astype(BF16)
        ws = _bmm(wq_scr[:, n], sb)
        v_new = u_scr[:, n] - ws[:, :c]
        vb = v_new.astype(BF16)
        o = ws[:, c:] + _bmm(at_scr[:, n], vb)
        s_scr[...] = state * cd_scr[:, n] + _bmm_tn(kd_scr[:, n], vb)
        for h in range(nh):
            o_ref[n * c:(n + 1) * c, h * hd:(h + 1) * hd] = o[h]
        yield


def _gdn_kernel(q_ref, k_ref, v_ref, col_ref, row_ref, o_ref, s_scr, *sets, reverse, n_chunks, d):
    set0, set1 = sets[:len(sets) // 2], sets[len(sets) // 2:]
    i = pl.program_id(1)

    @pl.when(i == 0)
    def _():
        s_scr[...] = jnp.zeros_like(s_scr)
        for r in set1:
            r[...] = jnp.zeros_like(r)

    def step(dst, src):
        local = _gdn_chunk_local(q_ref, k_ref, v_ref, col_ref, row_ref, dst, reverse=reverse, g=n_chunks, d=d)
        recur = _gdn_recurrence(src, s_scr, o_ref, reverse=reverse, g=n_chunks)
        for _ in itertools.zip_longest(local, recur):
            pass

    @pl.when(i % 2 == 0)
    def _():
        step(set0, set1)

    @pl.when(i % 2 == 1)
    def _():
        step(set1, set0)


def _gdn(p, col, row, *, q_col, batch, seq, d, tt=512):
    nh = GDN_HEADS
    width = nh * HEAD_DIM
    nt = seq // tt
    n_chunks = tt // CHUNK
    reverse = d == 1
    order = (lambda t: nt - 1 - t) if reverse else (lambda t: t)
    tile_in = lambda b, i: b * nt + order(jnp.minimum(i, nt - 1))
    tile_out = lambda b, i: b * nt + order(jnp.maximum(i - 1, 0))
    scratch_set = [
        pltpu.VMEM((nh, n_chunks, CHUNK, HEAD_DIM), F32),
        pltpu.VMEM((nh, n_chunks, 2 * CHUNK, HEAD_DIM), BF16),
        pltpu.VMEM((nh, n_chunks, CHUNK, HEAD_DIM), BF16),
        pltpu.VMEM((nh, n_chunks, CHUNK, CHUNK), BF16),
        pltpu.VMEM((nh, n_chunks, 1, HEAD_DIM), F32),
    ]
    return pl.pallas_call(
        functools.partial(_gdn_kernel, reverse=reverse, n_chunks=n_chunks, d=d),
        grid=(batch, nt + 1),
        in_specs=[
            pl.BlockSpec((tt, width), lambda b, i: (tile_in(b, i), q_col)),
            pl.BlockSpec((tt, width), lambda b, i: (tile_in(b, i), q_col + 1)),
            pl.BlockSpec((tt, width), lambda b, i: (tile_in(b, i), q_col + 2)),
            pl.BlockSpec((tt, LANES), lambda b, i: (tile_in(b, i), 0)),
            pl.BlockSpec((n_chunks, SUBLANES, nh * CHUNK), lambda b, i: (tile_in(b, i), 0, 0)),
        ],
        out_specs=pl.BlockSpec((tt, width), lambda b, i: (tile_out(b, i), 0)),
        out_shape=jax.ShapeDtypeStruct((batch * seq, width), F32),
        scratch_shapes=[pltpu.VMEM((nh, HEAD_DIM, HEAD_DIM), F32)] + scratch_set + scratch_set,
        compiler_params=_cparams(("parallel", "arbitrary")),
        name="gdn_bwd" if reverse else "gdn_fwd",
    )(p, p, p, col, row)


def _mixer(x1, batch, seq, mix_norm, w_in, w_out, rg_conv_w, rg_conv_b, rg_gate_a_w, rg_gate_a_b,
           rg_gate_x_w, rg_gate_x_b, rg_lambda, gdn_conv_w, gdn_a_log, gdn_dt_bias, gdn_norm):
    rg_w = rg_conv_w.shape[1]
    qkv_w = gdn_conv_w.shape[1]
    gdn_vw = GDN_HEADS * HEAD_DIM
    n_main = 2 * rg_w + qkv_w + gdn_vw
    n_gate = w_in.shape[1] - n_main
    w_main = w_in[:, :n_main].astype(BF16)
    w_ba = jnp.zeros((w_in.shape[0], LANES), F32).at[:, :n_gate].set(w_in[:, n_main:]).astype(BF16)
    p, ba = _in_proj(x1, mix_norm, w_main, w_ba, rg_conv_w, rg_conv_b, gdn_conv_w, seq=seq)

    w_gates, b_gates = _rg_gate_weights(rg_gate_a_w, rg_gate_a_b, rg_gate_x_w, rg_gate_x_b)
    y_rg = _rglru(p, w_gates, b_gates, rg_lambda.astype(F32), batch=batch, seq=seq, gate_col0=rg_w // LANES)

    col, row = _gate_prep(ba, gdn_a_log, gdn_dt_bias)
    q_col = 2 * rg_w // gdn_vw
    o_f = _gdn(p, col, row, q_col=q_col, batch=batch, seq=seq, d=0)
    o_b = _gdn(p, col, row, q_col=q_col, batch=batch, seq=seq, d=1)
    return y_rg, o_f, o_b, p, (2 * rg_w + qkv_w) // gdn_vw, gdn_norm, w_out.astype(BF16)


def kernel(x, ffn1_norm, ffn1_w_gate, ffn1_w_up, ffn1_w_down, mix_norm, w_in, w_out, rg_conv_w, rg_conv_b, rg_gate_a_w, rg_gate_a_b, rg_gate_x_w, rg_gate_x_b, rg_lambda, gdn_conv_w, gdn_a_log, gdn_dt_bias, gdn_norm, ffn2_norm, ffn2_w_gate, ffn2_w_up, ffn2_w_down, final_norm):
    batch, seq, d_model = x.shape
    depth = ffn1_norm.shape[0]
    h = x.reshape(batch * seq, d_model)
    for l in range(depth):
        last = l == depth - 1
        h = _ffn(h, ffn1_norm[l], ffn1_w_gate[l].astype(BF16), ffn1_w_up[l].astype(BF16),
                 ffn1_w_down[l].astype(BF16), final_norm, final_norm=False)
        mixer = _mixer(h, batch, seq, mix_norm[l], w_in[l], w_out[l], rg_conv_w[l], rg_conv_b[l],
                   rg_gate_a_w[l], rg_gate_a_b[l], rg_gate_x_w[l], rg_gate_x_b[l], rg_lambda[l],
                   gdn_conv_w[l], gdn_a_log[l], gdn_dt_bias[l], gdn_norm[l])
        h = _ffn(h, ffn2_norm[l], ffn2_w_gate[l].astype(BF16), ffn2_w_up[l].astype(BF16),
                 ffn2_w_down[l].astype(BF16), final_norm, final_norm=last, mixer=mixer)
    return h.reshape(batch, seq, d_model)
```

```python
import functools
import itertools

import jax
import jax.numpy as jnp
from jax import lax
from jax.experimental import pallas as pl
from jax.experimental.pallas import tpu as pltpu

F32 = jnp.float32
BF16 = jnp.bfloat16

EPS = 1e-6
LOG2_E = 1.4426950408889634
RG_C = 8.0
RG_BLOCKS = 8
GDN_HEADS = 4
HEAD_DIM = 128
CHUNK = 64
LANES = 128
SUBLANES = 8
VMEM_LIMIT = 56 * 1024 * 1024


def _cparams(sem):
    return pltpu.CompilerParams(dimension_semantics=sem, vmem_limit_bytes=VMEM_LIMIT)


def _rms(x):
    return x * lax.rsqrt(jnp.mean(x * x, axis=-1, keepdims=True) + EPS)


def _sigmoid(x):
    return 1.0 / (1.0 + jnp.exp(-x))


def _softplus(x):
    return jnp.maximum(x, 0.0) + jnp.log(1.0 + jnp.exp(-jnp.abs(x)))


def _gated_mixer_rows(yrg_ref, of_ref, ob_ref, z_ref, gn_ref, wo_ref, rows):
    o = of_ref[rows, :] + ob_ref[rows, :]
    z = z_ref[rows, :]
    gn = gn_ref[...]
    parts = [yrg_ref[rows, :].astype(BF16)]
    for h in range(GDN_HEADS):
        hs = slice(h * HEAD_DIM, (h + 1) * HEAD_DIM)
        zh = z[:, hs]
        parts.append((_rms(o[:, hs]) * gn * (zh * _sigmoid(zh))).astype(BF16))
    return jnp.dot(jnp.concatenate(parts, axis=-1), wo_ref[...], preferred_element_type=F32)


def _ffn_kernel(*refs, final_norm, n_sub, with_mixer):
    if with_mixer:
        x_ref, mixer_refs, (g_ref, wg_ref, wu_ref, wd_ref, fg_ref, o_ref) = refs[0], refs[1:7], refs[7:]
    else:
        x_ref, g_ref, wg_ref, wu_ref, wd_ref, fg_ref, o_ref = refs
    sub = x_ref.shape[0] // n_sub
    g = g_ref[...]
    for r in range(n_sub):
        rows = slice(r * sub, (r + 1) * sub)
        x = x_ref[rows, :]
        if with_mixer:
            x = x + _gated_mixer_rows(*mixer_refs, rows)
        h = (_rms(x) * g).astype(BF16)
        gate = jnp.dot(h, wg_ref[...], preferred_element_type=F32)
        up = jnp.dot(h, wu_ref[...], preferred_element_type=F32)
        act = (gate * _sigmoid(gate) * up).astype(BF16)
        y = x + 0.5 * jnp.dot(act, wd_ref[...], preferred_element_type=F32)
        if final_norm:
            y = _rms(y) * fg_ref[...]
        o_ref[rows, :] = y


def _ffn(x, g, wg, wu, wd, fg, *, final_norm, mixer=None, tm=512, n_sub=2):
    m, d = x.shape
    f = wg.shape[1]
    assert m % tm == 0 and tm % n_sub == 0
    resident = dict(pipeline_mode=pl.Buffered(1))
    row_spec = lambda w, col=0: pl.BlockSpec((tm, w), lambda i: (i, col))
    const_spec = lambda shape, **kw: pl.BlockSpec(shape, lambda i: (0, 0), **kw)
    operands, in_specs = [x], [row_spec(d)]
    if mixer is not None:
        y_rg, o_f, o_b, p, z_col, gdn_norm, w_out = mixer
        wr, wv = y_rg.shape[1], o_f.shape[1]
        operands += [y_rg, o_f, o_b, p, gdn_norm.reshape(1, -1), w_out]
        in_specs += [row_spec(wr), row_spec(wv), row_spec(wv), row_spec(wv, z_col),
                     const_spec((1, HEAD_DIM)), const_spec((wr + wv, d), **resident)]
    operands += [g.reshape(1, d), wg, wu, wd, fg.reshape(1, d)]
    in_specs += [const_spec((1, d)), const_spec((d, f), **resident), const_spec((d, f), **resident),
                 const_spec((f, d), **resident), const_spec((1, d))]
    return pl.pallas_call(
        functools.partial(_ffn_kernel, final_norm=final_norm, n_sub=n_sub, with_mixer=mixer is not None),
        grid=(m // tm,),
        in_specs=in_specs,
        out_specs=row_spec(d),
        out_shape=jax.ShapeDtypeStruct((m, d), F32),
        compiler_params=_cparams(("parallel",)),
        name="ffn_final" if final_norm else "ffn",
    )(*operands)


def _inproj_kernel(x_ref, xp_ref, xn_ref, g_ref, w_ref, cwr_ref, cbr_ref, cwq_ref, p_ref, ba_ref, stage_scr, *,
                   n_sub, tiles_per_seq, rg_w, qkv_w, n_qk, q_scale):
    tm, n = p_ref.shape
    sub = tm // n_sub
    hal = SUBLANES
    i = pl.program_id(0)
    pos = i % tiles_per_seq
    prev = jnp.where(pos == 0, 0.0, xp_ref[...])
    nxt = jnp.where(pos == tiles_per_seq - 1, 0.0, xn_ref[...])
    g = g_ref[...]
    cwr = cwr_ref[...]
    mid = slice(hal, hal + sub)
    blk = GDN_HEADS * HEAD_DIM

    def stage(slot, pw):
        for j in range(blk // LANES):
            stage_scr[slot * (blk // LANES) + j] = pw[:, j * LANES:(j + 1) * LANES]

    def conv(slab, cw):
        tap = lambda off: stage_scr[slab, hal + off:hal + off + sub, :]
        acc = cw[0:1, :] * tap(-2) + cw[1:2, :] * tap(-1)
        return acc + cw[2:3, :] * tap(0) + cw[3:4, :] * tap(1)

    for r in range(n_sub):
        lo, hi = r * sub - hal, (r + 1) * sub + hal
        parts = ([prev] if lo < 0 else []) + [x_ref[max(lo, 0):min(hi, tm), :]] + ([nxt] if hi > tm else [])
        xw = jnp.concatenate(parts, axis=0)
        h = (_rms(xw) * g).astype(BF16)
        rows = slice(r * sub, (r + 1) * sub)
        proj = lambda c0, c1: jnp.dot(h, w_ref[:, c0:c1], preferred_element_type=F32)
        stage(0, proj(0, rg_w))
        for c in range(0, rg_w, LANES):
            p_ref[rows, c:c + LANES] = conv(c // LANES, cwr[:, c:c + LANES]) + cbr_ref[:, c:c + LANES]
        p_ref[rows, rg_w:2 * rg_w] = proj(rg_w, 2 * rg_w)[mid]
        for c0 in range(0, qkv_w, blk):
            slot = 1 + c0 // blk
            stage(slot, proj(2 * rg_w + c0, 2 * rg_w + c0 + blk))
            for c in range(c0, c0 + blk, LANES):
                y = conv((blk + c) // LANES, cwq_ref[:, c:c + LANES])
                y = y * _sigmoid(y)
                if c < n_qk * LANES:
                    scale = q_scale if c < n_qk * LANES // 2 else 1.0
                    y = y * (lax.rsqrt(jnp.sum(y * y, axis=-1, keepdims=True) + EPS) * scale)
                p_ref[rows, 2 * rg_w + c:2 * rg_w + c + LANES] = y
        tail = proj(2 * rg_w + qkv_w, n + LANES)[mid]
        p_ref[rows, 2 * rg_w + qkv_w:n] = tail[:, :n - 2 * rg_w - qkv_w]
        ba_ref[rows, :] = tail[:, n - 2 * rg_w - qkv_w:]


def _in_proj(x, g, w_all, rg_conv_w, rg_conv_b, qkv_conv_w, *, seq, tm=512, n_sub=2):
    m, d = x.shape
    n = w_all.shape[1] - LANES
    rg_w, qkv_w = rg_conv_w.shape[1], qkv_conv_w.shape[1]
    assert m % tm == 0 and tm % n_sub == 0 and seq % tm == 0
    assert rg_w == GDN_HEADS * HEAD_DIM and qkv_w % rg_w == 0
    hb = tm // SUBLANES
    const_spec = lambda shape, **kw: pl.BlockSpec(shape, lambda i: (0, 0), **kw)
    return pl.pallas_call(
        functools.partial(_inproj_kernel, n_sub=n_sub, tiles_per_seq=seq // tm, rg_w=rg_w, qkv_w=qkv_w,
                          n_qk=2 * GDN_HEADS, q_scale=HEAD_DIM ** -0.5),
        grid=(m // tm,),
        in_specs=[
            pl.BlockSpec((tm, d), lambda i: (i, 0)),
            pl.BlockSpec((SUBLANES, d), lambda i: (jnp.maximum(i * hb - 1, 0), 0)),
            pl.BlockSpec((SUBLANES, d), lambda i: (jnp.minimum((i + 1) * hb, m // SUBLANES - 1), 0)),
            const_spec((1, d)),
            const_spec((d, n + LANES), pipeline_mode=pl.Buffered(1)),
            const_spec((4, rg_w)),
            const_spec((1, rg_w)),
            const_spec((4, qkv_w)),
        ],
        out_specs=[
            pl.BlockSpec((tm, n), lambda i: (i, 0)),
            pl.BlockSpec((tm, LANES), lambda i: (i, 0)),
        ],
        out_shape=[jax.ShapeDtypeStruct((m, n), F32), jax.ShapeDtypeStruct((m, LANES), F32)],
        scratch_shapes=[pltpu.VMEM(((rg_w + qkv_w) // LANES, tm // n_sub + 2 * SUBLANES, LANES), F32)],
        compiler_params=_cparams(("parallel",)),
        name="in_proj",
    )(x, x, x, g.reshape(1, d), w_all, rg_conv_w, rg_conv_b.reshape(1, -1), qkv_conv_w)


RG_SEGS = 2 * SUBLANES
RG_PAD = 8


def _rglru_kernel(x_ref, gate_ref, wg_ref, bg_ref, lam_ref, o_ref,
                  af_scr, bf_scr, ab_scr, bb_scr, *, seq, rt):
    seg_len = seq // RG_SEGS
    pitch = seg_len + RG_PAD
    tiles_per_seg = seg_len // rt
    n_tiles = seq // rt
    a_scr = (af_scr, ab_scr)
    b_scr = (bf_scr, bb_scr)

    bg = bg_ref[...]
    half_log2_a = (-0.5 * RG_C * LOG2_E) * _softplus(-lam_ref[...])

    def gates(i, _):
        r0 = pl.multiple_of(i * rt, rt)
        xc = x_ref[pl.ds(r0, rt), :]
        pre = jnp.dot(xc.astype(BF16), wg_ref[...], preferred_element_type=F32) + bg
        hx = 0.5 * xc
        seg = i // tiles_per_seg
        off = pl.multiple_of(seg * pitch + (i - seg * tiles_per_seg) * rt, SUBLANES)
        for d in range(2):
            tr = jnp.tanh(pre[:, (2 * d) * LANES:(2 * d + 1) * LANES])
            ti = jnp.tanh(pre[:, (2 * d + 1) * LANES:(2 * d + 2) * LANES])
            hl = half_log2_a[d:d + 1, :]
            a = jnp.exp2(hl + hl * tr)
            om = 1.0 - a * a
            b = jnp.where(om > 0.0, om * lax.rsqrt(om), 0.0) * (hx + hx * ti)
            a_scr[d][pl.ds(off, rt), :] = a
            b_scr[d][pl.ds(off, rt), :] = b
        return 0

    lax.fori_loop(0, n_tiles, gates, 0)

    def scan(t, carry):
        hf, pf, hb, pb = carry
        idx = pl.ds(t, RG_SEGS, stride=pitch)
        a = af_scr[idx, :]
        hf = a * hf + bf_scr[idx, :]
        pf = a * pf
        bf_scr[idx, :] = hf
        af_scr[idx, :] = pf
        idx = pl.ds(seg_len - 1 - t, RG_SEGS, stride=pitch)
        a = ab_scr[idx, :]
        hb = a * hb + bb_scr[idx, :]
        pb = a * pb
        bb_scr[idx, :] = hb
        ab_scr[idx, :] = pb
        return hf, pf, hb, pb

    zeros = jnp.zeros((RG_SEGS, LANES), F32)
    ones = jnp.ones((RG_SEGS, LANES), F32)
    hf, pf, hb, pb = lax.fori_loop(0, seg_len, scan, (zeros, ones, zeros, ones), unroll=8)

    cf = [jnp.zeros((1, LANES), F32)]
    for s in range(1, RG_SEGS):
        cf.append(hf[s - 1:s, :] + pf[s - 1:s, :] * cf[s - 1])
    cbk = [None] * RG_SEGS
    cbk[RG_SEGS - 1] = jnp.zeros((1, LANES), F32)
    for s in range(RG_SEGS - 2, -1, -1):
        cbk[s] = hb[s + 1:s + 2, :] + pb[s + 1:s + 2, :] * cbk[s + 1]

    for s in range(RG_SEGS):
        def fix(k, _, s=s):
            off = pl.multiple_of(s * pitch + k * rt, SUBLANES)
            rows = pl.multiple_of(s * seg_len + k * rt, SUBLANES)
            h = bf_scr[pl.ds(off, rt), :] + af_scr[pl.ds(off, rt), :] * cf[s]
            h = h + (bb_scr[pl.ds(off, rt), :] + ab_scr[pl.ds(off, rt), :] * cbk[s])
            o_ref[pl.ds(rows, rt), :] = h * jax.nn.gelu(gate_ref[pl.ds(rows, rt), :], approximate=True)
            return 0

        lax.fori_loop(0, tiles_per_seg, fix, 0)


def _rglru(p, w_gates, b_gates, lam, *, batch, seq, gate_col0, rt=512):
    ngrp = w_gates.shape[0]
    seg_len = seq // RG_SEGS
    rt = min(rt, seg_len)
    assert seq % RG_SEGS == 0 and seg_len % rt == 0
    scr = pltpu.VMEM((RG_SEGS * (seg_len + RG_PAD), LANES), F32)
    return pl.pallas_call(
        functools.partial(_rglru_kernel, seq=seq, rt=rt),
        grid=(batch, ngrp),
        in_specs=[
            pl.BlockSpec((seq, LANES), lambda b, c: (b, c)),
            pl.BlockSpec((seq, LANES), lambda b, c: (b, gate_col0 + c)),
            pl.BlockSpec((None, LANES, 4 * LANES), lambda b, c: (c, 0, 0)),
            pl.BlockSpec((None, 1, 4 * LANES), lambda b, c: (c, 0, 0)),
            pl.BlockSpec((2, LANES), lambda b, c: (0, c)),
        ],
        out_specs=pl.BlockSpec((seq, LANES), lambda b, c: (b, c)),
        out_shape=jax.ShapeDtypeStruct((batch * seq, ngrp * LANES), F32),
        scratch_shapes=[scr, scr, scr, scr],
        compiler_params=_cparams(("parallel", "parallel")),
        name="rglru",
    )(p, p, w_gates, b_gates, lam)


def _rg_gate_weights(wa, ba, wx, bx):
    ndir, nblk, blk, _ = wa.shape
    ngrp = nblk * blk // LANES
    per = LANES // blk

    def bd(w):
        w = w.reshape(ndir, ngrp, per, blk, blk)
        eye = jnp.eye(per, dtype=w.dtype)
        return jnp.einsum("dgpij,pq->dgpiqj", w, eye).reshape(ndir, ngrp, LANES, LANES)

    a, x = bd(wa), bd(wx)
    w = jnp.concatenate([a[0], x[0], a[1], x[1]], axis=-1)
    ba = ba.reshape(ndir, ngrp, 1, LANES)
    bx = bx.reshape(ndir, ngrp, 1, LANES)
    b = jnp.concatenate([ba[0], bx[0], ba[1], bx[1]], axis=-1)
    return (0.5 * w).astype(BF16), (0.5 * b).astype(F32)


def _gate_kernel(ba_ref, alog_ref, dtb_ref, col_ref, row_ref, *, tm):
    nh = GDN_HEADS
    ri = lax.broadcasted_iota(jnp.int32, (LANES, LANES), 0)
    ci = lax.broadcasted_iota(jnp.int32, (LANES, LANES), 1)
    same = (ri // CHUNK) == (ci // CHUNK)
    lower = jnp.where(same & (ri >= ci), 1.0, 0.0).astype(BF16)
    upper = jnp.where(same & (ri <= ci), 1.0, 0.0).astype(BF16)
    masks = jnp.concatenate([lower, upper], axis=0)
    lane = ci
    neg_a = -jnp.exp(alog_ref[...])
    dtb = dtb_ref[...]
    for k in range(tm // LANES):
        sl = slice(k * LANES, (k + 1) * LANES)
        raw = ba_ref[sl, :]
        beta = _sigmoid(raw)
        gk = neg_a * _softplus(raw + dtb)
        gk = jnp.where((lane >= 2 * nh) & (lane < 4 * nh), gk, 0.0)
        csum = jnp.zeros((2 * LANES, LANES), F32)
        rest = gk
        for _ in range(3):
            piece = rest.astype(BF16)
            csum = csum + jnp.dot(masks, piece, preferred_element_type=F32)
            rest = rest - piece.astype(F32)
        gc = jnp.where(lane < 3 * nh, csum[:LANES], csum[LANES:])
        col = jnp.where(lane < 2 * nh, beta, gc)
        col_ref[sl, :] = col
        colt = col.T
        for q in range(LANES // CHUNK):
            rows = [jnp.concatenate([colt[(2 + dr) * nh + h:(2 + dr) * nh + h + 1, q * CHUNK:(q + 1) * CHUNK]
                                     for h in range(nh)], axis=1) for dr in range(2)]
            rows.append(jnp.zeros((SUBLANES - 2, nh * CHUNK), F32))
            row_ref[k * (LANES // CHUNK) + q] = jnp.concatenate(rows, axis=0)


def _gate_prep(ba, a_log, dt_bias, *, tm=512):
    m = ba.shape[0]
    nh = GDN_HEADS
    pad = lambda v: jnp.zeros((1, LANES), F32).at[0, 2 * nh:4 * nh].set(v.reshape(-1).astype(F32))
    return pl.pallas_call(
        functools.partial(_gate_kernel, tm=tm),
        grid=(m // tm,),
        in_specs=[
            pl.BlockSpec((tm, LANES), lambda i: (i, 0)),
            pl.BlockSpec((1, LANES), lambda i: (0, 0)),
            pl.BlockSpec((1, LANES), lambda i: (0, 0)),
        ],
        out_specs=[
            pl.BlockSpec((tm, LANES), lambda i: (i, 0)),
            pl.BlockSpec((tm // CHUNK, SUBLANES, nh * CHUNK), lambda i: (i, 0, 0)),
        ],
        out_shape=[jax.ShapeDtypeStruct((m, LANES), F32),
                   jax.ShapeDtypeStruct((m // CHUNK, SUBLANES, nh * CHUNK), F32)],
        compiler_params=_cparams(("parallel",)),
        name="gate_prep",
    )(ba, pad(a_log), pad(dt_bias))


def _bmm(a, b):
    return lax.dot_general(a.astype(BF16), b.astype(BF16), (((2,), (1,)), ((0,), (0,))),
                           preferred_element_type=F32)


def _bmm_nt(a, b):
    return lax.dot_general(a.astype(BF16), b.astype(BF16), (((2,), (2,)), ((0,), (0,))),
                           preferred_element_type=F32)


def _bmm_tn(a, b):
    return lax.dot_general(a.astype(BF16), b.astype(BF16), (((1,), (1,)), ((0,), (0,))),
                           preferred_element_type=F32)


def _unit_tri_inverse_levels(lm, ri, ci, bdmask, nb):
    def blockdiag(a):
        ab = a.astype(BF16)
        return jnp.where(bdmask, jnp.concatenate([ab] * nb, axis=1), jnp.zeros((), BF16))

    eye = jnp.where(ri == ci, 1.0, 0.0).astype(F32)
    x = eye - jnp.where((ri // 2) == (ci // 2), lm, 0.0)
    s = 2
    while s < CHUNK:
        cm = jnp.where(((ri // (2 * s)) == (ci // (2 * s))) & ((ri // s) != (ci // s)), lm, 0.0)
        x = x - _bmm(_bmm(x, blockdiag(cm)), blockdiag(x))
        yield x
        s *= 2


def _gdn_chunk_local(q_ref, k_ref, v_ref, col_ref, row_ref, dst, *, reverse, g, d):
    u_scr, wq_scr, kd_scr, at_scr, cd_scr = dst
    nh, c, hd = GDN_HEADS, CHUNK, HEAD_DIM
    pw = nh * c
    fw = nh * hd
    ri = lax.broadcasted_iota(jnp.int32, (1, c, pw), 1)
    ci = lax.broadcasted_iota(jnp.int32, (1, c, pw), 2) % c
    if reverse:
        incl, strict, last = ri <= ci, ri < ci, 0
    else:
        incl, strict, last = ri >= ci, ri > ci, c - 1
    bdmask = (lax.broadcasted_iota(jnp.int32, (1, pw, pw), 1) // c
              == lax.broadcasted_iota(jnp.int32, (1, pw, pw), 2) // c)
    kmask = (lax.broadcasted_iota(jnp.int32, (1, pw, fw), 1) // c
             == lax.broadcasted_iota(jnp.int32, (1, pw, fw), 2) // hd)
    low_half = lax.broadcasted_iota(jnp.int32, (g * c, hd), 1) < c

    col = col_ref[...]
    bcast = lambda j: jnp.broadcast_to(col[:, j:j + 1], (g * c, hd))
    beta = jnp.concatenate([bcast(d * nh + h) for h in range(nh)], axis=1).reshape(g, c, fw)
    gcs = [bcast((2 + d) * nh + h) for h in range(nh)]
    gc = jnp.concatenate(gcs, axis=1).reshape(g, c, fw)
    gc_col = jnp.concatenate([jnp.where(low_half, gcs[h], gcs[h + 1]) for h in range(0, nh, 2)],
                             axis=1).reshape(g, c, pw)
    gc_row = row_ref[:, d:d + 1, :]
    g_last = gc[:, last:last + 1, :]
    eg = jnp.exp(gc)
    decay = jnp.where(incl, jnp.exp(gc_col - gc_row), 0.0)
    q = q_ref[...].reshape(g, c, fw)
    k = k_ref[...].reshape(g, c, fw)
    v = v_ref[...].reshape(g, c, fw)
    k_beta = k * beta
    v_beta = v * beta
    kb16 = k.astype(BF16)
    k_bd = jnp.where(kmask, jnp.concatenate([kb16] * nh, axis=1), jnp.zeros((), BF16))
    kq = _bmm_nt(jnp.concatenate([k_beta, q], axis=1), k_bd)
    lm = jnp.where(strict, kq[:, :c] * decay, 0.0)
    attn = kq[:, c:] * decay
    yield
    for t in _unit_tri_inverse_levels(lm, ri, ci, bdmask, nh):
        yield
    kbe = k_beta * eg
    qd = q * eg
    kd = k * jnp.exp(g_last - gc)
    cd = jnp.exp(g_last)
    for h in range(nh):
        hs = slice(h * hd, (h + 1) * hd)
        ps = slice(h * c, (h + 1) * c)
        uw = _bmm(t[:, :, ps], jnp.concatenate([v_beta[:, :, hs], kbe[:, :, hs]], axis=-1))
        u_scr[h] = uw[:, :, :hd]
        wq_scr[h] = jnp.concatenate([uw[:, :, hd:], qd[:, :, hs]], axis=1).astype(BF16)
        at_scr[h] = attn[:, :, ps].astype(BF16)
        kd_scr[h] = kd[:, :, hs].astype(BF16)
        cd_scr[h] = cd[:, :, hs]
        if h % 2 == 1:
            yield


def _gdn_recurrence(src, s_scr, o_ref, *, reverse, g):
    u_scr, wq_scr, kd_scr, at_scr, cd_scr = src
    nh, c, hd = GDN_HEADS, CHUNK, HEAD_DIM
    for step in range(g):
        n = g - 1 - step if reverse else step
        state = s_scr[...]
        sb = state.astype(BF16)
        ws = _bmm(wq_scr[:, n], sb)
        v_new = u_scr[:, n] - ws[:, :c]
        vb = v_new.astype(BF16)
        o = ws[:, c:] + _bmm(at_scr[:, n], vb)
        s_scr[...] = state * cd_scr[:, n] + _bmm_tn(kd_scr[:, n], vb)
        for h in range(nh):
            o_ref[n * c:(n + 1) * c, h * hd:(h + 1) * hd] = o[h]
        yield


def _gdn_kernel(q_ref, k_ref, v_ref, col_ref, row_ref, o_ref, s_scr, *sets, reverse, n_chunks, d):
    set0, set1 = sets[:len(sets) // 2], sets[len(sets) // 2:]
    i = pl.program_id(1)

    @pl.when(i == 0)
    def _():
        s_scr[...] = jnp.zeros_like(s_scr)
        for r in set1:
            r[...] = jnp.zeros_like(r)

    def step(dst, src):
        local = _gdn_chunk_local(q_ref, k_ref, v_ref, col_ref, row_ref, dst, reverse=reverse, g=n_chunks, d=d)
        recur = _gdn_recurrence(src, s_scr, o_ref, reverse=reverse, g=n_chunks)
        for _ in itertools.zip_longest(local, recur):
            pass

    @pl.when(i % 2 == 0)
    def _():
        step(set0, set1)

    @pl.when(i % 2 == 1)
    def _():
        step(set1, set0)


def _gdn(p, col, row, *, q_col, batch, seq, d, tt=512):
    nh = GDN_HEADS
    width = nh * HEAD_DIM
    nt = seq // tt
    n_chunks = tt // CHUNK
    reverse = d == 1
    order = (lambda t: nt - 1 - t) if reverse else (lambda t: t)
    tile_in = lambda b, i: b * nt + order(jnp.minimum(i, nt - 1))
    tile_out = lambda b, i: b * nt + order(jnp.maximum(i - 1, 0))
    scratch_set = [
        pltpu.VMEM((nh, n_chunks, CHUNK, HEAD_DIM), F32),
        pltpu.VMEM((nh, n_chunks, 2 * CHUNK, HEAD_DIM), BF16),
        pltpu.VMEM((nh, n_chunks, CHUNK, HEAD_DIM), BF16),
        pltpu.VMEM((nh, n_chunks, CHUNK, CHUNK), BF16),
        pltpu.VMEM((nh, n_chunks, 1, HEAD_DIM), F32),
    ]
    return pl.pallas_call(
        functools.partial(_gdn_kernel, reverse=reverse, n_chunks=n_chunks, d=d),
        grid=(batch, nt + 1),
        in_specs=[
            pl.BlockSpec((tt, width), lambda b, i: (tile_in(b, i), q_col)),
            pl.BlockSpec((tt, width), lambda b, i: (tile_in(b, i), q_col + 1)),
            pl.BlockSpec((tt, width), lambda b, i: (tile_in(b, i), q_col + 2)),
            pl.BlockSpec((tt, LANES), lambda b, i: (tile_in(b, i), 0)),
            pl.BlockSpec((n_chunks, SUBLANES, nh * CHUNK), lambda b, i: (tile_in(b, i), 0, 0)),
        ],
        out_specs=pl.BlockSpec((tt, width), lambda b, i: (tile_out(b, i), 0)),
        out_shape=jax.ShapeDtypeStruct((batch * seq, width), F32),
        scratch_shapes=[pltpu.VMEM((nh, HEAD_DIM, HEAD_DIM), F32)] + scratch_set + scratch_set,
        compiler_params=_cparams(("parallel", "arbitrary")),
        name="gdn_bwd" if reverse else "gdn_fwd",
    )(p, p, p, col, row)


def _mixer(x1, batch, seq, mix_norm, w_in, w_out, rg_conv_w, rg_conv_b, rg_gate_a_w, rg_gate_a_b,
           rg_gate_x_w, rg_gate_x_b, rg_lambda, gdn_conv_w, gdn_a_log, gdn_dt_bias, gdn_norm):
    rg_w = rg_conv_w.shape[1]
    qkv_w = gdn_conv_w.shape[1]
    gdn_vw = GDN_HEADS * HEAD_DIM
    n_main = 2 * rg_w + qkv_w + gdn_vw
    w_all = jnp.pad(w_in, ((0, 0), (0, n_main + LANES - w_in.shape[1]))).astype(BF16)
    p, ba = _in_proj(x1, mix_norm, w_all, rg_conv_w, rg_conv_b, gdn_conv_w, seq=seq)

    w_gates, b_gates = _rg_gate_weights(rg_gate_a_w, rg_gate_a_b, rg_gate_x_w, rg_gate_x_b)
    y_rg = _rglru(p, w_gates, b_gates, rg_lambda.astype(F32), batch=batch, seq=seq, gate_col0=rg_w // LANES)

    col, row = _gate_prep(ba, gdn_a_log, gdn_dt_bias)
    q_col = 2 * rg_w // gdn_vw
    o_f = _gdn(p, col, row, q_col=q_col, batch=batch, seq=seq, d=0)
    o_b = _gdn(p, col, row, q_col=q_col, batch=batch, seq=seq, d=1)
    return y_rg, o_f, o_b, p, (2 * rg_w + qkv_w) // gdn_vw, gdn_norm, w_out.astype(BF16)


def kernel(x, ffn1_norm, ffn1_w_gate, ffn1_w_up, ffn1_w_down, mix_norm, w_in, w_out, rg_conv_w, rg_conv_b, rg_gate_a_w, rg_gate_a_b, rg_gate_x_w, rg_gate_x_b, rg_lambda, gdn_conv_w, gdn_a_log, gdn_dt_bias, gdn_norm, ffn2_norm, ffn2_w_gate, ffn2_w_up, ffn2_w_down, final_norm):
    batch, seq, d_model = x.shape
    depth = ffn1_norm.shape[0]
    h = x.reshape(batch * seq, d_model)
    for l in range(depth):
        last = l == depth - 1
        h = _ffn(h, ffn1_norm[l], ffn1_w_gate[l].astype(BF16), ffn1_w_up[l].astype(BF16),
                 ffn1_w_down[l].astype(BF16), final_norm, final_norm=False)
        mixer = _mixer(h, batch, seq, mix_norm[l], w_in[l], w_out[l], rg_conv_w[l], rg_conv_b[l],
                   rg_gate_a_w[l], rg_gate_a_b[l], rg_gate_x_w[l], rg_gate_x_b[l], rg_lambda[l],
                   gdn_conv_w[l], gdn_a_log[l], gdn_dt_bias[l], gdn_norm[l])
        h = _ffn(h, ffn2_norm[l], ffn2_w_gate[l].astype(BF16), ffn2_w_up[l].astype(BF16),
                 ffn2_w_down[l].astype(BF16), final_norm, final_norm=last, mixer=mixer)
    return h.reshape(batch, seq, d_model)
```

```python
import functools
import itertools

import jax
import jax.numpy as jnp
from jax import lax
from jax.experimental import pallas as pl
from jax.experimental.pallas import tpu as pltpu

F32 = jnp.float32
BF16 = jnp.bfloat16

EPS = 1e-6
LOG2_E = 1.4426950408889634
RG_C = 8.0
RG_BLOCKS = 8
GDN_HEADS = 4
HEAD_DIM = 128
CHUNK = 64
LANES = 128
SUBLANES = 8
VMEM_LIMIT = 56 * 1024 * 1024


def _cparams(sem):
    return pltpu.CompilerParams(dimension_semantics=sem, vmem_limit_bytes=VMEM_LIMIT)


def _rms(x):
    return x * lax.rsqrt(jnp.mean(x * x, axis=-1, keepdims=True) + EPS)


def _sigmoid(x):
    return 1.0 / (1.0 + jnp.exp(-x))


def _softplus(x):
    return jnp.maximum(x, 0.0) + jnp.log(1.0 + jnp.exp(-jnp.abs(x)))


def _gated_mixer_rows(yrg_ref, of_ref, ob_ref, z_ref, gn_ref, wo_ref, rows):
    o = of_ref[rows, :] + ob_ref[rows, :]
    z = z_ref[rows, :]
    gn = gn_ref[...]
    parts = [yrg_ref[rows, :].astype(BF16)]
    for h in range(GDN_HEADS):
        hs = slice(h * HEAD_DIM, (h + 1) * HEAD_DIM)
        zh = z[:, hs]
        parts.append((_rms(o[:, hs]) * gn * (zh * _sigmoid(zh))).astype(BF16))
    return jnp.dot(jnp.concatenate(parts, axis=-1), wo_ref[...], preferred_element_type=F32)


def _ffn_kernel(*refs, final_norm, n_sub, with_mixer):
    if with_mixer:
        x_ref, mixer_refs, (g_ref, wg_ref, wu_ref, wd_ref, fg_ref, o_ref) = refs[0], refs[1:7], refs[7:]
    else:
        x_ref, g_ref, wg_ref, wu_ref, wd_ref, fg_ref, o_ref = refs
    sub = x_ref.shape[0] // n_sub
    g = g_ref[...]
    for r in range(n_sub):
        rows = slice(r * sub, (r + 1) * sub)
        x = x_ref[rows, :]
        if with_mixer:
            x = x + _gated_mixer_rows(*mixer_refs, rows)
        h = (_rms(x) * g).astype(BF16)
        gate = jnp.dot(h, wg_ref[...], preferred_element_type=F32)
        up = jnp.dot(h, wu_ref[...], preferred_element_type=F32)
        act = (gate * _sigmoid(gate) * up).astype(BF16)
        y = x + 0.5 * jnp.dot(act, wd_ref[...], preferred_element_type=F32)
        if final_norm:
            y = _rms(y) * fg_ref[...]
        o_ref[rows, :] = y


def _ffn(x, g, wg, wu, wd, fg, *, final_norm, mixer=None, tm=512, n_sub=2):
    m, d = x.shape
    f = wg.shape[1]
    assert m % tm == 0 and tm % n_sub == 0
    resident = dict(pipeline_mode=pl.Buffered(1))
    row_spec = lambda w, col=0: pl.BlockSpec((tm, w), lambda i: (i, col))
    const_spec = lambda shape, **kw: pl.BlockSpec(shape, lambda i: (0, 0), **kw)
    operands, in_specs = [x], [row_spec(d)]
    if mixer is not None:
        y_rg, o_f, o_b, p, z_col, gdn_norm, w_out = mixer
        wr, wv = y_rg.shape[1], o_f.shape[1]
        operands += [y_rg, o_f, o_b, p, gdn_norm.reshape(1, -1), w_out]
        in_specs += [row_spec(wr), row_spec(wv), row_spec(wv), row_spec(wv, z_col),
                     const_spec((1, HEAD_DIM)), const_spec((wr + wv, d), **resident)]
    operands += [g.reshape(1, d), wg, wu, wd, fg.reshape(1, d)]
    in_specs += [const_spec((1, d)), const_spec((d, f), **resident), const_spec((d, f), **resident),
                 const_spec((f, d), **resident), const_spec((1, d))]
    return pl.pallas_call(
        functools.partial(_ffn_kernel, final_norm=final_norm, n_sub=n_sub, with_mixer=mixer is not None),
        grid=(m // tm,),
        in_specs=in_specs,
        out_specs=row_spec(d),
        out_shape=jax.ShapeDtypeStruct((m, d), F32),
        compiler_params=_cparams(("parallel",)),
        name="ffn_final" if final_norm else "ffn",
    )(*operands)


def _inproj_kernel(x_ref, xp_ref, xn_ref, g_ref, w_ref, cwr_ref, cbr_ref, cwq_ref, p_ref, ba_ref, stage_scr, *,
                   n_sub, tiles_per_seq, rg_w, qkv_w, n_qk, q_scale):
    tm, n = p_ref.shape
    sub = tm // n_sub
    hal = SUBLANES
    i = pl.program_id(0)
    pos = i % tiles_per_seq
    prev = jnp.where(pos == 0, 0.0, xp_ref[...])
    nxt = jnp.where(pos == tiles_per_seq - 1, 0.0, xn_ref[...])
    g = g_ref[...]
    cwr = cwr_ref[...]
    mid = slice(hal, hal + sub)
    blk = GDN_HEADS * HEAD_DIM

    def stage(slot, pw):
        for j in range(blk // LANES):
            stage_scr[slot * (blk // LANES) + j] = pw[:, j * LANES:(j + 1) * LANES]

    def conv(slab, cw):
        tap = lambda off: stage_scr[slab, hal + off:hal + off + sub, :]
        acc = cw[0:1, :] * tap(-2) + cw[1:2, :] * tap(-1)
        return acc + cw[2:3, :] * tap(0) + cw[3:4, :] * tap(1)

    for r in range(n_sub):
        lo, hi = r * sub - hal, (r + 1) * sub + hal
        parts = ([prev] if lo < 0 else []) + [x_ref[max(lo, 0):min(hi, tm), :]] + ([nxt] if hi > tm else [])
        xw = jnp.concatenate(parts, axis=0)
        h = (_rms(xw) * g).astype(BF16)
        rows = slice(r * sub, (r + 1) * sub)
        proj = lambda c0, c1: jnp.dot(h, w_ref[:, c0:c1], preferred_element_type=F32)
        stage(0, proj(0, rg_w))
        for c in range(0, rg_w, LANES):
            p_ref[rows, c:c + LANES] = conv(c // LANES, cwr[:, c:c + LANES]) + cbr_ref[:, c:c + LANES]
        p_ref[rows, rg_w:2 * rg_w] = proj(rg_w, 2 * rg_w)[mid]
        for c0 in range(0, qkv_w, blk):
            slot = 1 + c0 // blk
            stage(slot, proj(2 * rg_w + c0, 2 * rg_w + c0 + blk))
            for c in range(c0, c0 + blk, LANES):
                y = conv((blk + c) // LANES, cwq_ref[:, c:c + LANES])
                y = y * _sigmoid(y)
                if c < n_qk * LANES:
                    scale = q_scale if c < n_qk * LANES // 2 else 1.0
                    y = y * (lax.rsqrt(jnp.sum(y * y, axis=-1, keepdims=True) + EPS) * scale)
                p_ref[rows, 2 * rg_w + c:2 * rg_w + c + LANES] = y
        tail = proj(2 * rg_w + qkv_w, n + LANES)[mid]
        p_ref[rows, 2 * rg_w + qkv_w:n] = tail[:, :n - 2 * rg_w - qkv_w]
        ba_ref[rows, :] = tail[:, n - 2 * rg_w - qkv_w:]


def _in_proj(x, g, w_all, rg_conv_w, rg_conv_b, qkv_conv_w, *, seq, tm=512, n_sub=2):
    m, d = x.shape
    n = w_all.shape[1] - LANES
    rg_w, qkv_w = rg_conv_w.shape[1], qkv_conv_w.shape[1]
    assert m % tm == 0 and tm % n_sub == 0 and seq % tm == 0
    assert rg_w == GDN_HEADS * HEAD_DIM and qkv_w % rg_w == 0
    hb = tm // SUBLANES
    const_spec = lambda shape, **kw: pl.BlockSpec(shape, lambda i: (0, 0), **kw)
    return pl.pallas_call(
        functools.partial(_inproj_kernel, n_sub=n_sub, tiles_per_seq=seq // tm, rg_w=rg_w, qkv_w=qkv_w,
                          n_qk=2 * GDN_HEADS, q_scale=HEAD_DIM ** -0.5),
        grid=(m // tm,),
        in_specs=[
            pl.BlockSpec((tm, d), lambda i: (i, 0)),
            pl.BlockSpec((SUBLANES, d), lambda i: (jnp.maximum(i * hb - 1, 0), 0)),
            pl.BlockSpec((SUBLANES, d), lambda i: (jnp.minimum((i + 1) * hb, m // SUBLANES - 1), 0)),
            const_spec((1, d)),
            const_spec((d, n + LANES), pipeline_mode=pl.Buffered(1)),
            const_spec((4, rg_w)),
            const_spec((1, rg_w)),
            const_spec((4, qkv_w)),
        ],
        out_specs=[
            pl.BlockSpec((tm, n), lambda i: (i, 0)),
            pl.BlockSpec((tm, LANES), lambda i: (i, 0)),
        ],
        out_shape=[jax.ShapeDtypeStruct((m, n), F32), jax.ShapeDtypeStruct((m, LANES), F32)],
        scratch_shapes=[pltpu.VMEM(((rg_w + qkv_w) // LANES, tm // n_sub + 2 * SUBLANES, LANES), F32)],
        compiler_params=_cparams(("parallel",)),
        name="in_proj",
    )(x, x, x, g.reshape(1, d), w_all, rg_conv_w, rg_conv_b.reshape(1, -1), qkv_conv_w)


RG_SEGS = 2 * SUBLANES
RG_PAD = 4


def _rglru_kernel(x_ref, gate_ref, wg_ref, bg_ref, lam_ref, o_ref,
                  af_scr, bf_scr, ab_scr, bb_scr, *, seq, rt):
    seg_len = seq // RG_SEGS
    pitch = seg_len + RG_PAD
    tiles_per_seg = seg_len // rt
    n_tiles = seq // rt
    a_scr = (af_scr, ab_scr)
    b_scr = (bf_scr, bb_scr)

    bg = bg_ref[...]
    half_log2_a = (-0.5 * RG_C * LOG2_E) * _softplus(-lam_ref[...])

    def gates(i, _):
        r0 = pl.multiple_of(i * rt, rt)
        xc = x_ref[pl.ds(r0, rt), :]
        pre = jnp.dot(xc.astype(BF16), wg_ref[...], preferred_element_type=F32) + bg
        hx = 0.5 * xc
        seg = i // tiles_per_seg
        off = seg * pitch + (i - seg * tiles_per_seg) * rt
        for d in range(2):
            tr = jnp.tanh(pre[:, (2 * d) * LANES:(2 * d + 1) * LANES])
            ti = jnp.tanh(pre[:, (2 * d + 1) * LANES:(2 * d + 2) * LANES])
            hl = half_log2_a[d:d + 1, :]
            a = jnp.exp2(hl + hl * tr)
            om = 1.0 - a * a
            b = jnp.where(om > 0.0, om * lax.rsqrt(om), 0.0) * (hx + hx * ti)
            a_scr[d][pl.ds(off, rt), :] = a
            b_scr[d][pl.ds(off, rt), :] = b
        return 0

    lax.fori_loop(0, n_tiles, gates, 0)

    def scan(t, carry):
        hf, pf, hb, pb = carry
        idx = pl.ds(t, RG_SEGS, stride=pitch)
        a = af_scr[idx, :]
        hf = a * hf + bf_scr[idx, :]
        pf = a * pf
        bf_scr[idx, :] = hf
        af_scr[idx, :] = pf
        idx = pl.ds(seg_len - 1 - t, RG_SEGS, stride=pitch)
        a = ab_scr[idx, :]
        hb = a * hb + bb_scr[idx, :]
        pb = a * pb
        bb_scr[idx, :] = hb
        ab_scr[idx, :] = pb
        return hf, pf, hb, pb

    zeros = jnp.zeros((RG_SEGS, LANES), F32)
    ones = jnp.ones((RG_SEGS, LANES), F32)
    hf, pf, hb, pb = lax.fori_loop(0, seg_len, scan, (zeros, ones, zeros, ones), unroll=8)

    cf = [jnp.zeros((1, LANES), F32)]
    for s in range(1, RG_SEGS):
        cf.append(hf[s - 1:s, :] + pf[s - 1:s, :] * cf[s - 1])
    cbk = [None] * RG_SEGS
    cbk[RG_SEGS - 1] = jnp.zeros((1, LANES), F32)
    for s in range(RG_SEGS - 2, -1, -1):
        cbk[s] = hb[s + 1:s + 2, :] + pb[s + 1:s + 2, :] * cbk[s + 1]

    for s in range(RG_SEGS):
        def fix(k, _, s=s):
            off = s * pitch + k * rt
            rows = pl.multiple_of(s * seg_len + k * rt, SUBLANES)
            h = bf_scr[pl.ds(off, rt), :] + af_scr[pl.ds(off, rt), :] * cf[s]
            h = h + (bb_scr[pl.ds(off, rt), :] + ab_scr[pl.ds(off, rt), :] * cbk[s])
            o_ref[pl.ds(rows, rt), :] = h * jax.nn.gelu(gate_ref[pl.ds(rows, rt), :], approximate=True)
            return 0

        lax.fori_loop(0, tiles_per_seg, fix, 0)


def _rglru(p, w_gates, b_gates, lam, *, batch, seq, gate_col0, rt=512):
    ngrp = w_gates.shape[0]
    seg_len = seq // RG_SEGS
    rt = min(rt, seg_len)
    assert seq % RG_SEGS == 0 and seg_len % rt == 0
    scr = pltpu.VMEM((RG_SEGS * (seg_len + RG_PAD), LANES), F32)
    return pl.pallas_call(
        functools.partial(_rglru_kernel, seq=seq, rt=rt),
        grid=(batch, ngrp),
        in_specs=[
            pl.BlockSpec((seq, LANES), lambda b, c: (b, c)),
            pl.BlockSpec((seq, LANES), lambda b, c: (b, gate_col0 + c)),
            pl.BlockSpec((None, LANES, 4 * LANES), lambda b, c: (c, 0, 0)),
            pl.BlockSpec((None, 1, 4 * LANES), lambda b, c: (c, 0, 0)),
            pl.BlockSpec((2, LANES), lambda b, c: (0, c)),
        ],
        out_specs=pl.BlockSpec((seq, LANES), lambda b, c: (b, c)),
        out_shape=jax.ShapeDtypeStruct((batch * seq, ngrp * LANES), F32),
        scratch_shapes=[scr, scr, scr, scr],
        compiler_params=_cparams(("parallel", "parallel")),
        name="rglru",
    )(p, p, w_gates, b_gates, lam)


def _rg_gate_weights(wa, ba, wx, bx):
    ndir, nblk, blk, _ = wa.shape
    ngrp = nblk * blk // LANES
    per = LANES // blk

    def bd(w):
        w = w.reshape(ndir, ngrp, per, blk, blk)
        eye = jnp.eye(per, dtype=w.dtype)
        return jnp.einsum("dgpij,pq->dgpiqj", w, eye).reshape(ndir, ngrp, LANES, LANES)

    a, x = bd(wa), bd(wx)
    w = jnp.concatenate([a[0], x[0], a[1], x[1]], axis=-1)
    ba = ba.reshape(ndir, ngrp, 1, LANES)
    bx = bx.reshape(ndir, ngrp, 1, LANES)
    b = jnp.concatenate([ba[0], bx[0], ba[1], bx[1]], axis=-1)
    return (0.5 * w).astype(BF16), (0.5 * b).astype(F32)


def _gate_kernel(ba_ref, alog_ref, dtb_ref, col_ref, row_ref, *, tm):
    nh = GDN_HEADS
    ri = lax.broadcasted_iota(jnp.int32, (LANES, LANES), 0)
    ci = lax.broadcasted_iota(jnp.int32, (LANES, LANES), 1)
    same = (ri // CHUNK) == (ci // CHUNK)
    lower = jnp.where(same & (ri >= ci), 1.0, 0.0).astype(BF16)
    upper = jnp.where(same & (ri <= ci), 1.0, 0.0).astype(BF16)
    masks = jnp.concatenate([lower, upper], axis=0)
    lane = ci
    neg_a = -jnp.exp(alog_ref[...])
    dtb = dtb_ref[...]
    for k in range(tm // LANES):
        sl = slice(k * LANES, (k + 1) * LANES)
        raw = ba_ref[sl, :]
        beta = _sigmoid(raw)
        gk = neg_a * _softplus(raw + dtb)
        gk = jnp.where((lane >= 2 * nh) & (lane < 4 * nh), gk, 0.0)
        csum = jnp.zeros((2 * LANES, LANES), F32)
        rest = gk
        for _ in range(3):
            piece = rest.astype(BF16)
            csum = csum + jnp.dot(masks, piece, preferred_element_type=F32)
            rest = rest - piece.astype(F32)
        gc = jnp.where(lane < 3 * nh, csum[:LANES], csum[LANES:])
        col = jnp.where(lane < 2 * nh, beta, gc)
        col_ref[sl, :] = col
        colt = col.T
        for q in range(LANES // CHUNK):
            rows = [jnp.concatenate([colt[(2 + dr) * nh + h:(2 + dr) * nh + h + 1, q * CHUNK:(q + 1) * CHUNK]
                                     for h in range(nh)], axis=1) for dr in range(2)]
            rows.append(jnp.zeros((SUBLANES - 2, nh * CHUNK), F32))
            row_ref[k * (LANES // CHUNK) + q] = jnp.concatenate(rows, axis=0)


def _gate_prep(ba, a_log, dt_bias, *, tm=2048):
    m = ba.shape[0]
    nh = GDN_HEADS
    pad = lambda v: jnp.zeros((1, LANES), F32).at[0, 2 * nh:4 * nh].set(v.reshape(-1).astype(F32))
    return pl.pallas_call(
        functools.partial(_gate_kernel, tm=tm),
        grid=(m // tm,),
        in_specs=[
            pl.BlockSpec((tm, LANES), lambda i: (i, 0)),
            pl.BlockSpec((1, LANES), lambda i: (0, 0)),
            pl.BlockSpec((1, LANES), lambda i: (0, 0)),
        ],
        out_specs=[
            pl.BlockSpec((tm, LANES), lambda i: (i, 0)),
            pl.BlockSpec((tm // CHUNK, SUBLANES, nh * CHUNK), lambda i: (i, 0, 0)),
        ],
        out_shape=[jax.ShapeDtypeStruct((m, LANES), F32),
                   jax.ShapeDtypeStruct((m // CHUNK, SUBLANES, nh * CHUNK), F32)],
        compiler_params=_cparams(("parallel",)),
        name="gate_prep",
    )(ba, pad(a_log), pad(dt_bias))


def _bmm(a, b):
    return lax.dot_general(a.astype(BF16), b.astype(BF16), (((2,), (1,)), ((0,), (0,))),
                           preferred_element_type=F32)


def _bmm_nt(a, b):
    return lax.dot_general(a.astype(BF16), b.astype(BF16), (((2,), (2,)), ((0,), (0,))),
                           preferred_element_type=F32)


def _bmm_tn(a, b):
    return lax.dot_general(a.astype(BF16), b.astype(BF16), (((1,), (1,)), ((0,), (0,))),
                           preferred_element_type=F32)


def _unit_tri_inverse_levels(lm, ri, ci, bdmask, nb):
    def blockdiag(a):
        ab = a.astype(BF16)
        return jnp.where(bdmask, jnp.concatenate([ab] * nb, axis=1), jnp.zeros((), BF16))

    eye = jnp.where(ri == ci, 1.0, 0.0).astype(F32)
    x = eye - jnp.where((ri // 2) == (ci // 2), lm, 0.0)
    s = 2
    while s < CHUNK:
        cm = jnp.where(((ri // (2 * s)) == (ci // (2 * s))) & ((ri // s) != (ci // s)), lm, 0.0)
        x = x - _bmm(_bmm(x, blockdiag(cm)), blockdiag(x))
        yield x
        s *= 2


def _gdn_chunk_local(q_ref, k_ref, v_ref, col_ref, row_ref, dst, *, reverse, g, d):
    u_scr, wq_scr, kd_scr, at_scr, cd_scr = dst
    nh, c, hd = GDN_HEADS, CHUNK, HEAD_DIM
    pw = nh * c
    fw = nh * hd
    ri = lax.broadcasted_iota(jnp.int32, (1, c, pw), 1)
    ci = lax.broadcasted_iota(jnp.int32, (1, c, pw), 2) % c
    if reverse:
        incl, strict, last = ri <= ci, ri < ci, 0
    else:
        incl, strict, last = ri >= ci, ri > ci, c - 1
    bdmask = (lax.broadcasted_iota(jnp.int32, (1, pw, pw), 1) // c
              == lax.broadcasted_iota(jnp.int32, (1, pw, pw), 2) // c)
    kmask = (lax.broadcasted_iota(jnp.int32, (1, pw, fw), 1) // c
             == lax.broadcasted_iota(jnp.int32, (1, pw, fw), 2) // hd)
    low_half = lax.broadcasted_iota(jnp.int32, (g * c, hd), 1) < c

    col = col_ref[...]
    bcast = lambda j: jnp.broadcast_to(col[:, j:j + 1], (g * c, hd))
    beta = jnp.concatenate([bcast(d * nh + h) for h in range(nh)], axis=1).reshape(g, c, fw)
    gcs = [bcast((2 + d) * nh + h) for h in range(nh)]
    gc = jnp.concatenate(gcs, axis=1).reshape(g, c, fw)
    gc_col = jnp.concatenate([jnp.where(low_half, gcs[h], gcs[h + 1]) for h in range(0, nh, 2)],
                             axis=1).reshape(g, c, pw)
    gc_row = row_ref[:, d:d + 1, :]
    g_last = gc[:, last:last + 1, :]
    eg = jnp.exp(gc)
    decay = jnp.where(incl, jnp.exp(gc_col - gc_row), 0.0)
    q = q_ref[...].reshape(g, c, fw)
    k = k_ref[...].reshape(g, c, fw)
    v = v_ref[...].reshape(g, c, fw)
    k_beta = k * beta
    v_beta = v * beta
    kb16 = k.astype(BF16)
    k_bd = jnp.where(kmask, jnp.concatenate([kb16] * nh, axis=1), jnp.zeros((), BF16))
    kq = _bmm_nt(jnp.concatenate([k_beta, q], axis=1), k_bd)
    lm = jnp.where(strict, kq[:, :c] * decay, 0.0)
    attn = kq[:, c:] * decay
    yield
    for t in _unit_tri_inverse_levels(lm, ri, ci, bdmask, nh):
        yield
    kbe = k_beta * eg
    qd = q * eg
    kd = k * jnp.exp(g_last - gc)
    cd = jnp.exp(g_last)
    for h in range(nh):
        hs = slice(h * hd, (h + 1) * hd)
        ps = slice(h * c, (h + 1) * c)
        uw = _bmm(t[:, :, ps], jnp.concatenate([v_beta[:, :, hs], kbe[:, :, hs]], axis=-1))
        u_scr[h] = uw[:, :, :hd]
        wq_scr[h] = jnp.concatenate([uw[:, :, hd:], qd[:, :, hs]], axis=1).astype(BF16)
        at_scr[h] = attn[:, :, ps].astype(BF16)
        kd_scr[h] = kd[:, :, hs].astype(BF16)
        cd_scr[h] = cd[:, :, hs]
        if h % 2 == 1:
            yield


def _gdn_recurrence(src, s_scr, o_ref, *, reverse, g):
    u_scr, wq_scr, kd_scr, at_scr, cd_scr = src
    nh, c, hd = GDN_HEADS, CHUNK, HEAD_DIM
    for step in range(g):
        n = g - 1 - step if reverse else step
        state = s_scr[...]
        sb = state.astype(BF16)
        ws = _bmm(wq_scr[:, n], sb)
        v_new = u_scr[:, n] - ws[:, :c]
        vb = v_new.astype(BF16)
        o = ws[:, c:] + _bmm(at_scr[:, n], vb)
        s_scr[...] = state * cd_scr[:, n] + _bmm_tn(kd_scr[:, n], vb)
        for h in range(nh):
            o_ref[n * c:(n + 1) * c, h * hd:(h + 1) * hd] = o[h]
        yield


def _gdn_kernel(q_ref, k_ref, v_ref, col_ref, row_ref, o_ref, s_scr, *sets, reverse, n_chunks, d):
    set0, set1 = sets[:len(sets) // 2], sets[len(sets) // 2:]
    i = pl.program_id(1)

    @pl.when(i == 0)
    def _():
        s_scr[...] = jnp.zeros_like(s_scr)
        for r in set1:
            r[...] = jnp.zeros_like(r)

    def step(dst, src):
        local = _gdn_chunk_local(q_ref, k_ref, v_ref, col_ref, row_ref, dst, reverse=reverse, g=n_chunks, d=d)
        recur = _gdn_recurrence(src, s_scr, o_ref, reverse=reverse, g=n_chunks)
        for _ in itertools.zip_longest(local, recur):
            pass

    @pl.when(i % 2 == 0)
    def _():
        step(set0, set1)

    @pl.when(i % 2 == 1)
    def _():
        step(set1, set0)


def _gdn(p, col, row, *, q_col, batch, seq, d, tt=512):
    nh = GDN_HEADS
    width = nh * HEAD_DIM
    nt = seq // tt
    n_chunks = tt // CHUNK
    reverse = d == 1
    order = (lambda t: nt - 1 - t) if reverse else (lambda t: t)
    tile_in = lambda b, i: b * nt + order(jnp.minimum(i, nt - 1))
    tile_out = lambda b, i: b * nt + order(jnp.maximum(i - 1, 0))
    scratch_set = [
        pltpu.VMEM((nh, n_chunks, CHUNK, HEAD_DIM), F32),
        pltpu.VMEM((nh, n_chunks, 2 * CHUNK, HEAD_DIM), BF16),
        pltpu.VMEM((nh, n_chunks, CHUNK, HEAD_DIM), BF16),
        pltpu.VMEM((nh, n_chunks, CHUNK, CHUNK), BF16),
        pltpu.VMEM((nh, n_chunks, 1, HEAD_DIM), F32),
    ]
    return pl.pallas_call(
        functools.partial(_gdn_kernel, reverse=reverse, n_chunks=n_chunks, d=d),
        grid=(batch, nt + 1),
        in_specs=[
            pl.BlockSpec((tt, width), lambda b, i: (tile_in(b, i), q_col)),
            pl.BlockSpec((tt, width), lambda b, i: (tile_in(b, i), q_col + 1)),
            pl.BlockSpec((tt, width), lambda b, i: (tile_in(b, i), q_col + 2)),
            pl.BlockSpec((tt, LANES), lambda b, i: (tile_in(b, i), 0)),
            pl.BlockSpec((n_chunks, SUBLANES, nh * CHUNK), lambda b, i: (tile_in(b, i), 0, 0)),
        ],
        out_specs=pl.BlockSpec((tt, width), lambda b, i: (tile_out(b, i), 0)),
        out_shape=jax.ShapeDtypeStruct((batch * seq, width), F32),
        scratch_shapes=[pltpu.VMEM((nh, HEAD_DIM, HEAD_DIM), F32)] + scratch_set + scratch_set,
        compiler_params=_cparams(("parallel", "arbitrary")),
        name="gdn_bwd" if reverse else "gdn_fwd",
    )(p, p, p, col, row)


def _mixer(x1, batch, seq, mix_norm, w_in, w_out, rg_conv_w, rg_conv_b, rg_gate_a_w, rg_gate_a_b,
           rg_gate_x_w, rg_gate_x_b, rg_lambda, gdn_conv_w, gdn_a_log, gdn_dt_bias, gdn_norm):
    rg_w = rg_conv_w.shape[1]
    qkv_w = gdn_conv_w.shape[1]
    gdn_vw = GDN_HEADS * HEAD_DIM
    n_main = 2 * rg_w + qkv_w + gdn_vw
    w_all = jnp.pad(w_in, ((0, 0), (0, n_main + LANES - w_in.shape[1]))).astype(BF16)
    p, ba = _in_proj(x1, mix_norm, w_all, rg_conv_w, rg_conv_b, gdn_conv_w, seq=seq)

    w_gates, b_gates = _rg_gate_weights(rg_gate_a_w, rg_gate_a_b, rg_gate_x_w, rg_gate_x_b)
    y_rg = _rglru(p, w_gates, b_gates, rg_lambda.astype(F32), batch=batch, seq=seq, gate_col0=rg_w // LANES)

    col, row = _gate_prep(ba, gdn_a_log, gdn_dt_bias)
    q_col = 2 * rg_w // gdn_vw
    o_f = _gdn(p, col, row, q_col=q_col, batch=batch, seq=seq, d=0)
    o_b = _gdn(p, col, row, q_col=q_col, batch=batch, seq=seq, d=1)
    return y_rg, o_f, o_b, p, (2 * rg_w + qkv_w) // gdn_vw, gdn_norm, w_out.astype(BF16)


def kernel(x, ffn1_norm, ffn1_w_gate, ffn1_w_up, ffn1_w_down, mix_norm, w_in, w_out, rg_conv_w, rg_conv_b, rg_gate_a_w, rg_gate_a_b, rg_gate_x_w, rg_gate_x_b, rg_lambda, gdn_conv_w, gdn_a_log, gdn_dt_bias, gdn_norm, ffn2_norm, ffn2_w_gate, ffn2_w_up, ffn2_w_down, final_norm):
    batch, seq, d_model = x.shape
    depth = ffn1_norm.shape[0]
    h = x.reshape(batch * seq, d_model)
    for l in range(depth):
        last = l == depth - 1
        h = _ffn(h, ffn1_norm[l], ffn1_w_gate[l].astype(BF16), ffn1_w_up[l].astype(BF16),
                 ffn1_w_down[l].astype(BF16), final_norm, final_norm=False)
        mixer = _mixer(h, batch, seq, mix_norm[l], w_in[l], w_out[l], rg_conv_w[l], rg_conv_b[l],
                   rg_gate_a_w[l], rg_gate_a_b[l], rg_gate_x_w[l], rg_gate_x_b[l], rg_lambda[l],
                   gdn_conv_w[l], gdn_a_log[l], gdn_dt_bias[l], gdn_norm[l])
        h = _ffn(h, ffn2_norm[l], ffn2_w_gate[l].astype(BF16), ffn2_w_up[l].astype(BF16),
                 ffn2_w_down[l].astype(BF16), final_norm, final_norm=last, mixer=mixer)
    return h.reshape(batch, seq, d_model)
```

```python
import functools
import itertools

import jax
import jax.numpy as jnp
from jax import lax
from jax.experimental import pallas as pl
from jax.experimental.pallas import tpu as pltpu

F32 = jnp.float32
BF16 = jnp.bfloat16

EPS = 1e-6
LOG2_E = 1.4426950408889634
RG_C = 8.0
RG_BLOCKS = 8
GDN_HEADS = 4
HEAD_DIM = 128
CHUNK = 64
LANES = 128
SUBLANES = 8
VMEM_LIMIT = 56 * 1024 * 1024


def _cparams(sem):
    return pltpu.CompilerParams(dimension_semantics=sem, vmem_limit_bytes=VMEM_LIMIT)


def _rms(x):
    return x * lax.rsqrt(jnp.mean(x * x, axis=-1, keepdims=True) + EPS)


def _sigmoid(x):
    return 1.0 / (1.0 + jnp.exp(-x))


def _softplus(x):
    return jnp.maximum(x, 0.0) + jnp.log(1.0 + jnp.exp(-jnp.abs(x)))


def _gated_mixer_rows(yrg_ref, of_ref, ob_ref, z_ref, gn_ref, wo_ref, rows):
    o = of_ref[rows, :] + ob_ref[rows, :]
    z = z_ref[rows, :]
    gn = gn_ref[...]
    parts = [yrg_ref[rows, :].astype(BF16)]
    for h in range(GDN_HEADS):
        hs = slice(h * HEAD_DIM, (h + 1) * HEAD_DIM)
        zh = z[:, hs]
        parts.append((_rms(o[:, hs]) * gn * (zh * _sigmoid(zh))).astype(BF16))
    return jnp.dot(jnp.concatenate(parts, axis=-1), wo_ref[...], preferred_element_type=F32)


W_CHUNKS = 8


def _stage_weight_bf16(w_hbm, w_scr, stage, sem):
    rows = w_hbm.shape[0] // W_CHUNKS
    copy = lambda k: pltpu.make_async_copy(w_hbm.at[pl.ds(k * rows, rows), :], stage.at[k % 2], sem.at[k % 2])
    copy(0).start()
    for k in range(W_CHUNKS):
        if k + 1 < W_CHUNKS:
            copy(k + 1).start()
        copy(k).wait()
        w_scr[k * rows:(k + 1) * rows, :] = stage[k % 2].astype(BF16)


def _ffn_kernel(*refs, final_norm, n_sub, with_mixer):
    refs, (wg_ref, wu_ref, wd_ref, stage_in, stage_dn, sem) = refs[:-6], refs[-6:]
    if with_mixer:
        x_ref, mixer_refs, (g_ref, wg_hbm, wu_hbm, wd_hbm, fg_ref, o_ref) = refs[0], refs[1:7], refs[7:]
    else:
        x_ref, g_ref, wg_hbm, wu_hbm, wd_hbm, fg_ref, o_ref = refs

    @pl.when(pl.program_id(0) == 0)
    def _():
        _stage_weight_bf16(wg_hbm, wg_ref, stage_in, sem)
        _stage_weight_bf16(wu_hbm, wu_ref, stage_in, sem)
        _stage_weight_bf16(wd_hbm, wd_ref, stage_dn, sem)

    sub = x_ref.shape[0] // n_sub
    g = g_ref[...]
    for r in range(n_sub):
        rows = slice(r * sub, (r + 1) * sub)
        x = x_ref[rows, :]
        if with_mixer:
            x = x + _gated_mixer_rows(*mixer_refs, rows)
        h = (_rms(x) * g).astype(BF16)
        gate = jnp.dot(h, wg_ref[...], preferred_element_type=F32)
        up = jnp.dot(h, wu_ref[...], preferred_element_type=F32)
        act = (gate * _sigmoid(gate) * up).astype(BF16)
        y = x + 0.5 * jnp.dot(act, wd_ref[...], preferred_element_type=F32)
        if final_norm:
            y = _rms(y) * fg_ref[...]
        o_ref[rows, :] = y


def _ffn(x, g, wg, wu, wd, fg, *, final_norm, mixer=None, tm=512, n_sub=2):
    m, d = x.shape
    f = wg.shape[1]
    assert m % tm == 0 and tm % n_sub == 0
    resident = dict(pipeline_mode=pl.Buffered(1))
    row_spec = lambda w, col=0: pl.BlockSpec((tm, w), lambda i: (i, col))
    const_spec = lambda shape, **kw: pl.BlockSpec(shape, lambda i: (0, 0), **kw)
    operands, in_specs = [x], [row_spec(d)]
    if mixer is not None:
        y_rg, o_f, o_b, p, z_col, gdn_norm, w_out = mixer
        wr, wv = y_rg.shape[1], o_f.shape[1]
        operands += [y_rg, o_f, o_b, p, gdn_norm.reshape(1, -1), w_out]
        in_specs += [row_spec(wr), row_spec(wv), row_spec(wv), row_spec(wv, z_col),
                     const_spec((1, HEAD_DIM)), const_spec((wr + wv, d), **resident)]
    assert d % W_CHUNKS == 0 and f % (W_CHUNKS * 2 * SUBLANES) == 0
    in_hbm = pl.BlockSpec(memory_space=pl.ANY)
    operands += [g.reshape(1, d), wg, wu, wd, fg.reshape(1, d)]
    in_specs += [const_spec((1, d)), in_hbm, in_hbm, in_hbm, const_spec((1, d))]
    return pl.pallas_call(
        functools.partial(_ffn_kernel, final_norm=final_norm, n_sub=n_sub, with_mixer=mixer is not None),
        grid=(m // tm,),
        in_specs=in_specs,
        out_specs=row_spec(d),
        out_shape=jax.ShapeDtypeStruct((m, d), F32),
        scratch_shapes=[
            pltpu.VMEM((d, f), BF16), pltpu.VMEM((d, f), BF16), pltpu.VMEM((f, d), BF16),
            pltpu.VMEM((2, d // W_CHUNKS, f), F32), pltpu.VMEM((2, f // W_CHUNKS, d), F32),
            pltpu.SemaphoreType.DMA((2,)),
        ],
        compiler_params=_cparams(("arbitrary",)),
        name="ffn_final" if final_norm else "ffn",
    )(*operands)


def _inproj_kernel(x_ref, xp_ref, xn_ref, g_ref, w_ref, cwr_ref, cbr_ref, cwq_ref, p_ref, ba_ref, stage_scr, *,
                   n_sub, tiles_per_seq, rg_w, qkv_w, n_qk, q_scale):
    tm, n = p_ref.shape
    sub = tm // n_sub
    hal = SUBLANES
    i = pl.program_id(0)
    pos = i % tiles_per_seq
    prev = jnp.where(pos == 0, 0.0, xp_ref[...])
    nxt = jnp.where(pos == tiles_per_seq - 1, 0.0, xn_ref[...])
    g = g_ref[...]
    cwr = cwr_ref[...]
    mid = slice(hal, hal + sub)
    blk = GDN_HEADS * HEAD_DIM

    def stage(slot, pw):
        for j in range(blk // LANES):
            stage_scr[slot * (blk // LANES) + j] = pw[:, j * LANES:(j + 1) * LANES]

    def conv(slab, cw):
        tap = lambda off: stage_scr[slab, hal + off:hal + off + sub, :]
        acc = cw[0:1, :] * tap(-2) + cw[1:2, :] * tap(-1)
        return acc + cw[2:3, :] * tap(0) + cw[3:4, :] * tap(1)

    for r in range(n_sub):
        lo, hi = r * sub - hal, (r + 1) * sub + hal
        parts = ([prev] if lo < 0 else []) + [x_ref[max(lo, 0):min(hi, tm), :]] + ([nxt] if hi > tm else [])
        xw = jnp.concatenate(parts, axis=0)
        h = (_rms(xw) * g).astype(BF16)
        rows = slice(r * sub, (r + 1) * sub)
        proj = lambda c0, c1: jnp.dot(h, w_ref[:, c0:c1], preferred_element_type=F32)
        stage(0, proj(0, rg_w))
        for c in range(0, rg_w, LANES):
            p_ref[rows, c:c + LANES] = conv(c // LANES, cwr[:, c:c + LANES]) + cbr_ref[:, c:c + LANES]
        p_ref[rows, rg_w:2 * rg_w] = proj(rg_w, 2 * rg_w)[mid]
        for c0 in range(0, qkv_w, blk):
            slot = 1 + c0 // blk
            stage(slot, proj(2 * rg_w + c0, 2 * rg_w + c0 + blk))
            for c in range(c0, c0 + blk, LANES):
                y = conv((blk + c) // LANES, cwq_ref[:, c:c + LANES])
                y = y * _sigmoid(y)
                if c < n_qk * LANES:
                    scale = q_scale if c < n_qk * LANES // 2 else 1.0
                    y = y * (lax.rsqrt(jnp.sum(y * y, axis=-1, keepdims=True) + EPS) * scale)
                p_ref[rows, 2 * rg_w + c:2 * rg_w + c + LANES] = y
        tail = proj(2 * rg_w + qkv_w, n + LANES)[mid]
        p_ref[rows, 2 * rg_w + qkv_w:n] = tail[:, :n - 2 * rg_w - qkv_w]
        ba_ref[rows, :] = tail[:, n - 2 * rg_w - qkv_w:]


def _in_proj(x, g, w_all, rg_conv_w, rg_conv_b, qkv_conv_w, *, seq, tm=512, n_sub=2):
    m, d = x.shape
    n = w_all.shape[1] - LANES
    rg_w, qkv_w = rg_conv_w.shape[1], qkv_conv_w.shape[1]
    assert m % tm == 0 and tm % n_sub == 0 and seq % tm == 0
    assert rg_w == GDN_HEADS * HEAD_DIM and qkv_w % rg_w == 0
    hb = tm // SUBLANES
    const_spec = lambda shape, **kw: pl.BlockSpec(shape, lambda i: (0, 0), **kw)
    return pl.pallas_call(
        functools.partial(_inproj_kernel, n_sub=n_sub, tiles_per_seq=seq // tm, rg_w=rg_w, qkv_w=qkv_w,
                          n_qk=2 * GDN_HEADS, q_scale=HEAD_DIM ** -0.5),
        grid=(m // tm,),
        in_specs=[
            pl.BlockSpec((tm, d), lambda i: (i, 0)),
            pl.BlockSpec((SUBLANES, d), lambda i: (jnp.maximum(i * hb - 1, 0), 0)),
            pl.BlockSpec((SUBLANES, d), lambda i: (jnp.minimum((i + 1) * hb, m // SUBLANES - 1), 0)),
            const_spec((1, d)),
            const_spec((d, n + LANES), pipeline_mode=pl.Buffered(1)),
            const_spec((4, rg_w)),
            const_spec((1, rg_w)),
            const_spec((4, qkv_w)),
        ],
        out_specs=[
            pl.BlockSpec((tm, n), lambda i: (i, 0)),
            pl.BlockSpec((tm, LANES), lambda i: (i, 0)),
        ],
        out_shape=[jax.ShapeDtypeStruct((m, n), F32), jax.ShapeDtypeStruct((m, LANES), F32)],
        scratch_shapes=[pltpu.VMEM(((rg_w + qkv_w) // LANES, tm // n_sub + 2 * SUBLANES, LANES), F32)],
        compiler_params=_cparams(("parallel",)),
        name="in_proj",
    )(x, x, x, g.reshape(1, d), w_all, rg_conv_w, rg_conv_b.reshape(1, -1), qkv_conv_w)


RG_SEGS = 2 * SUBLANES
RG_PAD = 4


def _rglru_kernel(x_ref, gate_ref, wg_ref, bg_ref, lam_ref, o_ref,
                  af_scr, bf_scr, ab_scr, bb_scr, *, seq, rt):
    seg_len = seq // RG_SEGS
    pitch = seg_len + RG_PAD
    tiles_per_seg = seg_len // rt
    n_tiles = seq // rt
    a_scr = (af_scr, ab_scr)
    b_scr = (bf_scr, bb_scr)

    bg = bg_ref[...]
    half_log2_a = (-0.5 * RG_C * LOG2_E) * _softplus(-lam_ref[...])

    def gates(i, _):
        r0 = pl.multiple_of(i * rt, rt)
        xc = x_ref[pl.ds(r0, rt), :]
        pre = jnp.dot(xc.astype(BF16), wg_ref[...], preferred_element_type=F32) + bg
        hx = 0.5 * xc
        seg = i // tiles_per_seg
        off = seg * pitch + (i - seg * tiles_per_seg) * rt
        for d in range(2):
            tr = jnp.tanh(pre[:, (2 * d) * LANES:(2 * d + 1) * LANES])
            ti = jnp.tanh(pre[:, (2 * d + 1) * LANES:(2 * d + 2) * LANES])
            hl = half_log2_a[d:d + 1, :]
            a = jnp.exp2(hl + hl * tr)
            om = 1.0 - a * a
            b = jnp.where(om > 0.0, om * lax.rsqrt(om), 0.0) * (hx + hx * ti)
            a_scr[d][pl.ds(off, rt), :] = a
            b_scr[d][pl.ds(off, rt), :] = b
        return 0

    lax.fori_loop(0, n_tiles, gates, 0)

    def scan(t, carry):
        hf, pf, hb, pb = carry
        idx = pl.ds(t, RG_SEGS, stride=pitch)
        a = af_scr[idx, :]
        hf = a * hf + bf_scr[idx, :]
        pf = a * pf
        bf_scr[idx, :] = hf
        af_scr[idx, :] = pf
        idx = pl.ds(seg_len - 1 - t, RG_SEGS, stride=pitch)
        a = ab_scr[idx, :]
        hb = a * hb + bb_scr[idx, :]
        pb = a * pb
        bb_scr[idx, :] = hb
        ab_scr[idx, :] = pb
        return hf, pf, hb, pb

    zeros = jnp.zeros((RG_SEGS, LANES), F32)
    ones = jnp.ones((RG_SEGS, LANES), F32)
    hf, pf, hb, pb = lax.fori_loop(0, seg_len, scan, (zeros, ones, zeros, ones), unroll=8)

    cf = [jnp.zeros((1, LANES), F32)]
    for s in range(1, RG_SEGS):
        cf.append(hf[s - 1:s, :] + pf[s - 1:s, :] * cf[s - 1])
    cbk = [None] * RG_SEGS
    cbk[RG_SEGS - 1] = jnp.zeros((1, LANES), F32)
    for s in range(RG_SEGS - 2, -1, -1):
        cbk[s] = hb[s + 1:s + 2, :] + pb[s + 1:s + 2, :] * cbk[s + 1]

    for s in range(RG_SEGS):
        def fix(k, _, s=s):
            off = s * pitch + k * rt
            rows = pl.multiple_of(s * seg_len + k * rt, SUBLANES)
            h = bf_scr[pl.ds(off, rt), :] + af_scr[pl.ds(off, rt), :] * cf[s]
            h = h + (bb_scr[pl.ds(off, rt), :] + ab_scr[pl.ds(off, rt), :] * cbk[s])
            o_ref[pl.ds(rows, rt), :] = h * jax.nn.gelu(gate_ref[pl.ds(rows, rt), :], approximate=True)
            return 0

        lax.fori_loop(0, tiles_per_seg, fix, 0)


def _rglru(p, w_gates, b_gates, lam, *, batch, seq, gate_col0, rt=512):
    ngrp = w_gates.shape[0]
    seg_len = seq // RG_SEGS
    rt = min(rt, seg_len)
    assert seq % RG_SEGS == 0 and seg_len % rt == 0
    scr = pltpu.VMEM((RG_SEGS * (seg_len + RG_PAD), LANES), F32)
    return pl.pallas_call(
        functools.partial(_rglru_kernel, seq=seq, rt=rt),
        grid=(batch, ngrp),
        in_specs=[
            pl.BlockSpec((seq, LANES), lambda b, c: (b, c)),
            pl.BlockSpec((seq, LANES), lambda b, c: (b, gate_col0 + c)),
            pl.BlockSpec((None, LANES, 4 * LANES), lambda b, c: (c, 0, 0)),
            pl.BlockSpec((None, 1, 4 * LANES), lambda b, c: (c, 0, 0)),
            pl.BlockSpec((2, LANES), lambda b, c: (0, c)),
        ],
        out_specs=pl.BlockSpec((seq, LANES), lambda b, c: (b, c)),
        out_shape=jax.ShapeDtypeStruct((batch * seq, ngrp * LANES), F32),
        scratch_shapes=[scr, scr, scr, scr],
        compiler_params=_cparams(("parallel", "parallel")),
        name="rglru",
    )(p, p, w_gates, b_gates, lam)


def _rg_gate_weights(wa, ba, wx, bx):
    ndir, nblk, blk, _ = wa.shape
    ngrp = nblk * blk // LANES
    per = LANES // blk

    def bd(w):
        w = w.reshape(ndir, ngrp, per, blk, blk)
        eye = jnp.eye(per, dtype=w.dtype)
        return jnp.einsum("dgpij,pq->dgpiqj", w, eye).reshape(ndir, ngrp, LANES, LANES)

    a, x = bd(wa), bd(wx)
    w = jnp.concatenate([a[0], x[0], a[1], x[1]], axis=-1)
    ba = ba.reshape(ndir, ngrp, 1, LANES)
    bx = bx.reshape(ndir, ngrp, 1, LANES)
    b = jnp.concatenate([ba[0], bx[0], ba[1], bx[1]], axis=-1)
    return (0.5 * w).astype(BF16), (0.5 * b).astype(F32)


def _gate_kernel(ba_ref, alog_ref, dtb_ref, col_ref, row_ref, *, tm):
    nh = GDN_HEADS
    ri = lax.broadcasted_iota(jnp.int32, (LANES, LANES), 0)
    ci = lax.broadcasted_iota(jnp.int32, (LANES, LANES), 1)
    same = (ri // CHUNK) == (ci // CHUNK)
    lower = jnp.where(same & (ri >= ci), 1.0, 0.0).astype(BF16)
    upper = jnp.where(same & (ri <= ci), 1.0, 0.0).astype(BF16)
    masks = jnp.concatenate([lower, upper], axis=0)
    lane = ci
    neg_a = -jnp.exp(alog_ref[...])
    dtb = dtb_ref[...]
    for k in range(tm // LANES):
        sl = slice(k * LANES, (k + 1) * LANES)
        raw = ba_ref[sl, :]
        beta = _sigmoid(raw)
        gk = neg_a * _softplus(raw + dtb)
        gk = jnp.where((lane >= 2 * nh) & (lane < 4 * nh), gk, 0.0)
        csum = jnp.zeros((2 * LANES, LANES), F32)
        rest = gk
        for _ in range(3):
            piece = rest.astype(BF16)
            csum = csum + jnp.dot(masks, piece, preferred_element_type=F32)
            rest = rest - piece.astype(F32)
        gc = jnp.where(lane < 3 * nh, csum[:LANES], csum[LANES:])
        col = jnp.where(lane < 2 * nh, beta, gc)
        col_ref[sl, :] = col
        colt = col.T
        for q in range(LANES // CHUNK):
            rows = [jnp.concatenate([colt[(2 + dr) * nh + h:(2 + dr) * nh + h + 1, q * CHUNK:(q + 1) * CHUNK]
                                     for h in range(nh)], axis=1) for dr in range(2)]
            rows.append(jnp.zeros((SUBLANES - 2, nh * CHUNK), F32))
            row_ref[k * (LANES // CHUNK) + q] = jnp.concatenate(rows, axis=0)


def _gate_prep(ba, a_log, dt_bias, *, tm=2048):
    m = ba.shape[0]
    nh = GDN_HEADS
    pad = lambda v: jnp.zeros((1, LANES), F32).at[0, 2 * nh:4 * nh].set(v.reshape(-1).astype(F32))
    return pl.pallas_call(
        functools.partial(_gate_kernel, tm=tm),
        grid=(m // tm,),
        in_specs=[
            pl.BlockSpec((tm, LANES), lambda i: (i, 0)),
            pl.BlockSpec((1, LANES), lambda i: (0, 0)),
            pl.BlockSpec((1, LANES), lambda i: (0, 0)),
        ],
        out_specs=[
            pl.BlockSpec((tm, LANES), lambda i: (i, 0)),
            pl.BlockSpec((tm // CHUNK, SUBLANES, nh * CHUNK), lambda i: (i, 0, 0)),
        ],
        out_shape=[jax.ShapeDtypeStruct((m, LANES), F32),
                   jax.ShapeDtypeStruct((m // CHUNK, SUBLANES, nh * CHUNK), F32)],
        compiler_params=_cparams(("parallel",)),
        name="gate_prep",
    )(ba, pad(a_log), pad(dt_bias))


def _bmm(a, b):
    return lax.dot_general(a.astype(BF16), b.astype(BF16), (((2,), (1,)), ((0,), (0,))),
                           preferred_element_type=F32)


def _bmm_nt(a, b):
    return lax.dot_general(a.astype(BF16), b.astype(BF16), (((2,), (2,)), ((0,), (0,))),
                           preferred_element_type=F32)


def _bmm_tn(a, b):
    return lax.dot_general(a.astype(BF16), b.astype(BF16), (((1,), (1,)), ((0,), (0,))),
                           preferred_element_type=F32)


def _unit_tri_inverse_levels(lm, ri, ci, bdmask, nb):
    def blockdiag(a):
        ab = a.astype(BF16)
        return jnp.where(bdmask, jnp.concatenate([ab] * nb, axis=1), jnp.zeros((), BF16))

    eye = jnp.where(ri == ci, 1.0, 0.0).astype(F32)
    x = eye - jnp.where((ri // 2) == (ci // 2), lm, 0.0)
    s = 2
    while s < CHUNK:
        cm = jnp.where(((ri // (2 * s)) == (ci // (2 * s))) & ((ri // s) != (ci // s)), lm, 0.0)
        x = x - _bmm(_bmm(x, blockdiag(cm)), blockdiag(x))
        yield x
        s *= 2


def _gdn_chunk_local(q_ref, k_ref, v_ref, col_ref, row_ref, dst, *, reverse, g, d):
    u_scr, wq_scr, kd_scr, at_scr, cd_scr = dst
    nh, c, hd = GDN_HEADS, CHUNK, HEAD_DIM
    pw = nh * c
    fw = nh * hd
    ri = lax.broadcasted_iota(jnp.int32, (1, c, pw), 1)
    ci = lax.broadcasted_iota(jnp.int32, (1, c, pw), 2) % c
    if reverse:
        incl, strict, last = ri <= ci, ri < ci, 0
    else:
        incl, strict, last = ri >= ci, ri > ci, c - 1
    bdmask = (lax.broadcasted_iota(jnp.int32, (1, pw, pw), 1) // c
              == lax.broadcasted_iota(jnp.int32, (1, pw, pw), 2) // c)
    kmask = (lax.broadcasted_iota(jnp.int32, (1, pw, fw), 1) // c
             == lax.broadcasted_iota(jnp.int32, (1, pw, fw), 2) // hd)
    low_half = lax.broadcasted_iota(jnp.int32, (g * c, hd), 1) < c

    col = col_ref[...]
    bcast = lambda j: jnp.broadcast_to(col[:, j:j + 1], (g * c, hd))
    beta = jnp.concatenate([bcast(d * nh + h) for h in range(nh)], axis=1).reshape(g, c, fw)
    gcs = [bcast((2 + d) * nh + h) for h in range(nh)]
    gc = jnp.concatenate(gcs, axis=1).reshape(g, c, fw)
    gc_col = jnp.concatenate([jnp.where(low_half, gcs[h], gcs[h + 1]) for h in range(0, nh, 2)],
                             axis=1).reshape(g, c, pw)
    gc_row = row_ref[:, d:d + 1, :]
    g_last = gc[:, last:last + 1, :]
    eg = jnp.exp(gc)
    decay = jnp.where(incl, jnp.exp(gc_col - gc_row), 0.0)
    q = q_ref[...].reshape(g, c, fw)
    k = k_ref[...].reshape(g, c, fw)
    v = v_ref[...].reshape(g, c, fw)
    k_beta = k * beta
    v_beta = v * beta
    kb16 = k.astype(BF16)
    k_bd = jnp.where(kmask, jnp.concatenate([kb16] * nh, axis=1), jnp.zeros((), BF16))
    kq = _bmm_nt(jnp.concatenate([k_beta, q], axis=1), k_bd)
    lm = jnp.where(strict, kq[:, :c] * decay, 0.0)
    attn = kq[:, c:] * decay
    yield
    for t in _unit_tri_inverse_levels(lm, ri, ci, bdmask, nh):
        yield
    kbe = k_beta * eg
    qd = q * eg
    kd = k * jnp.exp(g_last - gc)
    cd = jnp.exp(g_last)
    for h in range(nh):
        hs = slice(h * hd, (h + 1) * hd)
        ps = slice(h * c, (h + 1) * c)
        uw = _bmm(t[:, :, ps], jnp.concatenate([v_beta[:, :, hs], kbe[:, :, hs]], axis=-1))
        u_scr[h] = uw[:, :, :hd]
        wq_scr[h] = jnp.concatenate([uw[:, :, hd:], qd[:, :, hs]], axis=1).astype(BF16)
        at_scr[h] = attn[:, :, ps].astype(BF16)
        kd_scr[h] = kd[:, :, hs].astype(BF16)
        cd_scr[h] = cd[:, :, hs]
        if h % 2 == 1:
            yield


def _gdn_recurrence(src, s_scr, o_ref, *, reverse, g):
    u_scr, wq_scr, kd_scr, at_scr, cd_scr = src
    nh, c, hd = GDN_HEADS, CHUNK, HEAD_DIM
    for step in range(g):
        n = g - 1 - step if reverse else step
        state = s_scr[...]
        sb = state.astype(BF16)
        ws = _bmm(wq_scr[:, n], sb)
        v_new = u_scr[:, n] - ws[:, :c]
        vb = v_new.astype(BF16)
        o = ws[:, c:] + _bmm(at_scr[:, n], vb)
        s_scr[...] = state * cd_scr[:, n] + _bmm_tn(kd_scr[:, n], vb)
        for h in range(nh):
            o_ref[n * c:(n + 1) * c, h * hd:(h + 1) * hd] = o[h]
        yield


def _gdn_kernel(q_ref, k_ref, v_ref, col_ref, row_ref, o_ref, s_scr, *sets, reverse, n_chunks, d):
    set0, set1 = sets[:len(sets) // 2], sets[len(sets) // 2:]
    i = pl.program_id(1)

    @pl.when(i == 0)
    def _():
        s_scr[...] = jnp.zeros_like(s_scr)
        for r in set1:
            r[...] = jnp.zeros_like(r)

    def step(dst, src):
        local = _gdn_chunk_local(q_ref, k_ref, v_ref, col_ref, row_ref, dst, reverse=reverse, g=n_chunks, d=d)
        recur = _gdn_recurrence(src, s_scr, o_ref, reverse=reverse, g=n_chunks)
        for _ in itertools.zip_longest(local, recur):
            pass

    @pl.when(i % 2 == 0)
    def _():
        step(set0, set1)

    @pl.when(i % 2 == 1)
    def _():
        step(set1, set0)


def _gdn(p, col, row, *, q_col, batch, seq, d, tt=512):
    nh = GDN_HEADS
    width = nh * HEAD_DIM
    nt = seq // tt
    n_chunks = tt // CHUNK
    reverse = d == 1
    order = (lambda t: nt - 1 - t) if reverse else (lambda t: t)
    tile_in = lambda b, i: b * nt + order(jnp.minimum(i, nt - 1))
    tile_out = lambda b, i: b * nt + order(jnp.maximum(i - 1, 0))
    scratch_set = [
        pltpu.VMEM((nh, n_chunks, CHUNK, HEAD_DIM), F32),
        pltpu.VMEM((nh, n_chunks, 2 * CHUNK, HEAD_DIM), BF16),
        pltpu.VMEM((nh, n_chunks, CHUNK, HEAD_DIM), BF16),
        pltpu.VMEM((nh, n_chunks, CHUNK, CHUNK), BF16),
        pltpu.VMEM((nh, n_chunks, 1, HEAD_DIM), F32),
    ]
    return pl.pallas_call(
        functools.partial(_gdn_kernel, reverse=reverse, n_chunks=n_chunks, d=d),
        grid=(batch, nt + 1),
        in_specs=[
            pl.BlockSpec((tt, width), lambda b, i: (tile_in(b, i), q_col)),
            pl.BlockSpec((tt, width), lambda b, i: (tile_in(b, i), q_col + 1)),
            pl.BlockSpec((tt, width), lambda b, i: (tile_in(b, i), q_col + 2)),
            pl.BlockSpec((tt, LANES), lambda b, i: (tile_in(b, i), 0)),
            pl.BlockSpec((n_chunks, SUBLANES, nh * CHUNK), lambda b, i: (tile_in(b, i), 0, 0)),
        ],
        out_specs=pl.BlockSpec((tt, width), lambda b, i: (tile_out(b, i), 0)),
        out_shape=jax.ShapeDtypeStruct((batch * seq, width), F32),
        scratch_shapes=[pltpu.VMEM((nh, HEAD_DIM, HEAD_DIM), F32)] + scratch_set + scratch_set,
        compiler_params=_cparams(("parallel", "arbitrary")),
        name="gdn_bwd" if reverse else "gdn_fwd",
    )(p, p, p, col, row)


def _mixer(x1, batch, seq, mix_norm, w_in, w_out, rg_conv_w, rg_conv_b, rg_gate_a_w, rg_gate_a_b,
           rg_gate_x_w, rg_gate_x_b, rg_lambda, gdn_conv_w, gdn_a_log, gdn_dt_bias, gdn_norm):
    rg_w = rg_conv_w.shape[1]
    qkv_w = gdn_conv_w.shape[1]
    gdn_vw = GDN_HEADS * HEAD_DIM
    n_main = 2 * rg_w + qkv_w + gdn_vw
    w_all = jnp.pad(w_in, ((0, 0), (0, n_main + LANES - w_in.shape[1]))).astype(BF16)
    p, ba = _in_proj(x1, mix_norm, w_all, rg_conv_w, rg_conv_b, gdn_conv_w, seq=seq)

    w_gates, b_gates = _rg_gate_weights(rg_gate_a_w, rg_gate_a_b, rg_gate_x_w, rg_gate_x_b)
    y_rg = _rglru(p, w_gates, b_gates, rg_lambda.astype(F32), batch=batch, seq=seq, gate_col0=rg_w // LANES)

    col, row = _gate_prep(ba, gdn_a_log, gdn_dt_bias)
    q_col = 2 * rg_w // gdn_vw
    o_f = _gdn(p, col, row, q_col=q_col, batch=batch, seq=seq, d=0)
    o_b = _gdn(p, col, row, q_col=q_col, batch=batch, seq=seq, d=1)
    return y_rg, o_f, o_b, p, (2 * rg_w + qkv_w) // gdn_vw, gdn_norm, w_out.astype(BF16)


def kernel(x, ffn1_norm, ffn1_w_gate, ffn1_w_up, ffn1_w_down, mix_norm, w_in, w_out, rg_conv_w, rg_conv_b, rg_gate_a_w, rg_gate_a_b, rg_gate_x_w, rg_gate_x_b, rg_lambda, gdn_conv_w, gdn_a_log, gdn_dt_bias, gdn_norm, ffn2_norm, ffn2_w_gate, ffn2_w_up, ffn2_w_down, final_norm):
    batch, seq, d_model = x.shape
    depth = ffn1_norm.shape[0]
    h = x.reshape(batch * seq, d_model)
    for l in range(depth):
        last = l == depth - 1
        h = _ffn(h, ffn1_norm[l], ffn1_w_gate[l], ffn1_w_up[l], ffn1_w_down[l], final_norm, final_norm=False)
        mixer = _mixer(h, batch, seq, mix_norm[l], w_in[l], w_out[l], rg_conv_w[l], rg_conv_b[l],
                   rg_gate_a_w[l], rg_gate_a_b[l], rg_gate_x_w[l], rg_gate_x_b[l], rg_lambda[l],
                   gdn_conv_w[l], gdn_a_log[l], gdn_dt_bias[l], gdn_norm[l])
        h = _ffn(h, ffn2_norm[l], ffn2_w_gate[l], ffn2_w_up[l], ffn2_w_down[l], final_norm,
                 final_norm=last, mixer=mixer)
    return h.reshape(batch, seq, d_model)
```

```python
import functools
import itertools

import jax
import jax.numpy as jnp
from jax import lax
from jax.experimental import pallas as pl
from jax.experimental.pallas import tpu as pltpu

F32 = jnp.float32
BF16 = jnp.bfloat16

EPS = 1e-6
LOG2_E = 1.4426950408889634
RG_C = 8.0
RG_BLOCKS = 8
GDN_HEADS = 4
HEAD_DIM = 128
CHUNK = 64
LANES = 128
SUBLANES = 8
VMEM_LIMIT = 56 * 1024 * 1024


def _cparams(sem):
    return pltpu.CompilerParams(dimension_semantics=sem, vmem_limit_bytes=VMEM_LIMIT)


def _rms(x):
    return x * lax.rsqrt(jnp.mean(x * x, axis=-1, keepdims=True) + EPS)


def _sigmoid(x):
    return 1.0 / (1.0 + jnp.exp(-x))


def _softplus(x):
    return jnp.maximum(x, 0.0) + jnp.log(1.0 + jnp.exp(-jnp.abs(x)))


def _gated_mixer_rows(yrg_ref, of_ref, ob_ref, z_ref, gn_ref, wo_ref, rows):
    o = of_ref[rows, :] + ob_ref[rows, :]
    z = z_ref[rows, :]
    gn = gn_ref[...]
    parts = [yrg_ref[rows, :].astype(BF16)]
    for h in range(GDN_HEADS):
        hs = slice(h * HEAD_DIM, (h + 1) * HEAD_DIM)
        zh = z[:, hs]
        parts.append((_rms(o[:, hs]) * gn * (zh * _sigmoid(zh))).astype(BF16))
    return jnp.dot(jnp.concatenate(parts, axis=-1), wo_ref[...], preferred_element_type=F32)


W_CHUNKS = 8


def _stage_weight_bf16(w_hbm, w_scr, stage, sem):
    rows = w_hbm.shape[0] // W_CHUNKS
    copy = lambda k: pltpu.make_async_copy(w_hbm.at[pl.ds(k * rows, rows), :], stage.at[k % 2], sem.at[k % 2])
    copy(0).start()
    for k in range(W_CHUNKS):
        if k + 1 < W_CHUNKS:
            copy(k + 1).start()
        copy(k).wait()
        w_scr[k * rows:(k + 1) * rows, :] = stage[k % 2].astype(BF16)


def _ffn_kernel(*refs, final_norm, n_sub, with_mixer):
    refs, (wg_ref, wu_ref, wd_ref, stage_in, stage_dn, sem) = refs[:-6], refs[-6:]
    if with_mixer:
        x_ref, mixer_refs, (g_ref, wg_hbm, wu_hbm, wd_hbm, fg_ref, o_ref) = refs[0], refs[1:7], refs[7:]
    else:
        x_ref, g_ref, wg_hbm, wu_hbm, wd_hbm, fg_ref, o_ref = refs

    @pl.when(pl.program_id(0) == 0)
    def _():
        _stage_weight_bf16(wg_hbm, wg_ref, stage_in, sem)
        _stage_weight_bf16(wu_hbm, wu_ref, stage_in, sem)
        _stage_weight_bf16(wd_hbm, wd_ref, stage_dn, sem)

    sub = x_ref.shape[0] // n_sub
    g = g_ref[...]
    for r in range(n_sub):
        rows = slice(r * sub, (r + 1) * sub)
        x = x_ref[rows, :]
        if with_mixer:
            x = x + _gated_mixer_rows(*mixer_refs, rows)
        h = (_rms(x) * g).astype(BF16)
        gate = jnp.dot(h, wg_ref[...], preferred_element_type=F32)
        up = jnp.dot(h, wu_ref[...], preferred_element_type=F32)
        act = (gate * _sigmoid(gate) * up).astype(BF16)
        y = x + 0.5 * jnp.dot(act, wd_ref[...], preferred_element_type=F32)
        if final_norm:
            y = _rms(y) * fg_ref[...]
        o_ref[rows, :] = y


def _ffn(x, g, wg, wu, wd, fg, *, final_norm, mixer=None, tm=512, n_sub=2):
    m, d = x.shape
    f = wg.shape[1]
    assert m % tm == 0 and tm % n_sub == 0
    resident = dict(pipeline_mode=pl.Buffered(1))
    row_spec = lambda w, col=0: pl.BlockSpec((tm, w), lambda i: (i, col))
    const_spec = lambda shape, **kw: pl.BlockSpec(shape, lambda i: (0, 0), **kw)
    operands, in_specs = [x], [row_spec(d)]
    if mixer is not None:
        y_rg, o_f, o_b, p, z_col, gdn_norm, w_out = mixer
        wr, wv = y_rg.shape[1], o_f.shape[1]
        operands += [y_rg, o_f, o_b, p, gdn_norm.reshape(1, -1), w_out]
        in_specs += [row_spec(wr), row_spec(wv), row_spec(wv), row_spec(wv, z_col),
                     const_spec((1, HEAD_DIM)), const_spec((wr + wv, d), **resident)]
    assert d % W_CHUNKS == 0 and f % (W_CHUNKS * 2 * SUBLANES) == 0
    in_hbm = pl.BlockSpec(memory_space=pl.ANY)
    operands += [g.reshape(1, d), wg, wu, wd, fg.reshape(1, d)]
    in_specs += [const_spec((1, d)), in_hbm, in_hbm, in_hbm, const_spec((1, d))]
    return pl.pallas_call(
        functools.partial(_ffn_kernel, final_norm=final_norm, n_sub=n_sub, with_mixer=mixer is not None),
        grid=(m // tm,),
        in_specs=in_specs,
        out_specs=row_spec(d),
        out_shape=jax.ShapeDtypeStruct((m, d), F32),
        scratch_shapes=[
            pltpu.VMEM((d, f), BF16), pltpu.VMEM((d, f), BF16), pltpu.VMEM((f, d), BF16),
            pltpu.VMEM((2, d // W_CHUNKS, f), F32), pltpu.VMEM((2, f // W_CHUNKS, d), F32),
            pltpu.SemaphoreType.DMA((2,)),
        ],
        compiler_params=_cparams(("arbitrary",)),
        name="ffn_final" if final_norm else "ffn",
    )(*operands)


def _inproj_kernel(x_ref, xp_ref, xn_ref, g_ref, w32_ref, cwr_ref, cbr_ref, cwq_ref, p_ref, ba_ref,
                   stage_scr, w_ref, *, n_sub, tiles_per_seq, rg_w, qkv_w, n_qk, q_scale):
    tm, n = p_ref.shape
    sub = tm // n_sub
    hal = SUBLANES
    i = pl.program_id(0)

    @pl.when(i == 0)
    def _():
        n_in = w32_ref.shape[1]
        for c in range(0, n, 4 * LANES):
            w_ref[:, c:c + 4 * LANES] = w32_ref[:, c:c + 4 * LANES].astype(BF16)
        w_ref[:, n:] = jnp.zeros((w_ref.shape[0], LANES), BF16)
        w_ref[:, n:n_in] = w32_ref[:, n:n_in].astype(BF16)

    pos = i % tiles_per_seq
    prev = jnp.where(pos == 0, 0.0, xp_ref[...])
    nxt = jnp.where(pos == tiles_per_seq - 1, 0.0, xn_ref[...])
    g = g_ref[...]
    cwr = cwr_ref[...]
    mid = slice(hal, hal + sub)
    blk = GDN_HEADS * HEAD_DIM

    def stage(slot, pw):
        for j in range(blk // LANES):
            stage_scr[slot * (blk // LANES) + j] = pw[:, j * LANES:(j + 1) * LANES]

    def conv(slab, cw):
        tap = lambda off: stage_scr[slab, hal + off:hal + off + sub, :]
        acc = cw[0:1, :] * tap(-2) + cw[1:2, :] * tap(-1)
        return acc + cw[2:3, :] * tap(0) + cw[3:4, :] * tap(1)

    for r in range(n_sub):
        lo, hi = r * sub - hal, (r + 1) * sub + hal
        parts = ([prev] if lo < 0 else []) + [x_ref[max(lo, 0):min(hi, tm), :]] + ([nxt] if hi > tm else [])
        xw = jnp.concatenate(parts, axis=0)
        h = (_rms(xw) * g).astype(BF16)
        rows = slice(r * sub, (r + 1) * sub)
        proj = lambda c0, c1: jnp.dot(h, w_ref[:, c0:c1], preferred_element_type=F32)
        stage(0, proj(0, rg_w))
        for c in range(0, rg_w, LANES):
            p_ref[rows, c:c + LANES] = conv(c // LANES, cwr[:, c:c + LANES]) + cbr_ref[:, c:c + LANES]
        p_ref[rows, rg_w:2 * rg_w] = proj(rg_w, 2 * rg_w)[mid]
        for c0 in range(0, qkv_w, blk):
            slot = 1 + c0 // blk
            stage(slot, proj(2 * rg_w + c0, 2 * rg_w + c0 + blk))
            for c in range(c0, c0 + blk, LANES):
                y = conv((blk + c) // LANES, cwq_ref[:, c:c + LANES])
                y = y * _sigmoid(y)
                if c < n_qk * LANES:
                    scale = q_scale if c < n_qk * LANES // 2 else 1.0
                    y = y * (lax.rsqrt(jnp.sum(y * y, axis=-1, keepdims=True) + EPS) * scale)
                p_ref[rows, 2 * rg_w + c:2 * rg_w + c + LANES] = y
        tail = proj(2 * rg_w + qkv_w, n + LANES)[mid]
        p_ref[rows, 2 * rg_w + qkv_w:n] = tail[:, :n - 2 * rg_w - qkv_w]
        ba_ref[rows, :] = tail[:, n - 2 * rg_w - qkv_w:]


def _in_proj(x, g, w_in, rg_conv_w, rg_conv_b, qkv_conv_w, *, n, seq, tm=512, n_sub=2):
    m, d = x.shape
    rg_w, qkv_w = rg_conv_w.shape[1], qkv_conv_w.shape[1]
    assert n % (4 * LANES) == 0 and n < w_in.shape[1] <= n + LANES
    assert m % tm == 0 and tm % n_sub == 0 and seq % tm == 0
    assert rg_w == GDN_HEADS * HEAD_DIM and qkv_w % rg_w == 0
    hb = tm // SUBLANES
    const_spec = lambda shape, **kw: pl.BlockSpec(shape, lambda i: (0, 0), **kw)
    return pl.pallas_call(
        functools.partial(_inproj_kernel, n_sub=n_sub, tiles_per_seq=seq // tm, rg_w=rg_w, qkv_w=qkv_w,
                          n_qk=2 * GDN_HEADS, q_scale=HEAD_DIM ** -0.5),
        grid=(m // tm,),
        in_specs=[
            pl.BlockSpec((tm, d), lambda i: (i, 0)),
            pl.BlockSpec((SUBLANES, d), lambda i: (jnp.maximum(i * hb - 1, 0), 0)),
            pl.BlockSpec((SUBLANES, d), lambda i: (jnp.minimum((i + 1) * hb, m // SUBLANES - 1), 0)),
            const_spec((1, d)),
            const_spec(w_in.shape, pipeline_mode=pl.Buffered(1)),
            const_spec((4, rg_w)),
            const_spec((1, rg_w)),
            const_spec((4, qkv_w)),
        ],
        out_specs=[
            pl.BlockSpec((tm, n), lambda i: (i, 0)),
            pl.BlockSpec((tm, LANES), lambda i: (i, 0)),
        ],
        out_shape=[jax.ShapeDtypeStruct((m, n), F32), jax.ShapeDtypeStruct((m, LANES), F32)],
        scratch_shapes=[pltpu.VMEM(((rg_w + qkv_w) // LANES, tm // n_sub + 2 * SUBLANES, LANES), F32),
                        pltpu.VMEM((d, n + LANES), BF16)],
        compiler_params=_cparams(("arbitrary",)),
        name="in_proj",
    )(x, x, x, g.reshape(1, d), w_in, rg_conv_w, rg_conv_b.reshape(1, -1), qkv_conv_w)


RG_SEGS = 2 * SUBLANES
RG_PAD = 4


def _rglru_kernel(x_ref, gate_ref, wg_ref, bg_ref, lam_ref, o_ref,
                  af_scr, bf_scr, ab_scr, bb_scr, *, seq, rt):
    seg_len = seq // RG_SEGS
    pitch = seg_len + RG_PAD
    tiles_per_seg = seg_len // rt
    n_tiles = seq // rt
    a_scr = (af_scr, ab_scr)
    b_scr = (bf_scr, bb_scr)

    bg = bg_ref[...]
    half_log2_a = (-0.5 * RG_C * LOG2_E) * _softplus(-lam_ref[...])

    def gates(i, _):
        r0 = pl.multiple_of(i * rt, rt)
        xc = x_ref[pl.ds(r0, rt), :]
        pre = jnp.dot(xc.astype(BF16), wg_ref[...], preferred_element_type=F32) + bg
        hx = 0.5 * xc
        seg = i // tiles_per_seg
        off = seg * pitch + (i - seg * tiles_per_seg) * rt
        for d in range(2):
            tr = jnp.tanh(pre[:, (2 * d) * LANES:(2 * d + 1) * LANES])
            ti = jnp.tanh(pre[:, (2 * d + 1) * LANES:(2 * d + 2) * LANES])
            hl = half_log2_a[d:d + 1, :]
            a = jnp.exp2(hl + hl * tr)
            om = 1.0 - a * a
            b = jnp.where(om > 0.0, om * lax.rsqrt(om), 0.0) * (hx + hx * ti)
            a_scr[d][pl.ds(off, rt), :] = a
            b_scr[d][pl.ds(off, rt), :] = b
        return 0

    lax.fori_loop(0, n_tiles, gates, 0)

    def scan(t, carry):
        hf, pf, hb, pb = carry
        idx = pl.ds(t, RG_SEGS, stride=pitch)
        a = af_scr[idx, :]
        hf = a * hf + bf_scr[idx, :]
        pf = a * pf
        bf_scr[idx, :] = hf
        af_scr[idx, :] = pf
        idx = pl.ds(seg_len - 1 - t, RG_SEGS, stride=pitch)
        a = ab_scr[idx, :]
        hb = a * hb + bb_scr[idx, :]
        pb = a * pb
        bb_scr[idx, :] = hb
        ab_scr[idx, :] = pb
        return hf, pf, hb, pb

    zeros = jnp.zeros((RG_SEGS, LANES), F32)
    ones = jnp.ones((RG_SEGS, LANES), F32)
    hf, pf, hb, pb = lax.fori_loop(0, seg_len, scan, (zeros, ones, zeros, ones), unroll=8)

    cf = [jnp.zeros((1, LANES), F32)]
    for s in range(1, RG_SEGS):
        cf.append(hf[s - 1:s, :] + pf[s - 1:s, :] * cf[s - 1])
    cbk = [None] * RG_SEGS
    cbk[RG_SEGS - 1] = jnp.zeros((1, LANES), F32)
    for s in range(RG_SEGS - 2, -1, -1):
        cbk[s] = hb[s + 1:s + 2, :] + pb[s + 1:s + 2, :] * cbk[s + 1]

    for s in range(RG_SEGS):
        def fix(k, _, s=s):
            off = s * pitch + k * rt
            rows = pl.multiple_of(s * seg_len + k * rt, SUBLANES)
            h = bf_scr[pl.ds(off, rt), :] + af_scr[pl.ds(off, rt), :] * cf[s]
            h = h + (bb_scr[pl.ds(off, rt), :] + ab_scr[pl.ds(off, rt), :] * cbk[s])
            o_ref[pl.ds(rows, rt), :] = h * jax.nn.gelu(gate_ref[pl.ds(rows, rt), :], approximate=True)
            return 0

        lax.fori_loop(0, tiles_per_seg, fix, 0)


def _rglru(p, w_gates, b_gates, lam, *, batch, seq, gate_col0, rt=512):
    ngrp = w_gates.shape[0]
    seg_len = seq // RG_SEGS
    rt = min(rt, seg_len)
    assert seq % RG_SEGS == 0 and seg_len % rt == 0
    scr = pltpu.VMEM((RG_SEGS * (seg_len + RG_PAD), LANES), F32)
    return pl.pallas_call(
        functools.partial(_rglru_kernel, seq=seq, rt=rt),
        grid=(batch, ngrp),
        in_specs=[
            pl.BlockSpec((seq, LANES), lambda b, c: (b, c)),
            pl.BlockSpec((seq, LANES), lambda b, c: (b, gate_col0 + c)),
            pl.BlockSpec((None, LANES, 4 * LANES), lambda b, c: (c, 0, 0)),
            pl.BlockSpec((None, 1, 4 * LANES), lambda b, c: (c, 0, 0)),
            pl.BlockSpec((2, LANES), lambda b, c: (0, c)),
        ],
        out_specs=pl.BlockSpec((seq, LANES), lambda b, c: (b, c)),
        out_shape=jax.ShapeDtypeStruct((batch * seq, ngrp * LANES), F32),
        scratch_shapes=[scr, scr, scr, scr],
        compiler_params=_cparams(("parallel", "parallel")),
        name="rglru",
    )(p, p, w_gates, b_gates, lam)


def _rg_gate_weights(wa, ba, wx, bx):
    ndir, nblk, blk, _ = wa.shape
    ngrp = nblk * blk // LANES
    per = LANES // blk

    def bd(w):
        w = w.reshape(ndir, ngrp, per, blk, blk)
        eye = jnp.eye(per, dtype=w.dtype)
        return jnp.einsum("dgpij,pq->dgpiqj", w, eye).reshape(ndir, ngrp, LANES, LANES)

    a, x = bd(wa), bd(wx)
    w = jnp.concatenate([a[0], x[0], a[1], x[1]], axis=-1)
    ba = ba.reshape(ndir, ngrp, 1, LANES)
    bx = bx.reshape(ndir, ngrp, 1, LANES)
    b = jnp.concatenate([ba[0], bx[0], ba[1], bx[1]], axis=-1)
    return (0.5 * w).astype(BF16), (0.5 * b).astype(F32)


def _gate_kernel(ba_ref, alog_ref, dtb_ref, col_ref, row_ref, *, tm):
    nh = GDN_HEADS
    ri = lax.broadcasted_iota(jnp.int32, (LANES, LANES), 0)
    ci = lax.broadcasted_iota(jnp.int32, (LANES, LANES), 1)
    same = (ri // CHUNK) == (ci // CHUNK)
    lower = jnp.where(same & (ri >= ci), 1.0, 0.0).astype(BF16)
    upper = jnp.where(same & (ri <= ci), 1.0, 0.0).astype(BF16)
    masks = jnp.concatenate([lower, upper], axis=0)
    lane = ci
    neg_a = -jnp.exp(alog_ref[...])
    dtb = dtb_ref[...]
    for k in range(tm // LANES):
        sl = slice(k * LANES, (k + 1) * LANES)
        raw = ba_ref[sl, :]
        beta = _sigmoid(raw)
        gk = neg_a * _softplus(raw + dtb)
        gk = jnp.where((lane >= 2 * nh) & (lane < 4 * nh), gk, 0.0)
        csum = jnp.zeros((2 * LANES, LANES), F32)
        rest = gk
        for _ in range(3):
            piece = rest.astype(BF16)
            csum = csum + jnp.dot(masks, piece, preferred_element_type=F32)
            rest = rest - piece.astype(F32)
        gc = jnp.where(lane < 3 * nh, csum[:LANES], csum[LANES:])
        col = jnp.where(lane < 2 * nh, beta, gc)
        col_ref[sl, :] = col
        colt = col.T
        for q in range(LANES // CHUNK):
            rows = [jnp.concatenate([colt[(2 + dr) * nh + h:(2 + dr) * nh + h + 1, q * CHUNK:(q + 1) * CHUNK]
                                     for h in range(nh)], axis=1) for dr in range(2)]
            rows.append(jnp.zeros((SUBLANES - 2, nh * CHUNK), F32))
            row_ref[k * (LANES // CHUNK) + q] = jnp.concatenate(rows, axis=0)


def _gate_prep(ba, a_log, dt_bias, *, tm=2048):
    m = ba.shape[0]
    nh = GDN_HEADS
    pad = lambda v: jnp.zeros((1, LANES), F32).at[0, 2 * nh:4 * nh].set(v.reshape(-1).astype(F32))
    return pl.pallas_call(
        functools.partial(_gate_kernel, tm=tm),
        grid=(m // tm,),
        in_specs=[
            pl.BlockSpec((tm, LANES), lambda i: (i, 0)),
            pl.BlockSpec((1, LANES), lambda i: (0, 0)),
            pl.BlockSpec((1, LANES), lambda i: (0, 0)),
        ],
        out_specs=[
            pl.BlockSpec((tm, LANES), lambda i: (i, 0)),
            pl.BlockSpec((tm // CHUNK, SUBLANES, nh * CHUNK), lambda i: (i, 0, 0)),
        ],
        out_shape=[jax.ShapeDtypeStruct((m, LANES), F32),
                   jax.ShapeDtypeStruct((m // CHUNK, SUBLANES, nh * CHUNK), F32)],
        compiler_params=_cparams(("parallel",)),
        name="gate_prep",
    )(ba, pad(a_log), pad(dt_bias))


def _bmm(a, b):
    return lax.dot_general(a.astype(BF16), b.astype(BF16), (((2,), (1,)), ((0,), (0,))),
                           preferred_element_type=F32)


def _bmm_nt(a, b):
    return lax.dot_general(a.astype(BF16), b.astype(BF16), (((2,), (2,)), ((0,), (0,))),
                           preferred_element_type=F32)


def _bmm_tn(a, b):
    return lax.dot_general(a.astype(BF16), b.astype(BF16), (((1,), (1,)), ((0,), (0,))),
                           preferred_element_type=F32)


def _unit_tri_inverse_levels(lm, ri, ci, bdmask, nb):
    def blockdiag(a):
        ab = a.astype(BF16)
        return jnp.where(bdmask, jnp.concatenate([ab] * nb, axis=1), jnp.zeros((), BF16))

    eye = jnp.where(ri == ci, 1.0, 0.0).astype(F32)
    x = eye - jnp.where((ri // 2) == (ci // 2), lm, 0.0)
    s = 2
    while s < CHUNK:
        cm = jnp.where(((ri // (2 * s)) == (ci // (2 * s))) & ((ri // s) != (ci // s)), lm, 0.0)
        x = x - _bmm(_bmm(x, blockdiag(cm)), blockdiag(x))
        yield x
        s *= 2


def _gdn_chunk_local(q_ref, k_ref, v_ref, col_ref, row_ref, dst, *, reverse, g, d):
    u_scr, wq_scr, kd_scr, at_scr, cd_scr = dst
    nh, c, hd = GDN_HEADS, CHUNK, HEAD_DIM
    pw = nh * c
    fw = nh * hd
    ri = lax.broadcasted_iota(jnp.int32, (1, c, pw), 1)
    ci = lax.broadcasted_iota(jnp.int32, (1, c, pw), 2) % c
    if reverse:
        incl, strict, last = ri <= ci, ri < ci, 0
    else:
        incl, strict, last = ri >= ci, ri > ci, c - 1
    bdmask = (lax.broadcasted_iota(jnp.int32, (1, pw, pw), 1) // c
              == lax.broadcasted_iota(jnp.int32, (1, pw, pw), 2) // c)
    kmask = (lax.broadcasted_iota(jnp.int32, (1, pw, fw), 1) // c
             == lax.broadcasted_iota(jnp.int32, (1, pw, fw), 2) // hd)
    low_half = lax.broadcasted_iota(jnp.int32, (g * c, hd), 1) < c

    col = col_ref[...]
    bcast = lambda j: jnp.broadcast_to(col[:, j:j + 1], (g * c, hd))
    beta = jnp.concatenate([bcast(d * nh + h) for h in range(nh)], axis=1).reshape(g, c, fw)
    gcs = [bcast((2 + d) * nh + h) for h in range(nh)]
    gc = jnp.concatenate(gcs, axis=1).reshape(g, c, fw)
    gc_col = jnp.concatenate([jnp.where(low_half, gcs[h], gcs[h + 1]) for h in range(0, nh, 2)],
                             axis=1).reshape(g, c, pw)
    gc_row = row_ref[:, d:d + 1, :]
    g_last = gc[:, last:last + 1, :]
    eg = jnp.exp(gc)
    decay = jnp.where(incl, jnp.exp(gc_col - gc_row), 0.0)
    q = q_ref[...].reshape(g, c, fw)
    k = k_ref[...].reshape(g, c, fw)
    v = v_ref[...].reshape(g, c, fw)
    k_beta = k * beta
    v_beta = v * beta
    kb16 = k.astype(BF16)
    k_bd = jnp.where(kmask, jnp.concatenate([kb16] * nh, axis=1), jnp.zeros((), BF16))
    kq = _bmm_nt(jnp.concatenate([k_beta, q], axis=1), k_bd)
    lm = jnp.where(strict, kq[:, :c] * decay, 0.0)
    attn = kq[:, c:] * decay
    yield
    for t in _unit_tri_inverse_levels(lm, ri, ci, bdmask, nh):
        yield
    kbe = k_beta * eg
    qd = q * eg
    kd = k * jnp.exp(g_last - gc)
    cd = jnp.exp(g_last)
    for h in range(nh):
        hs = slice(h * hd, (h + 1) * hd)
        ps = slice(h * c, (h + 1) * c)
        uw = _bmm(t[:, :, ps], jnp.concatenate([v_beta[:, :, hs], kbe[:, :, hs]], axis=-1))
        u_scr[h] = uw[:, :, :hd]
        wq_scr[h] = jnp.concatenate([uw[:, :, hd:], qd[:, :, hs]], axis=1).astype(BF16)
        at_scr[h] = attn[:, :, ps].astype(BF16)
        kd_scr[h] = kd[:, :, hs].astype(BF16)
        cd_scr[h] = cd[:, :, hs]
        if h % 2 == 1:
            yield


def _gdn_recurrence(src, s_scr, o_ref, *, reverse, g):
    u_scr, wq_scr, kd_scr, at_scr, cd_scr = src
    nh, c, hd = GDN_HEADS, CHUNK, HEAD_DIM
    for step in range(g):
        n = g - 1 - step if reverse else step
        state = s_scr[...]
        sb = state.astype(BF16)
        ws = _bmm(wq_scr[:, n], sb)
        v_new = u_scr[:, n] - ws[:, :c]
        vb = v_new.astype(BF16)
        o = ws[:, c:] + _bmm(at_scr[:, n], vb)
        s_scr[...] = state * cd_scr[:, n] + _bmm_tn(kd_scr[:, n], vb)
        for h in range(nh):
            o_ref[n * c:(n + 1) * c, h * hd:(h + 1) * hd] = o[h]
        yield


def _gdn_kernel(q_ref, k_ref, v_ref, col_ref, row_ref, o_ref, s_scr, *sets, reverse, n_chunks, d):
    set0, set1 = sets[:len(sets) // 2], sets[len(sets) // 2:]
    i = pl.program_id(1)

    @pl.when(i == 0)
    def _():
        s_scr[...] = jnp.zeros_like(s_scr)
        for r in set1:
            r[...] = jnp.zeros_like(r)

    def step(dst, src):
        local = _gdn_chunk_local(q_ref, k_ref, v_ref, col_ref, row_ref, dst, reverse=reverse, g=n_chunks, d=d)
        recur = _gdn_recurrence(src, s_scr, o_ref, reverse=reverse, g=n_chunks)
        for _ in itertools.zip_longest(local, recur):
            pass

    @pl.when(i % 2 == 0)
    def _():
        step(set0, set1)

    @pl.when(i % 2 == 1)
    def _():
        step(set1, set0)


def _gdn(p, col, row, *, q_col, batch, seq, d, tt=512):
    nh = GDN_HEADS
    width = nh * HEAD_DIM
    nt = seq // tt
    n_chunks = tt // CHUNK
    reverse = d == 1
    order = (lambda t: nt - 1 - t) if reverse else (lambda t: t)
    tile_in = lambda b, i: b * nt + order(jnp.minimum(i, nt - 1))
    tile_out = lambda b, i: b * nt + order(jnp.maximum(i - 1, 0))
    scratch_set = [
        pltpu.VMEM((nh, n_chunks, CHUNK, HEAD_DIM), F32),
        pltpu.VMEM((nh, n_chunks, 2 * CHUNK, HEAD_DIM), BF16),
        pltpu.VMEM((nh, n_chunks, CHUNK, HEAD_DIM), BF16),
        pltpu.VMEM((nh, n_chunks, CHUNK, CHUNK), BF16),
        pltpu.VMEM((nh, n_chunks, 1, HEAD_DIM), F32),
    ]
    return pl.pallas_call(
        functools.partial(_gdn_kernel, reverse=reverse, n_chunks=n_chunks, d=d),
        grid=(batch, nt + 1),
        in_specs=[
            pl.BlockSpec((tt, width), lambda b, i: (tile_in(b, i), q_col)),
            pl.BlockSpec((tt, width), lambda b, i: (tile_in(b, i), q_col + 1)),
            pl.BlockSpec((tt, width), lambda b, i: (tile_in(b, i), q_col + 2)),
            pl.BlockSpec((tt, LANES), lambda b, i: (tile_in(b, i), 0)),
            pl.BlockSpec((n_chunks, SUBLANES, nh * CHUNK), lambda b, i: (tile_in(b, i), 0, 0)),
        ],
        out_specs=pl.BlockSpec((tt, width), lambda b, i: (tile_out(b, i), 0)),
        out_shape=jax.ShapeDtypeStruct((batch * seq, width), F32),
        scratch_shapes=[pltpu.VMEM((nh, HEAD_DIM, HEAD_DIM), F32)] + scratch_set + scratch_set,
        compiler_params=_cparams(("parallel", "arbitrary")),
        name="gdn_bwd" if reverse else "gdn_fwd",
    )(p, p, p, col, row)


def _mixer(x1, batch, seq, mix_norm, w_in, w_out, rg_conv_w, rg_conv_b, rg_gate_a_w, rg_gate_a_b,
           rg_gate_x_w, rg_gate_x_b, rg_lambda, gdn_conv_w, gdn_a_log, gdn_dt_bias, gdn_norm):
    rg_w = rg_conv_w.shape[1]
    qkv_w = gdn_conv_w.shape[1]
    gdn_vw = GDN_HEADS * HEAD_DIM
    n_main = 2 * rg_w + qkv_w + gdn_vw
    p, ba = _in_proj(x1, mix_norm, w_in, rg_conv_w, rg_conv_b, gdn_conv_w, n=n_main, seq=seq)

    w_gates, b_gates = _rg_gate_weights(rg_gate_a_w, rg_gate_a_b, rg_gate_x_w, rg_gate_x_b)
    y_rg = _rglru(p, w_gates, b_gates, rg_lambda.astype(F32), batch=batch, seq=seq, gate_col0=rg_w // LANES)

    col, row = _gate_prep(ba, gdn_a_log, gdn_dt_bias)
    q_col = 2 * rg_w // gdn_vw
    o_f = _gdn(p, col, row, q_col=q_col, batch=batch, seq=seq, d=0)
    o_b = _gdn(p, col, row, q_col=q_col, batch=batch, seq=seq, d=1)
    return y_rg, o_f, o_b, p, (2 * rg_w + qkv_w) // gdn_vw, gdn_norm, w_out.astype(BF16)


def kernel(x, ffn1_norm, ffn1_w_gate, ffn1_w_up, ffn1_w_down, mix_norm, w_in, w_out, rg_conv_w, rg_conv_b, rg_gate_a_w, rg_gate_a_b, rg_gate_x_w, rg_gate_x_b, rg_lambda, gdn_conv_w, gdn_a_log, gdn_dt_bias, gdn_norm, ffn2_norm, ffn2_w_gate, ffn2_w_up, ffn2_w_down, final_norm):
    batch, seq, d_model = x.shape
    depth = ffn1_norm.shape[0]
    h = x.reshape(batch * seq, d_model)
    for l in range(depth):
        last = l == depth - 1
        h = _ffn(h, ffn1_norm[l], ffn1_w_gate[l], ffn1_w_up[l], ffn1_w_down[l], final_norm, final_norm=False)
        mixer = _mixer(h, batch, seq, mix_norm[l], w_in[l], w_out[l], rg_conv_w[l], rg_conv_b[l],
                   rg_gate_a_w[l], rg_gate_a_b[l], rg_gate_x_w[l], rg_gate_x_b[l], rg_lambda[l],
                   gdn_conv_w[l], gdn_a_log[l], gdn_dt_bias[l], gdn_norm[l])
        h = _ffn(h, ffn2_norm[l], ffn2_w_gate[l], ffn2_w_up[l], ffn2_w_down[l], final_norm,
                 final_norm=last, mixer=mixer)
    return h.reshape(batch, seq, d_model)
```

```python
import functools
import itertools

import jax
import jax.numpy as jnp
from jax import lax
from jax.experimental import pallas as pl
from jax.experimental.pallas import tpu as pltpu

F32 = jnp.float32
BF16 = jnp.bfloat16

EPS = 1e-6
LOG2_E = 1.4426950408889634
RG_C = 8.0
RG_BLOCKS = 8
GDN_HEADS = 4
HEAD_DIM = 128
CHUNK = 64
LANES = 128
SUBLANES = 8
VMEM_LIMIT = 56 * 1024 * 1024


def _cparams(sem):
    return pltpu.CompilerParams(dimension_semantics=sem, vmem_limit_bytes=VMEM_LIMIT)


def _rms(x):
    return x * lax.rsqrt(jnp.mean(x * x, axis=-1, keepdims=True) + EPS)


def _sigmoid(x):
    return 1.0 / (1.0 + jnp.exp(-x))


def _softplus(x):
    return jnp.maximum(x, 0.0) + jnp.log(1.0 + jnp.exp(-jnp.abs(x)))


def _gated_mixer_rows(yrg_ref, of_ref, ob_ref, z_ref, gn_ref, wo_ref, rows):
    o = of_ref[rows, :] + ob_ref[rows, :]
    z = z_ref[rows, :]
    gn = gn_ref[...]
    parts = [yrg_ref[rows, :].astype(BF16)]
    for h in range(GDN_HEADS):
        hs = slice(h * HEAD_DIM, (h + 1) * HEAD_DIM)
        zh = z[:, hs]
        parts.append((_rms(o[:, hs]) * gn * (zh * _sigmoid(zh))).astype(BF16))
    return jnp.dot(jnp.concatenate(parts, axis=-1), wo_ref[...], preferred_element_type=F32)


W_CHUNKS = 8


def _stage_weight_bf16(w_hbm, w_scr, stage, sem):
    rows = w_hbm.shape[0] // W_CHUNKS
    copy = lambda k: pltpu.make_async_copy(w_hbm.at[pl.ds(k * rows, rows), :], stage.at[k % 2], sem.at[k % 2])
    copy(0).start()
    for k in range(W_CHUNKS):
        if k + 1 < W_CHUNKS:
            copy(k + 1).start()
        copy(k).wait()
        w_scr[k * rows:(k + 1) * rows, :] = stage[k % 2].astype(BF16)


def _ffn_kernel(*refs, final_norm, n_sub, with_mixer):
    refs, (wg_ref, wu_ref, wd_ref, stage_in, stage_dn, sem) = refs[:-6], refs[-6:]
    if with_mixer:
        x_ref, mixer_refs, (g_ref, wg_hbm, wu_hbm, wd_hbm, fg_ref, o_ref) = refs[0], refs[1:7], refs[7:]
    else:
        x_ref, g_ref, wg_hbm, wu_hbm, wd_hbm, fg_ref, o_ref = refs

    @pl.when(pl.program_id(0) == 0)
    def _():
        _stage_weight_bf16(wg_hbm, wg_ref, stage_in, sem)
        _stage_weight_bf16(wu_hbm, wu_ref, stage_in, sem)
        _stage_weight_bf16(wd_hbm, wd_ref, stage_dn, sem)

    sub = x_ref.shape[0] // n_sub
    g = g_ref[...]
    for r in range(n_sub):
        rows = slice(r * sub, (r + 1) * sub)
        x = x_ref[rows, :]
        if with_mixer:
            x = x + _gated_mixer_rows(*mixer_refs, rows)
        h = (_rms(x) * g).astype(BF16)
        gate = jnp.dot(h, wg_ref[...], preferred_element_type=F32)
        up = jnp.dot(h, wu_ref[...], preferred_element_type=F32)
        act = (gate * _sigmoid(gate) * up).astype(BF16)
        y = x + 0.5 * jnp.dot(act, wd_ref[...], preferred_element_type=F32)
        if final_norm:
            y = _rms(y) * fg_ref[...]
        o_ref[rows, :] = y


def _ffn(x, g, wg, wu, wd, fg, *, final_norm, mixer=None, tm=512, n_sub=2):
    m, d = x.shape
    f = wg.shape[1]
    assert m % tm == 0 and tm % n_sub == 0
    resident = dict(pipeline_mode=pl.Buffered(1))
    row_spec = lambda w, col=0: pl.BlockSpec((tm, w), lambda i: (i, col))
    const_spec = lambda shape, **kw: pl.BlockSpec(shape, lambda i: (0, 0), **kw)
    operands, in_specs = [x], [row_spec(d)]
    if mixer is not None:
        y_rg, o_f, o_b, p, z_col, gdn_norm, w_out = mixer
        wr, wv = y_rg.shape[1], o_f.shape[1]
        operands += [y_rg, o_f, o_b, p, gdn_norm.reshape(1, -1), w_out]
        in_specs += [row_spec(wr), row_spec(wv), row_spec(wv), row_spec(wv, z_col),
                     const_spec((1, HEAD_DIM)), const_spec((wr + wv, d), **resident)]
    assert d % W_CHUNKS == 0 and f % (W_CHUNKS * 2 * SUBLANES) == 0
    in_hbm = pl.BlockSpec(memory_space=pl.ANY)
    operands += [g.reshape(1, d), wg, wu, wd, fg.reshape(1, d)]
    in_specs += [const_spec((1, d)), in_hbm, in_hbm, in_hbm, const_spec((1, d))]
    return pl.pallas_call(
        functools.partial(_ffn_kernel, final_norm=final_norm, n_sub=n_sub, with_mixer=mixer is not None),
        grid=(m // tm,),
        in_specs=in_specs,
        out_specs=row_spec(d),
        out_shape=jax.ShapeDtypeStruct((m, d), F32),
        scratch_shapes=[
            pltpu.VMEM((d, f), BF16), pltpu.VMEM((d, f), BF16), pltpu.VMEM((f, d), BF16),
            pltpu.VMEM((2, d // W_CHUNKS, f), F32), pltpu.VMEM((2, f // W_CHUNKS, d), F32),
            pltpu.SemaphoreType.DMA((2,)),
        ],
        compiler_params=_cparams(("arbitrary",)),
        name="ffn_final" if final_norm else "ffn",
    )(*operands)


def _inproj_kernel(x_ref, xp_ref, xn_ref, g_ref, w32_ref, cwr_ref, cbr_ref, cwq_ref, p_ref, ba_ref,
                   stage_scr, w_ref, *, n_sub, tiles_per_seq, rg_w, qkv_w, n_qk, q_scale):
    tm, n = p_ref.shape
    sub = tm // n_sub
    hal = SUBLANES
    i = pl.program_id(0)

    @pl.when(i == 0)
    def _():
        n_in = w32_ref.shape[0]
        for c in range(0, n, LANES):
            w_ref[:, c:c + LANES] = w32_ref[c:c + LANES, :].T.astype(BF16)
        tail = w32_ref[n_in - LANES:n_in, :].T
        w_ref[:, n:] = jnp.zeros((w_ref.shape[0], LANES), BF16)
        w_ref[:, n:n_in] = tail[:, LANES - (n_in - n):].astype(BF16)

    pos = i % tiles_per_seq
    prev = jnp.where(pos == 0, 0.0, xp_ref[...])
    nxt = jnp.where(pos == tiles_per_seq - 1, 0.0, xn_ref[...])
    g = g_ref[...]
    cwr = cwr_ref[...]
    mid = slice(hal, hal + sub)
    blk = GDN_HEADS * HEAD_DIM

    def stage(slot, pw):
        for j in range(blk // LANES):
            stage_scr[slot * (blk // LANES) + j] = pw[:, j * LANES:(j + 1) * LANES]

    def conv(slab, cw):
        tap = lambda off: stage_scr[slab, hal + off:hal + off + sub, :]
        acc = cw[0:1, :] * tap(-2) + cw[1:2, :] * tap(-1)
        return acc + cw[2:3, :] * tap(0) + cw[3:4, :] * tap(1)

    for r in range(n_sub):
        lo, hi = r * sub - hal, (r + 1) * sub + hal
        parts = ([prev] if lo < 0 else []) + [x_ref[max(lo, 0):min(hi, tm), :]] + ([nxt] if hi > tm else [])
        xw = jnp.concatenate(parts, axis=0)
        h = (_rms(xw) * g).astype(BF16)
        rows = slice(r * sub, (r + 1) * sub)
        proj = lambda c0, c1: jnp.dot(h, w_ref[:, c0:c1], preferred_element_type=F32)
        stage(0, proj(0, rg_w))
        for c in range(0, rg_w, LANES):
            p_ref[rows, c:c + LANES] = conv(c // LANES, cwr[:, c:c + LANES]) + cbr_ref[:, c:c + LANES]
        p_ref[rows, rg_w:2 * rg_w] = proj(rg_w, 2 * rg_w)[mid]
        for c0 in range(0, qkv_w, blk):
            slot = 1 + c0 // blk
            stage(slot, proj(2 * rg_w + c0, 2 * rg_w + c0 + blk))
            for c in range(c0, c0 + blk, LANES):
                y = conv((blk + c) // LANES, cwq_ref[:, c:c + LANES])
                y = y * _sigmoid(y)
                if c < n_qk * LANES:
                    scale = q_scale if c < n_qk * LANES // 2 else 1.0
                    y = y * (lax.rsqrt(jnp.sum(y * y, axis=-1, keepdims=True) + EPS) * scale)
                p_ref[rows, 2 * rg_w + c:2 * rg_w + c + LANES] = y
        tail = proj(2 * rg_w + qkv_w, n + LANES)[mid]
        p_ref[rows, 2 * rg_w + qkv_w:n] = tail[:, :n - 2 * rg_w - qkv_w]
        ba_ref[rows, :] = tail[:, n - 2 * rg_w - qkv_w:]


def _in_proj(x, g, w_in_t, rg_conv_w, rg_conv_b, qkv_conv_w, *, n, seq, tm=512, n_sub=2):
    m, d = x.shape
    rg_w, qkv_w = rg_conv_w.shape[1], qkv_conv_w.shape[1]
    assert n % LANES == 0 and n < w_in_t.shape[0] <= n + LANES and w_in_t.shape[0] % SUBLANES == 0
    assert m % tm == 0 and tm % n_sub == 0 and seq % tm == 0
    assert rg_w == GDN_HEADS * HEAD_DIM and qkv_w % rg_w == 0
    hb = tm // SUBLANES
    const_spec = lambda shape, **kw: pl.BlockSpec(shape, lambda i: (0, 0), **kw)
    return pl.pallas_call(
        functools.partial(_inproj_kernel, n_sub=n_sub, tiles_per_seq=seq // tm, rg_w=rg_w, qkv_w=qkv_w,
                          n_qk=2 * GDN_HEADS, q_scale=HEAD_DIM ** -0.5),
        grid=(m // tm,),
        in_specs=[
            pl.BlockSpec((tm, d), lambda i: (i, 0)),
            pl.BlockSpec((SUBLANES, d), lambda i: (jnp.maximum(i * hb - 1, 0), 0)),
            pl.BlockSpec((SUBLANES, d), lambda i: (jnp.minimum((i + 1) * hb, m // SUBLANES - 1), 0)),
            const_spec((1, d)),
            const_spec(w_in_t.shape, pipeline_mode=pl.Buffered(1)),
            const_spec((4, rg_w)),
            const_spec((1, rg_w)),
            const_spec((4, qkv_w)),
        ],
        out_specs=[
            pl.BlockSpec((tm, n), lambda i: (i, 0)),
            pl.BlockSpec((tm, LANES), lambda i: (i, 0)),
        ],
        out_shape=[jax.ShapeDtypeStruct((m, n), F32), jax.ShapeDtypeStruct((m, LANES), F32)],
        scratch_shapes=[pltpu.VMEM(((rg_w + qkv_w) // LANES, tm // n_sub + 2 * SUBLANES, LANES), F32),
                        pltpu.VMEM((d, n + LANES), BF16)],
        compiler_params=_cparams(("arbitrary",)),
        name="in_proj",
    )(x, x, x, g.reshape(1, d), w_in_t, rg_conv_w, rg_conv_b.reshape(1, -1), qkv_conv_w)


RG_SEGS = 2 * SUBLANES
RG_PAD = 4


def _rglru_kernel(x_ref, gate_ref, wg_ref, bg_ref, lam_ref, o_ref,
                  af_scr, bf_scr, ab_scr, bb_scr, *, seq, rt):
    seg_len = seq // RG_SEGS
    pitch = seg_len + RG_PAD
    tiles_per_seg = seg_len // rt
    n_tiles = seq // rt
    a_scr = (af_scr, ab_scr)
    b_scr = (bf_scr, bb_scr)

    bg = bg_ref[...]
    half_log2_a = (-0.5 * RG_C * LOG2_E) * _softplus(-lam_ref[...])

    def gates(i, _):
        r0 = pl.multiple_of(i * rt, rt)
        xc = x_ref[pl.ds(r0, rt), :]
        pre = jnp.dot(xc.astype(BF16), wg_ref[...], preferred_element_type=F32) + bg
        hx = 0.5 * xc
        seg = i // tiles_per_seg
        off = seg * pitch + (i - seg * tiles_per_seg) * rt
        for d in range(2):
            tr = jnp.tanh(pre[:, (2 * d) * LANES:(2 * d + 1) * LANES])
            ti = jnp.tanh(pre[:, (2 * d + 1) * LANES:(2 * d + 2) * LANES])
            hl = half_log2_a[d:d + 1, :]
            a = jnp.exp2(hl + hl * tr)
            om = 1.0 - a * a
            b = jnp.where(om > 0.0, om * lax.rsqrt(om), 0.0) * (hx + hx * ti)
            a_scr[d][pl.ds(off, rt), :] = a
            b_scr[d][pl.ds(off, rt), :] = b
        return 0

    lax.fori_loop(0, n_tiles, gates, 0)

    def scan(t, carry):
        hf, pf, hb, pb = carry
        idx = pl.ds(t, RG_SEGS, stride=pitch)
        a = af_scr[idx, :]
        hf = a * hf + bf_scr[idx, :]
        pf = a * pf
        bf_scr[idx, :] = hf
        af_scr[idx, :] = pf
        idx = pl.ds(seg_len - 1 - t, RG_SEGS, stride=pitch)
        a = ab_scr[idx, :]
        hb = a * hb + bb_scr[idx, :]
        pb = a * pb
        bb_scr[idx, :] = hb
        ab_scr[idx, :] = pb
        return hf, pf, hb, pb

    zeros = jnp.zeros((RG_SEGS, LANES), F32)
    ones = jnp.ones((RG_SEGS, LANES), F32)
    hf, pf, hb, pb = lax.fori_loop(0, seg_len, scan, (zeros, ones, zeros, ones), unroll=8)

    cf = [jnp.zeros((1, LANES), F32)]
    for s in range(1, RG_SEGS):
        cf.append(hf[s - 1:s, :] + pf[s - 1:s, :] * cf[s - 1])
    cbk = [None] * RG_SEGS
    cbk[RG_SEGS - 1] = jnp.zeros((1, LANES), F32)
    for s in range(RG_SEGS - 2, -1, -1):
        cbk[s] = hb[s + 1:s + 2, :] + pb[s + 1:s + 2, :] * cbk[s + 1]

    for s in range(RG_SEGS):
        def fix(k, _, s=s):
            off = s * pitch + k * rt
            rows = pl.multiple_of(s * seg_len + k * rt, SUBLANES)
            h = bf_scr[pl.ds(off, rt), :] + af_scr[pl.ds(off, rt), :] * cf[s]
            h = h + (bb_scr[pl.ds(off, rt), :] + ab_scr[pl.ds(off, rt), :] * cbk[s])
            o_ref[pl.ds(rows, rt), :] = h * jax.nn.gelu(gate_ref[pl.ds(rows, rt), :], approximate=True)
            return 0

        lax.fori_loop(0, tiles_per_seg, fix, 0)


def _rglru(p, w_gates, b_gates, lam, *, batch, seq, gate_col0, rt=512):
    ngrp = w_gates.shape[0]
    seg_len = seq // RG_SEGS
    rt = min(rt, seg_len)
    assert seq % RG_SEGS == 0 and seg_len % rt == 0
    scr = pltpu.VMEM((RG_SEGS * (seg_len + RG_PAD), LANES), F32)
    return pl.pallas_call(
        functools.partial(_rglru_kernel, seq=seq, rt=rt),
        grid=(batch, ngrp),
        in_specs=[
            pl.BlockSpec((seq, LANES), lambda b, c: (b, c)),
            pl.BlockSpec((seq, LANES), lambda b, c: (b, gate_col0 + c)),
            pl.BlockSpec((None, LANES, 4 * LANES), lambda b, c: (c, 0, 0)),
            pl.BlockSpec((None, 1, 4 * LANES), lambda b, c: (c, 0, 0)),
            pl.BlockSpec((2, LANES), lambda b, c: (0, c)),
        ],
        out_specs=pl.BlockSpec((seq, LANES), lambda b, c: (b, c)),
        out_shape=jax.ShapeDtypeStruct((batch * seq, ngrp * LANES), F32),
        scratch_shapes=[scr, scr, scr, scr],
        compiler_params=_cparams(("parallel", "parallel")),
        name="rglru",
    )(p, p, w_gates, b_gates, lam)


def _rg_gate_weights(wa, ba, wx, bx):
    ndir, nblk, blk, _ = wa.shape
    ngrp = nblk * blk // LANES
    per = LANES // blk

    def bd(w):
        w = w.reshape(ndir, ngrp, per, blk, blk)
        eye = jnp.eye(per, dtype=w.dtype)
        return jnp.einsum("dgpij,pq->dgpiqj", w, eye).reshape(ndir, ngrp, LANES, LANES)

    a, x = bd(wa), bd(wx)
    w = jnp.concatenate([a[0], x[0], a[1], x[1]], axis=-1)
    ba = ba.reshape(ndir, ngrp, 1, LANES)
    bx = bx.reshape(ndir, ngrp, 1, LANES)
    b = jnp.concatenate([ba[0], bx[0], ba[1], bx[1]], axis=-1)
    return (0.5 * w).astype(BF16), (0.5 * b).astype(F32)


def _gate_kernel(ba_ref, alog_ref, dtb_ref, col_ref, row_ref, *, tm):
    nh = GDN_HEADS
    ri = lax.broadcasted_iota(jnp.int32, (LANES, LANES), 0)
    ci = lax.broadcasted_iota(jnp.int32, (LANES, LANES), 1)
    same = (ri // CHUNK) == (ci // CHUNK)
    lower = jnp.where(same & (ri >= ci), 1.0, 0.0).astype(BF16)
    upper = jnp.where(same & (ri <= ci), 1.0, 0.0).astype(BF16)
    masks = jnp.concatenate([lower, upper], axis=0)
    lane = ci
    neg_a = -jnp.exp(alog_ref[...])
    dtb = dtb_ref[...]
    for k in range(tm // LANES):
        sl = slice(k * LANES, (k + 1) * LANES)
        raw = ba_ref[sl, :]
        beta = _sigmoid(raw)
        gk = neg_a * _softplus(raw + dtb)
        gk = jnp.where((lane >= 2 * nh) & (lane < 4 * nh), gk, 0.0)
        csum = jnp.zeros((2 * LANES, LANES), F32)
        rest = gk
        for _ in range(3):
            piece = rest.astype(BF16)
            csum = csum + jnp.dot(masks, piece, preferred_element_type=F32)
            rest = rest - piece.astype(F32)
        gc = jnp.where(lane < 3 * nh, csum[:LANES], csum[LANES:])
        col = jnp.where(lane < 2 * nh, beta, gc)
        col_ref[sl, :] = col
        colt = col.T
        for q in range(LANES // CHUNK):
            rows = [jnp.concatenate([colt[(2 + dr) * nh + h:(2 + dr) * nh + h + 1, q * CHUNK:(q + 1) * CHUNK]
                                     for h in range(nh)], axis=1) for dr in range(2)]
            rows.append(jnp.zeros((SUBLANES - 2, nh * CHUNK), F32))
            row_ref[k * (LANES // CHUNK) + q] = jnp.concatenate(rows, axis=0)


def _gate_prep(ba, a_log, dt_bias, *, tm=2048):
    m = ba.shape[0]
    nh = GDN_HEADS
    pad = lambda v: jnp.zeros((1, LANES), F32).at[0, 2 * nh:4 * nh].set(v.reshape(-1).astype(F32))
    return pl.pallas_call(
        functools.partial(_gate_kernel, tm=tm),
        grid=(m // tm,),
        in_specs=[
            pl.BlockSpec((tm, LANES), lambda i: (i, 0)),
            pl.BlockSpec((1, LANES), lambda i: (0, 0)),
            pl.BlockSpec((1, LANES), lambda i: (0, 0)),
        ],
        out_specs=[
            pl.BlockSpec((tm, LANES), lambda i: (i, 0)),
            pl.BlockSpec((tm // CHUNK, SUBLANES, nh * CHUNK), lambda i: (i, 0, 0)),
        ],
        out_shape=[jax.ShapeDtypeStruct((m, LANES), F32),
                   jax.ShapeDtypeStruct((m // CHUNK, SUBLANES, nh * CHUNK), F32)],
        compiler_params=_cparams(("parallel",)),
        name="gate_prep",
    )(ba, pad(a_log), pad(dt_bias))


def _bmm(a, b):
    return lax.dot_general(a.astype(BF16), b.astype(BF16), (((2,), (1,)), ((0,), (0,))),
                           preferred_element_type=F32)


def _bmm_nt(a, b):
    return lax.dot_general(a.astype(BF16), b.astype(BF16), (((2,), (2,)), ((0,), (0,))),
                           preferred_element_type=F32)


def _bmm_tn(a, b):
    return lax.dot_general(a.astype(BF16), b.astype(BF16), (((1,), (1,)), ((0,), (0,))),
                           preferred_element_type=F32)


def _unit_tri_inverse_levels(lm, ri, ci, bdmask, nb):
    def blockdiag(a):
        ab = a.astype(BF16)
        return jnp.where(bdmask, jnp.concatenate([ab] * nb, axis=1), jnp.zeros((), BF16))

    eye = jnp.where(ri == ci, 1.0, 0.0).astype(F32)
    x = eye - jnp.where((ri // 2) == (ci // 2), lm, 0.0)
    s = 2
    while s < CHUNK:
        cm = jnp.where(((ri // (2 * s)) == (ci // (2 * s))) & ((ri // s) != (ci // s)), lm, 0.0)
        x = x - _bmm(_bmm(x, blockdiag(cm)), blockdiag(x))
        yield x
        s *= 2


def _gdn_chunk_local(q_ref, k_ref, v_ref, col_ref, row_ref, dst, *, reverse, g, d):
    u_scr, wq_scr, kd_scr, at_scr, cd_scr = dst
    nh, c, hd = GDN_HEADS, CHUNK, HEAD_DIM
    pw = nh * c
    fw = nh * hd
    ri = lax.broadcasted_iota(jnp.int32, (1, c, pw), 1)
    ci = lax.broadcasted_iota(jnp.int32, (1, c, pw), 2) % c
    if reverse:
        incl, strict, last = ri <= ci, ri < ci, 0
    else:
        incl, strict, last = ri >= ci, ri > ci, c - 1
    bdmask = (lax.broadcasted_iota(jnp.int32, (1, pw, pw), 1) // c
              == lax.broadcasted_iota(jnp.int32, (1, pw, pw), 2) // c)
    kmask = (lax.broadcasted_iota(jnp.int32, (1, pw, fw), 1) // c
             == lax.broadcasted_iota(jnp.int32, (1, pw, fw), 2) // hd)
    low_half = lax.broadcasted_iota(jnp.int32, (g * c, hd), 1) < c

    col = col_ref[...]
    bcast = lambda j: jnp.broadcast_to(col[:, j:j + 1], (g * c, hd))
    beta = jnp.concatenate([bcast(d * nh + h) for h in range(nh)], axis=1).reshape(g, c, fw)
    gcs = [bcast((2 + d) * nh + h) for h in range(nh)]
    gc = jnp.concatenate(gcs, axis=1).reshape(g, c, fw)
    gc_col = jnp.concatenate([jnp.where(low_half, gcs[h], gcs[h + 1]) for h in range(0, nh, 2)],
                             axis=1).reshape(g, c, pw)
    gc_row = row_ref[:, d:d + 1, :]
    g_last = gc[:, last:last + 1, :]
    eg = jnp.exp(gc)
    decay = jnp.where(incl, jnp.exp(gc_col - gc_row), 0.0)
    q = q_ref[...].reshape(g, c, fw)
    k = k_ref[...].reshape(g, c, fw)
    v = v_ref[...].reshape(g, c, fw)
    k_beta = k * beta
    v_beta = v * beta
    kb16 = k.astype(BF16)
    k_bd = jnp.where(kmask, jnp.concatenate([kb16] * nh, axis=1), jnp.zeros((), BF16))
    kq = _bmm_nt(jnp.concatenate([k_beta, q], axis=1), k_bd)
    lm = jnp.where(strict, kq[:, :c] * decay, 0.0)
    attn = kq[:, c:] * decay
    yield
    for t in _unit_tri_inverse_levels(lm, ri, ci, bdmask, nh):
        yield
    kbe = k_beta * eg
    qd = q * eg
    kd = k * jnp.exp(g_last - gc)
    cd = jnp.exp(g_last)
    for h in range(nh):
        hs = slice(h * hd, (h + 1) * hd)
        ps = slice(h * c, (h + 1) * c)
        uw = _bmm(t[:, :, ps], jnp.concatenate([v_beta[:, :, hs], kbe[:, :, hs]], axis=-1))
        u_scr[h] = uw[:, :, :hd]
        wq_scr[h] = jnp.concatenate([uw[:, :, hd:], qd[:, :, hs]], axis=1).astype(BF16)
        at_scr[h] = attn[:, :, ps].astype(BF16)
        kd_scr[h] = kd[:, :, hs].astype(BF16)
        cd_scr[h] = cd[:, :, hs]
        if h % 2 == 1:
            yield


def _gdn_recurrence(src, s_scr, o_ref, *, reverse, g):
    u_scr, wq_scr, kd_scr, at_scr, cd_scr = src
    nh, c, hd = GDN_HEADS, CHUNK, HEAD_DIM
    for step in range(g):
        n = g - 1 - step if reverse else step
        state = s_scr[...]
        sb = state.astype(BF16)
        ws = _bmm(wq_scr[:, n], sb)
        v_new = u_scr[:, n] - ws[:, :c]
        vb = v_new.astype(BF16)
        o = ws[:, c:] + _bmm(at_scr[:, n], vb)
        s_scr[...] = state * cd_scr[:, n] + _bmm_tn(kd_scr[:, n], vb)
        for h in range(nh):
            o_ref[n * c:(n + 1) * c, h * hd:(h + 1) * hd] = o[h]
        yield


def _gdn_kernel(q_ref, k_ref, v_ref, col_ref, row_ref, o_ref, s_scr, *sets, reverse, n_chunks, d):
    set0, set1 = sets[:len(sets) // 2], sets[len(sets) // 2:]
    i = pl.program_id(1)

    @pl.when(i == 0)
    def _():
        s_scr[...] = jnp.zeros_like(s_scr)
        for r in set1:
            r[...] = jnp.zeros_like(r)

    def step(dst, src):
        local = _gdn_chunk_local(q_ref, k_ref, v_ref, col_ref, row_ref, dst, reverse=reverse, g=n_chunks, d=d)
        recur = _gdn_recurrence(src, s_scr, o_ref, reverse=reverse, g=n_chunks)
        for _ in itertools.zip_longest(local, recur):
            pass

    @pl.when(i % 2 == 0)
    def _():
        step(set0, set1)

    @pl.when(i % 2 == 1)
    def _():
        step(set1, set0)


def _gdn(p, col, row, *, q_col, batch, seq, d, tt=512):
    nh = GDN_HEADS
    width = nh * HEAD_DIM
    nt = seq // tt
    n_chunks = tt // CHUNK
    reverse = d == 1
    order = (lambda t: nt - 1 - t) if reverse else (lambda t: t)
    tile_in = lambda b, i: b * nt + order(jnp.minimum(i, nt - 1))
    tile_out = lambda b, i: b * nt + order(jnp.maximum(i - 1, 0))
    scratch_set = [
        pltpu.VMEM((nh, n_chunks, CHUNK, HEAD_DIM), F32),
        pltpu.VMEM((nh, n_chunks, 2 * CHUNK, HEAD_DIM), BF16),
        pltpu.VMEM((nh, n_chunks, CHUNK, HEAD_DIM), BF16),
        pltpu.VMEM((nh, n_chunks, CHUNK, CHUNK), BF16),
        pltpu.VMEM((nh, n_chunks, 1, HEAD_DIM), F32),
    ]
    return pl.pallas_call(
        functools.partial(_gdn_kernel, reverse=reverse, n_chunks=n_chunks, d=d),
        grid=(batch, nt + 1),
        in_specs=[
            pl.BlockSpec((tt, width), lambda b, i: (tile_in(b, i), q_col)),
            pl.BlockSpec((tt, width), lambda b, i: (tile_in(b, i), q_col + 1)),
            pl.BlockSpec((tt, width), lambda b, i: (tile_in(b, i), q_col + 2)),
            pl.BlockSpec((tt, LANES), lambda b, i: (tile_in(b, i), 0)),
            pl.BlockSpec((n_chunks, SUBLANES, nh * CHUNK), lambda b, i: (tile_in(b, i), 0, 0)),
        ],
        out_specs=pl.BlockSpec((tt, width), lambda b, i: (tile_out(b, i), 0)),
        out_shape=jax.ShapeDtypeStruct((batch * seq, width), F32),
        scratch_shapes=[pltpu.VMEM((nh, HEAD_DIM, HEAD_DIM), F32)] + scratch_set + scratch_set,
        compiler_params=_cparams(("parallel", "arbitrary")),
        name="gdn_bwd" if reverse else "gdn_fwd",
    )(p, p, p, col, row)


def _mixer(x1, batch, seq, mix_norm, w_in, w_out, rg_conv_w, rg_conv_b, rg_gate_a_w, rg_gate_a_b,
           rg_gate_x_w, rg_gate_x_b, rg_lambda, gdn_conv_w, gdn_a_log, gdn_dt_bias, gdn_norm):
    rg_w = rg_conv_w.shape[1]
    qkv_w = gdn_conv_w.shape[1]
    gdn_vw = GDN_HEADS * HEAD_DIM
    n_main = 2 * rg_w + qkv_w + gdn_vw
    p, ba = _in_proj(x1, mix_norm, w_in.T, rg_conv_w, rg_conv_b, gdn_conv_w, n=n_main, seq=seq)

    w_gates, b_gates = _rg_gate_weights(rg_gate_a_w, rg_gate_a_b, rg_gate_x_w, rg_gate_x_b)
    y_rg = _rglru(p, w_gates, b_gates, rg_lambda.astype(F32), batch=batch, seq=seq, gate_col0=rg_w // LANES)

    col, row = _gate_prep(ba, gdn_a_log, gdn_dt_bias)
    q_col = 2 * rg_w // gdn_vw
    o_f = _gdn(p, col, row, q_col=q_col, batch=batch, seq=seq, d=0)
    o_b = _gdn(p, col, row, q_col=q_col, batch=batch, seq=seq, d=1)
    return y_rg, o_f, o_b, p, (2 * rg_w + qkv_w) // gdn_vw, gdn_norm, w_out.astype(BF16)


def kernel(x, ffn1_norm, ffn1_w_gate, ffn1_w_up, ffn1_w_down, mix_norm, w_in, w_out, rg_conv_w, rg_conv_b, rg_gate_a_w, rg_gate_a_b, rg_gate_x_w, rg_gate_x_b, rg_lambda, gdn_conv_w, gdn_a_log, gdn_dt_bias, gdn_norm, ffn2_norm, ffn2_w_gate, ffn2_w_up, ffn2_w_down, final_norm):
    batch, seq, d_model = x.shape
    depth = ffn1_norm.shape[0]
    h = x.reshape(batch * seq, d_model)
    for l in range(depth):
        last = l == depth - 1
        h = _ffn(h, ffn1_norm[l], ffn1_w_gate[l], ffn1_w_up[l], ffn1_w_down[l], final_norm, final_norm=False)
        mixer = _mixer(h, batch, seq, mix_norm[l], w_in[l], w_out[l], rg_conv_w[l], rg_conv_b[l],
                   rg_gate_a_w[l], rg_gate_a_b[l], rg_gate_x_w[l], rg_gate_x_b[l], rg_lambda[l],
                   gdn_conv_w[l], gdn_a_log[l], gdn_dt_bias[l], gdn_norm[l])
        h = _ffn(h, ffn2_norm[l], ffn2_w_gate[l], ffn2_w_up[l], ffn2_w_down[l], final_norm,
                 final_norm=last, mixer=mixer)
    return h.reshape(batch, seq, d_model)
```

```python
import functools
import itertools

import jax
import jax.numpy as jnp
from jax import lax
from jax.experimental import pallas as pl
from jax.experimental.pallas import tpu as pltpu

F32 = jnp.float32
BF16 = jnp.bfloat16

EPS = 1e-6
LOG2_E = 1.4426950408889634
RSQRT_FLOOR = 1e-30
RG_C = 8.0
GDN_HEADS = 4
HEAD_DIM = 128
CHUNK = 64
LANES = 128
SUBLANES = 8
VMEM_LIMIT = 56 * 1024 * 1024


def _cparams(sem):
    return pltpu.CompilerParams(dimension_semantics=sem, vmem_limit_bytes=VMEM_LIMIT)


def _rms(x):
    return x * lax.rsqrt(jnp.mean(x * x, axis=-1, keepdims=True) + EPS)


def _sigmoid(x):
    return 1.0 / (1.0 + jnp.exp(-x))


def _softplus(x):
    return jnp.maximum(x, 0.0) + jnp.log(1.0 + jnp.exp(-jnp.abs(x)))


def _gated_mixer_rows(yrg_ref, of_ref, ob_ref, z_ref, gn_ref, wo_ref, rows):
    o = of_ref[rows, :] + ob_ref[rows, :]
    z = z_ref[rows, :]
    gn = gn_ref[...]
    parts = [yrg_ref[rows, :].astype(BF16)]
    for h in range(GDN_HEADS):
        hs = slice(h * HEAD_DIM, (h + 1) * HEAD_DIM)
        zh = z[:, hs]
        parts.append((_rms(o[:, hs]) * gn * (zh * _sigmoid(zh))).astype(BF16))
    return jnp.dot(jnp.concatenate(parts, axis=-1), wo_ref[...], preferred_element_type=F32)


W_CHUNKS = 8


def _stage_weight_bf16(w_hbm, w_scr, stage, sem):
    rows = w_hbm.shape[0] // W_CHUNKS
    copy = lambda k: pltpu.make_async_copy(w_hbm.at[pl.ds(k * rows, rows), :], stage.at[k % 2], sem.at[k % 2])
    copy(0).start()
    for k in range(W_CHUNKS):
        if k + 1 < W_CHUNKS:
            copy(k + 1).start()
        copy(k).wait()
        w_scr[k * rows:(k + 1) * rows, :] = stage[k % 2].astype(BF16)


def _ffn_kernel(*refs, final_norm, n_sub, with_mixer):
    refs, (wg_ref, wu_ref, wd_ref, stage_in, stage_dn, sem) = refs[:-6], refs[-6:]
    if with_mixer:
        x_ref, mixer_refs, (g_ref, wg_hbm, wu_hbm, wd_hbm, fg_ref, o_ref) = refs[0], refs[1:7], refs[7:]
    else:
        x_ref, g_ref, wg_hbm, wu_hbm, wd_hbm, fg_ref, o_ref = refs

    @pl.when(pl.program_id(0) == 0)
    def _():
        _stage_weight_bf16(wg_hbm, wg_ref, stage_in, sem)
        _stage_weight_bf16(wu_hbm, wu_ref, stage_in, sem)
        _stage_weight_bf16(wd_hbm, wd_ref, stage_dn, sem)

    sub = x_ref.shape[0] // n_sub
    g = g_ref[...]
    for r in range(n_sub):
        rows = slice(r * sub, (r + 1) * sub)
        x = x_ref[rows, :]
        if with_mixer:
            x = x + _gated_mixer_rows(*mixer_refs, rows)
        h = (_rms(x) * g).astype(BF16)
        gate = jnp.dot(h, wg_ref[...], preferred_element_type=F32)
        up = jnp.dot(h, wu_ref[...], preferred_element_type=F32)
        act = (gate * _sigmoid(gate) * up).astype(BF16)
        y = x + 0.5 * jnp.dot(act, wd_ref[...], preferred_element_type=F32)
        if final_norm:
            y = _rms(y) * fg_ref[...]
        o_ref[rows, :] = y


def _ffn(x, g, wg, wu, wd, fg, *, final_norm, mixer=None, tm=512, n_sub=2):
    m, d = x.shape
    f = wg.shape[1]
    assert m % tm == 0 and tm % n_sub == 0
    resident = dict(pipeline_mode=pl.Buffered(1))
    row_spec = lambda w, col=0: pl.BlockSpec((tm, w), lambda i: (i, col))
    const_spec = lambda shape, **kw: pl.BlockSpec(shape, lambda i: (0, 0), **kw)
    operands, in_specs = [x], [row_spec(d)]
    if mixer is not None:
        y_rg, o_f, o_b, p, z_col, gdn_norm, w_out = mixer
        wr, wv = y_rg.shape[1], o_f.shape[1]
        operands += [y_rg, o_f, o_b, p, gdn_norm.reshape(1, -1), w_out]
        in_specs += [row_spec(wr), row_spec(wv), row_spec(wv), row_spec(wv, z_col),
                     const_spec((1, HEAD_DIM)), const_spec((wr + wv, d), **resident)]
    assert d % W_CHUNKS == 0 and f % (W_CHUNKS * 2 * SUBLANES) == 0
    in_hbm = pl.BlockSpec(memory_space=pl.ANY)
    operands += [g.reshape(1, d), wg, wu, wd, fg.reshape(1, d)]
    in_specs += [const_spec((1, d)), in_hbm, in_hbm, in_hbm, const_spec((1, d))]
    return pl.pallas_call(
        functools.partial(_ffn_kernel, final_norm=final_norm, n_sub=n_sub, with_mixer=mixer is not None),
        grid=(m // tm,),
        in_specs=in_specs,
        out_specs=row_spec(d),
        out_shape=jax.ShapeDtypeStruct((m, d), F32),
        scratch_shapes=[
            pltpu.VMEM((d, f), BF16), pltpu.VMEM((d, f), BF16), pltpu.VMEM((f, d), BF16),
            pltpu.VMEM((2, d // W_CHUNKS, f), F32), pltpu.VMEM((2, f // W_CHUNKS, d), F32),
            pltpu.SemaphoreType.DMA((2,)),
        ],
        compiler_params=_cparams(("arbitrary",)),
        name="ffn_final" if final_norm else "ffn",
    )(*operands)


def _inproj_kernel(x_ref, xp_ref, xn_ref, g_ref, w32_ref, cwr_ref, cbr_ref, cwq_ref, p_ref, ba_ref,
                   stage_scr, w_ref, *, n_sub, tiles_per_seq, rg_w, qkv_w, n_qk, q_scale):
    tm, n = p_ref.shape
    sub = tm // n_sub
    hal = SUBLANES
    i = pl.program_id(0)

    @pl.when(i == 0)
    def _():
        n_in = w32_ref.shape[0]
        for c in range(0, n, LANES):
            w_ref[:, c:c + LANES] = w32_ref[c:c + LANES, :].T.astype(BF16)
        tail = w32_ref[n_in - LANES:n_in, :].T
        w_ref[:, n:] = jnp.zeros((w_ref.shape[0], LANES), BF16)
        w_ref[:, n:n_in] = tail[:, LANES - (n_in - n):].astype(BF16)

    pos = i % tiles_per_seq
    prev = jnp.where(pos == 0, 0.0, xp_ref[...])
    nxt = jnp.where(pos == tiles_per_seq - 1, 0.0, xn_ref[...])
    g = g_ref[...]
    cwr = cwr_ref[...]
    mid = slice(hal, hal + sub)
    blk = GDN_HEADS * HEAD_DIM

    def stage(slot, pw):
        for j in range(blk // LANES):
            stage_scr[slot * (blk // LANES) + j] = pw[:, j * LANES:(j + 1) * LANES]

    def conv(slab, cw):
        tap = lambda off: stage_scr[slab, hal + off:hal + off + sub, :]
        acc = cw[0:1, :] * tap(-2) + cw[1:2, :] * tap(-1)
        return acc + cw[2:3, :] * tap(0) + cw[3:4, :] * tap(1)

    for r in range(n_sub):
        lo, hi = r * sub - hal, (r + 1) * sub + hal
        parts = ([prev] if lo < 0 else []) + [x_ref[max(lo, 0):min(hi, tm), :]] + ([nxt] if hi > tm else [])
        xw = jnp.concatenate(parts, axis=0)
        h = (_rms(xw) * g).astype(BF16)
        rows = slice(r * sub, (r + 1) * sub)
        proj = lambda c0, c1: jnp.dot(h, w_ref[:, c0:c1], preferred_element_type=F32)
        stage(0, proj(0, rg_w))
        for c in range(0, rg_w, LANES):
            p_ref[rows, c:c + LANES] = conv(c // LANES, cwr[:, c:c + LANES]) + cbr_ref[:, c:c + LANES]
        p_ref[rows, rg_w:2 * rg_w] = proj(rg_w, 2 * rg_w)[mid]
        for c0 in range(0, qkv_w, blk):
            slot = 1 + c0 // blk
            stage(slot, proj(2 * rg_w + c0, 2 * rg_w + c0 + blk))
            for c in range(c0, c0 + blk, LANES):
                y = conv((blk + c) // LANES, cwq_ref[:, c:c + LANES])
                y = y * _sigmoid(y)
                if c < n_qk * LANES:
                    scale = q_scale if c < n_qk * LANES // 2 else 1.0
                    y = y * (lax.rsqrt(jnp.sum(y * y, axis=-1, keepdims=True) + EPS) * scale)
                p_ref[rows, 2 * rg_w + c:2 * rg_w + c + LANES] = y
        tail = proj(2 * rg_w + qkv_w, n + LANES)[mid]
        p_ref[rows, 2 * rg_w + qkv_w:n] = tail[:, :n - 2 * rg_w - qkv_w]
        ba_ref[rows, :] = tail[:, n - 2 * rg_w - qkv_w:]


def _in_proj(x, g, w_in_t, rg_conv_w, rg_conv_b, qkv_conv_w, *, n, seq, tm=512, n_sub=2):
    m, d = x.shape
    rg_w, qkv_w = rg_conv_w.shape[1], qkv_conv_w.shape[1]
    assert n % LANES == 0 and n < w_in_t.shape[0] <= n + LANES and w_in_t.shape[0] % SUBLANES == 0
    assert m % tm == 0 and tm % n_sub == 0 and seq % tm == 0
    assert rg_w == GDN_HEADS * HEAD_DIM and qkv_w % rg_w == 0
    hb = tm // SUBLANES
    const_spec = lambda shape, **kw: pl.BlockSpec(shape, lambda i: (0, 0), **kw)
    return pl.pallas_call(
        functools.partial(_inproj_kernel, n_sub=n_sub, tiles_per_seq=seq // tm, rg_w=rg_w, qkv_w=qkv_w,
                          n_qk=2 * GDN_HEADS, q_scale=HEAD_DIM ** -0.5),
        grid=(m // tm,),
        in_specs=[
            pl.BlockSpec((tm, d), lambda i: (i, 0)),
            pl.BlockSpec((SUBLANES, d), lambda i: (jnp.maximum(i * hb - 1, 0), 0)),
            pl.BlockSpec((SUBLANES, d), lambda i: (jnp.minimum((i + 1) * hb, m // SUBLANES - 1), 0)),
            const_spec((1, d)),
            const_spec(w_in_t.shape, pipeline_mode=pl.Buffered(1)),
            const_spec((4, rg_w)),
            const_spec((1, rg_w)),
            const_spec((4, qkv_w)),
        ],
        out_specs=[
            pl.BlockSpec((tm, n), lambda i: (i, 0)),
            pl.BlockSpec((tm, LANES), lambda i: (i, 0)),
        ],
        out_shape=[jax.ShapeDtypeStruct((m, n), F32), jax.ShapeDtypeStruct((m, LANES), F32)],
        scratch_shapes=[pltpu.VMEM(((rg_w + qkv_w) // LANES, tm // n_sub + 2 * SUBLANES, LANES), F32),
                        pltpu.VMEM((d, n + LANES), BF16)],
        compiler_params=_cparams(("arbitrary",)),
        name="in_proj",
    )(x, x, x, g.reshape(1, d), w_in_t, rg_conv_w, rg_conv_b.reshape(1, -1), qkv_conv_w)


RG_SEGS = 2 * SUBLANES
RG_PAD = 4


def _rglru_kernel(x_ref, gate_ref, wg_ref, bg_ref, lam_ref, o_ref,
                  af_scr, bf_scr, ab_scr, bb_scr, *, seq, rt):
    seg_len = seq // RG_SEGS
    pitch = seg_len + RG_PAD
    tiles_per_seg = seg_len // rt
    n_tiles = seq // rt
    a_scr = (af_scr, ab_scr)
    b_scr = (bf_scr, bb_scr)

    bg = bg_ref[...]
    half_log2_a = (-0.5 * RG_C * LOG2_E) * _softplus(-lam_ref[...])

    def gates(i, _):
        r0 = pl.multiple_of(i * rt, rt)
        xc = x_ref[pl.ds(r0, rt), :]
        pre = jnp.dot(xc.astype(BF16), wg_ref[...], preferred_element_type=F32) + bg
        hx = 0.5 * xc
        seg = i // tiles_per_seg
        off = seg * pitch + (i - seg * tiles_per_seg) * rt
        for d in range(2):
            tr = jnp.tanh(pre[:, (2 * d) * LANES:(2 * d + 1) * LANES])
            ti = jnp.tanh(pre[:, (2 * d + 1) * LANES:(2 * d + 2) * LANES])
            hl = half_log2_a[d:d + 1, :]
            a = jnp.exp2(hl + hl * tr)
            om = 1.0 - a * a
            b = (om * lax.rsqrt(jnp.maximum(om, RSQRT_FLOOR))) * (hx + hx * ti)
            a_scr[d][pl.ds(off, rt), :] = a
            b_scr[d][pl.ds(off, rt), :] = b
        return 0

    lax.fori_loop(0, n_tiles, gates, 0, unroll=4)

    def scan(t, carry):
        hf, pf, hb, pb = carry
        idx = pl.ds(t, RG_SEGS, stride=pitch)
        a = af_scr[idx, :]
        hf = a * hf + bf_scr[idx, :]
        pf = a * pf
        bf_scr[idx, :] = hf
        af_scr[idx, :] = pf
        idx = pl.ds(seg_len - 1 - t, RG_SEGS, stride=pitch)
        a = ab_scr[idx, :]
        hb = a * hb + bb_scr[idx, :]
        pb = a * pb
        bb_scr[idx, :] = hb
        ab_scr[idx, :] = pb
        return hf, pf, hb, pb

    zeros = jnp.zeros((RG_SEGS, LANES), F32)
    ones = jnp.ones((RG_SEGS, LANES), F32)
    hf, pf, hb, pb = lax.fori_loop(0, seg_len, scan, (zeros, ones, zeros, ones), unroll=8)

    cf = [jnp.zeros((1, LANES), F32)]
    for s in range(1, RG_SEGS):
        cf.append(hf[s - 1:s, :] + pf[s - 1:s, :] * cf[s - 1])
    cbk = [None] * RG_SEGS
    cbk[RG_SEGS - 1] = jnp.zeros((1, LANES), F32)
    for s in range(RG_SEGS - 2, -1, -1):
        cbk[s] = hb[s + 1:s + 2, :] + pb[s + 1:s + 2, :] * cbk[s + 1]

    for s in range(RG_SEGS):
        def fix(k, _, s=s):
            off = s * pitch + k * rt
            rows = pl.multiple_of(s * seg_len + k * rt, SUBLANES)
            h = bf_scr[pl.ds(off, rt), :] + af_scr[pl.ds(off, rt), :] * cf[s]
            h = h + (bb_scr[pl.ds(off, rt), :] + ab_scr[pl.ds(off, rt), :] * cbk[s])
            o_ref[pl.ds(rows, rt), :] = h * jax.nn.gelu(gate_ref[pl.ds(rows, rt), :], approximate=True)
            return 0

        lax.fori_loop(0, tiles_per_seg, fix, 0)


def _rglru(p, w_gates, b_gates, lam, *, batch, seq, gate_col0, rt=512):
    ngrp = w_gates.shape[0]
    seg_len = seq // RG_SEGS
    rt = min(rt, seg_len)
    assert seq % RG_SEGS == 0 and seg_len % rt == 0
    scr = pltpu.VMEM((RG_SEGS * (seg_len + RG_PAD), LANES), F32)
    return pl.pallas_call(
        functools.partial(_rglru_kernel, seq=seq, rt=rt),
        grid=(batch, ngrp),
        in_specs=[
            pl.BlockSpec((seq, LANES), lambda b, c: (b, c)),
            pl.BlockSpec((seq, LANES), lambda b, c: (b, gate_col0 + c)),
            pl.BlockSpec((None, LANES, 4 * LANES), lambda b, c: (c, 0, 0)),
            pl.BlockSpec((None, 1, 4 * LANES), lambda b, c: (c, 0, 0)),
            pl.BlockSpec((2, LANES), lambda b, c: (0, c)),
        ],
        out_specs=pl.BlockSpec((seq, LANES), lambda b, c: (b, c)),
        out_shape=jax.ShapeDtypeStruct((batch * seq, ngrp * LANES), F32),
        scratch_shapes=[scr, scr, scr, scr],
        compiler_params=_cparams(("parallel", "parallel")),
        name="rglru",
    )(p, p, w_gates, b_gates, lam)


def _rg_gate_weights(wa, ba, wx, bx):
    ndir, nblk, blk, _ = wa.shape
    ngrp = nblk * blk // LANES
    per = LANES // blk

    def bd(w):
        w = w.reshape(ndir, ngrp, per, blk, blk)
        eye = jnp.eye(per, dtype=w.dtype)
        return jnp.einsum("dgpij,pq->dgpiqj", w, eye).reshape(ndir, ngrp, LANES, LANES)

    a, x = bd(wa), bd(wx)
    w = jnp.concatenate([a[0], x[0], a[1], x[1]], axis=-1)
    ba = ba.reshape(ndir, ngrp, 1, LANES)
    bx = bx.reshape(ndir, ngrp, 1, LANES)
    b = jnp.concatenate([ba[0], bx[0], ba[1], bx[1]], axis=-1)
    return (0.5 * w).astype(BF16), (0.5 * b).astype(F32)


def _gate_kernel(ba_ref, alog_ref, dtb_ref, col_ref, row_ref, *, tm):
    nh = GDN_HEADS
    ri = lax.broadcasted_iota(jnp.int32, (LANES, LANES), 0)
    ci = lax.broadcasted_iota(jnp.int32, (LANES, LANES), 1)
    same = (ri // CHUNK) == (ci // CHUNK)
    lower = jnp.where(same & (ri >= ci), 1.0, 0.0).astype(BF16)
    upper = jnp.where(same & (ri <= ci), 1.0, 0.0).astype(BF16)
    masks = jnp.concatenate([lower, upper], axis=0)
    lane = ci
    neg_a = -jnp.exp(alog_ref[...])
    dtb = dtb_ref[...]
    for k in range(tm // LANES):
        sl = slice(k * LANES, (k + 1) * LANES)
        raw = ba_ref[sl, :]
        beta = _sigmoid(raw)
        gk = neg_a * _softplus(raw + dtb)
        gk = jnp.where((lane >= 2 * nh) & (lane < 4 * nh), gk, 0.0)
        csum = jnp.zeros((2 * LANES, LANES), F32)
        rest = gk
        for _ in range(3):
            piece = rest.astype(BF16)
            csum = csum + jnp.dot(masks, piece, preferred_element_type=F32)
            rest = rest - piece.astype(F32)
        gc = jnp.where(lane < 3 * nh, csum[:LANES], csum[LANES:])
        col = jnp.where(lane < 2 * nh, beta, gc)
        col_ref[sl, :] = col
        colt = col.T
        for q in range(LANES // CHUNK):
            rows = [jnp.concatenate([colt[(2 + dr) * nh + h:(2 + dr) * nh + h + 1, q * CHUNK:(q + 1) * CHUNK]
                                     for h in range(nh)], axis=1) for dr in range(2)]
            rows.append(jnp.zeros((SUBLANES - 2, nh * CHUNK), F32))
            row_ref[k * (LANES // CHUNK) + q] = jnp.concatenate(rows, axis=0)


def _gate_prep(ba, a_log, dt_bias, *, tm=2048):
    m = ba.shape[0]
    nh = GDN_HEADS
    pad = lambda v: jnp.zeros((1, LANES), F32).at[0, 2 * nh:4 * nh].set(v.reshape(-1).astype(F32))
    return pl.pallas_call(
        functools.partial(_gate_kernel, tm=tm),
        grid=(m // tm,),
        in_specs=[
            pl.BlockSpec((tm, LANES), lambda i: (i, 0)),
            pl.BlockSpec((1, LANES), lambda i: (0, 0)),
            pl.BlockSpec((1, LANES), lambda i: (0, 0)),
        ],
        out_specs=[
            pl.BlockSpec((tm, LANES), lambda i: (i, 0)),
            pl.BlockSpec((tm // CHUNK, SUBLANES, nh * CHUNK), lambda i: (i, 0, 0)),
        ],
        out_shape=[jax.ShapeDtypeStruct((m, LANES), F32),
                   jax.ShapeDtypeStruct((m // CHUNK, SUBLANES, nh * CHUNK), F32)],
        compiler_params=_cparams(("parallel",)),
        name="gate_prep",
    )(ba, pad(a_log), pad(dt_bias))


def _bmm(a, b):
    return lax.dot_general(a.astype(BF16), b.astype(BF16), (((2,), (1,)), ((0,), (0,))),
                           preferred_element_type=F32)


def _bmm_nt(a, b):
    return lax.dot_general(a.astype(BF16), b.astype(BF16), (((2,), (2,)), ((0,), (0,))),
                           preferred_element_type=F32)


def _bmm_tn(a, b):
    return lax.dot_general(a.astype(BF16), b.astype(BF16), (((1,), (1,)), ((0,), (0,))),
                           preferred_element_type=F32)


def _unit_tri_inverse_levels(lm, ri, ci, bdmask, nb):
    def blockdiag(a):
        ab = a.astype(BF16)
        return jnp.where(bdmask, jnp.concatenate([ab] * nb, axis=1), jnp.zeros((), BF16))

    eye = jnp.where(ri == ci, 1.0, 0.0).astype(F32)
    x = eye - jnp.where((ri // 2) == (ci // 2), lm, 0.0)
    s = 2
    while s < CHUNK:
        cm = jnp.where(((ri // (2 * s)) == (ci // (2 * s))) & ((ri // s) != (ci // s)), lm, 0.0)
        x = x - _bmm(_bmm(x, blockdiag(cm)), blockdiag(x))
        yield x
        s *= 2


def _gdn_chunk_local(q_ref, k_ref, v_ref, col_ref, row_ref, dst, *, reverse, g, d):
    u_scr, wq_scr, kd_scr, at_scr, cd_scr = dst
    nh, c, hd = GDN_HEADS, CHUNK, HEAD_DIM
    pw = nh * c
    fw = nh * hd
    ri = lax.broadcasted_iota(jnp.int32, (1, c, pw), 1)
    ci = lax.broadcasted_iota(jnp.int32, (1, c, pw), 2) % c
    if reverse:
        incl, strict, last = ri <= ci, ri < ci, 0
    else:
        incl, strict, last = ri >= ci, ri > ci, c - 1
    bdmask = (lax.broadcasted_iota(jnp.int32, (1, pw, pw), 1) // c
              == lax.broadcasted_iota(jnp.int32, (1, pw, pw), 2) // c)
    kmask = (lax.broadcasted_iota(jnp.int32, (1, pw, fw), 1) // c
             == lax.broadcasted_iota(jnp.int32, (1, pw, fw), 2) // hd)
    low_half = lax.broadcasted_iota(jnp.int32, (g * c, hd), 1) < c

    col = col_ref[...]
    bcast = lambda j: jnp.broadcast_to(col[:, j:j + 1], (g * c, hd))
    beta = jnp.concatenate([bcast(d * nh + h) for h in range(nh)], axis=1).reshape(g, c, fw)
    gcs = [bcast((2 + d) * nh + h) for h in range(nh)]
    gc = jnp.concatenate(gcs, axis=1).reshape(g, c, fw)
    gc_col = jnp.concatenate([jnp.where(low_half, gcs[h], gcs[h + 1]) for h in range(0, nh, 2)],
                             axis=1).reshape(g, c, pw)
    gc_row = row_ref[:, d:d + 1, :]
    g_last = gc[:, last:last + 1, :]
    eg = jnp.exp(gc)
    decay = jnp.where(incl, jnp.exp(gc_col - gc_row), 0.0)
    q = q_ref[...].reshape(g, c, fw)
    k = k_ref[...].reshape(g, c, fw)
    v = v_ref[...].reshape(g, c, fw)
    k_beta = k * beta
    v_beta = v * beta
    kb16 = k.astype(BF16)
    k_bd = jnp.where(kmask, jnp.concatenate([kb16] * nh, axis=1), jnp.zeros((), BF16))
    kq = _bmm_nt(jnp.concatenate([k_beta, q], axis=1), k_bd)
    lm = jnp.where(strict, kq[:, :c] * decay, 0.0)
    attn = kq[:, c:] * decay
    yield
    for t in _unit_tri_inverse_levels(lm, ri, ci, bdmask, nh):
        yield
    kbe = k_beta * eg
    qd = q * eg
    kd = k * jnp.exp(g_last - gc)
    cd = jnp.exp(g_last)
    for h in range(nh):
        hs = slice(h * hd, (h + 1) * hd)
        ps = slice(h * c, (h + 1) * c)
        uw = _bmm(t[:, :, ps], jnp.concatenate([v_beta[:, :, hs], kbe[:, :, hs]], axis=-1))
        u_scr[h] = uw[:, :, :hd]
        wq_scr[h] = jnp.concatenate([uw[:, :, hd:], qd[:, :, hs]], axis=1).astype(BF16)
        at_scr[h] = attn[:, :, ps].astype(BF16)
        kd_scr[h] = kd[:, :, hs].astype(BF16)
        cd_scr[h] = cd[:, :, hs]
        if h % 2 == 1:
            yield


def _gdn_recurrence(src, s_scr, o_ref, *, reverse, g):
    u_scr, wq_scr, kd_scr, at_scr, cd_scr = src
    nh, c, hd = GDN_HEADS, CHUNK, HEAD_DIM
    for step in range(g):
        n = g - 1 - step if reverse else step
        state = s_scr[...]
        sb = state.astype(BF16)
        ws = _bmm(wq_scr[:, n], sb)
        v_new = u_scr[:, n] - ws[:, :c]
        vb = v_new.astype(BF16)
        o = ws[:, c:] + _bmm(at_scr[:, n], vb)
        s_scr[...] = state * cd_scr[:, n] + _bmm_tn(kd_scr[:, n], vb)
        for h in range(nh):
            o_ref[n * c:(n + 1) * c, h * hd:(h + 1) * hd] = o[h]
        yield


def _gdn_kernel(q_ref, k_ref, v_ref, col_ref, row_ref, o_ref, s_scr, *sets, reverse, n_chunks, n_tiles, d):
    set0, set1 = sets[:len(sets) // 2], sets[len(sets) // 2:]
    i = pl.program_id(1)

    @pl.when(i == 0)
    def _():
        s_scr[...] = jnp.zeros_like(s_scr)
        for r in set1:
            r[...] = jnp.zeros_like(r)

    def step(dst, src):
        local = _gdn_chunk_local(q_ref, k_ref, v_ref, col_ref, row_ref, dst, reverse=reverse, g=n_chunks, d=d)
        recur = _gdn_recurrence(src, s_scr, o_ref, reverse=reverse, g=n_chunks)
        for _ in itertools.zip_longest(local, recur):
            pass

    last = pl.num_programs(1) - 1

    @pl.when((i % 2 == 0) & (i < last))
    def _():
        step(set0, set1)

    @pl.when((i % 2 == 1) & (i < last))
    def _():
        step(set1, set0)

    @pl.when(i == last)
    def _():
        src = set0 if n_tiles % 2 == 1 else set1
        for _ in _gdn_recurrence(src, s_scr, o_ref, reverse=reverse, g=n_chunks):
            pass


def _gdn(p, col, row, *, q_col, batch, seq, d, tt=512):
    nh = GDN_HEADS
    width = nh * HEAD_DIM
    nt = seq // tt
    n_chunks = tt // CHUNK
    reverse = d == 1
    order = (lambda t: nt - 1 - t) if reverse else (lambda t: t)
    tile_in = lambda b, i: b * nt + order(jnp.minimum(i, nt - 1))
    tile_out = lambda b, i: b * nt + order(jnp.maximum(i - 1, 0))
    scratch_set = [
        pltpu.VMEM((nh, n_chunks, CHUNK, HEAD_DIM), F32),
        pltpu.VMEM((nh, n_chunks, 2 * CHUNK, HEAD_DIM), BF16),
        pltpu.VMEM((nh, n_chunks, CHUNK, HEAD_DIM), BF16),
        pltpu.VMEM((nh, n_chunks, CHUNK, CHUNK), BF16),
        pltpu.VMEM((nh, n_chunks, 1, HEAD_DIM), F32),
    ]
    return pl.pallas_call(
        functools.partial(_gdn_kernel, reverse=reverse, n_chunks=n_chunks, n_tiles=nt, d=d),
        grid=(batch, nt + 1),
        in_specs=[
            pl.BlockSpec((tt, width), lambda b, i: (tile_in(b, i), q_col)),
            pl.BlockSpec((tt, width), lambda b, i: (tile_in(b, i), q_col + 1)),
            pl.BlockSpec((tt, width), lambda b, i: (tile_in(b, i), q_col + 2)),
            pl.BlockSpec((tt, LANES), lambda b, i: (tile_in(b, i), 0)),
            pl.BlockSpec((n_chunks, SUBLANES, nh * CHUNK), lambda b, i: (tile_in(b, i), 0, 0)),
        ],
        out_specs=pl.BlockSpec((tt, width), lambda b, i: (tile_out(b, i), 0)),
        out_shape=jax.ShapeDtypeStruct((batch * seq, width), F32),
        scratch_shapes=[pltpu.VMEM((nh, HEAD_DIM, HEAD_DIM), F32)] + scratch_set + scratch_set,
        compiler_params=_cparams(("parallel", "arbitrary")),
        name="gdn_bwd" if reverse else "gdn_fwd",
    )(p, p, p, col, row)


def _mixer(x1, batch, seq, mix_norm, w_in, w_out, rg_conv_w, rg_conv_b, rg_gate_a_w, rg_gate_a_b,
           rg_gate_x_w, rg_gate_x_b, rg_lambda, gdn_conv_w, gdn_a_log, gdn_dt_bias, gdn_norm):
    rg_w = rg_conv_w.shape[1]
    qkv_w = gdn_conv_w.shape[1]
    gdn_vw = GDN_HEADS * HEAD_DIM
    n_main = 2 * rg_w + qkv_w + gdn_vw
    p, ba = _in_proj(x1, mix_norm, w_in.T, rg_conv_w, rg_conv_b, gdn_conv_w, n=n_main, seq=seq)

    w_gates, b_gates = _rg_gate_weights(rg_gate_a_w, rg_gate_a_b, rg_gate_x_w, rg_gate_x_b)
    y_rg = _rglru(p, w_gates, b_gates, rg_lambda.astype(F32), batch=batch, seq=seq, gate_col0=rg_w // LANES)

    col, row = _gate_prep(ba, gdn_a_log, gdn_dt_bias)
    q_col = 2 * rg_w // gdn_vw
    o_f = _gdn(p, col, row, q_col=q_col, batch=batch, seq=seq, d=0)
    o_b = _gdn(p, col, row, q_col=q_col, batch=batch, seq=seq, d=1)
    return y_rg, o_f, o_b, p, (2 * rg_w + qkv_w) // gdn_vw, gdn_norm, w_out.astype(BF16)


def kernel(x, ffn1_norm, ffn1_w_gate, ffn1_w_up, ffn1_w_down, mix_norm, w_in, w_out, rg_conv_w, rg_conv_b, rg_gate_a_w, rg_gate_a_b, rg_gate_x_w, rg_gate_x_b, rg_lambda, gdn_conv_w, gdn_a_log, gdn_dt_bias, gdn_norm, ffn2_norm, ffn2_w_gate, ffn2_w_up, ffn2_w_down, final_norm):
    batch, seq, d_model = x.shape
    depth = ffn1_norm.shape[0]
    h = x.reshape(batch * seq, d_model)
    for l in range(depth):
        last = l == depth - 1
        h = _ffn(h, ffn1_norm[l], ffn1_w_gate[l], ffn1_w_up[l], ffn1_w_down[l], final_norm, final_norm=False,
                 tm=1024, n_sub=4)
        mixer = _mixer(h, batch, seq, mix_norm[l], w_in[l], w_out[l], rg_conv_w[l], rg_conv_b[l],
                   rg_gate_a_w[l], rg_gate_a_b[l], rg_gate_x_w[l], rg_gate_x_b[l], rg_lambda[l],
                   gdn_conv_w[l], gdn_a_log[l], gdn_dt_bias[l], gdn_norm[l])
        h = _ffn(h, ffn2_norm[l], ffn2_w_gate[l], ffn2_w_up[l], ffn2_w_down[l], final_norm,
                 final_norm=last, mixer=mixer)
    return h.reshape(batch, seq, d_model)
```

```python
import functools
import itertools

import jax
import jax.numpy as jnp
from jax import lax
from jax.experimental import pallas as pl
from jax.experimental.pallas import tpu as pltpu

F32 = jnp.float32
BF16 = jnp.bfloat16

EPS = 1e-6
LOG2_E = 1.4426950408889634
RSQRT_FLOOR = 1e-30
RG_C = 8.0
GDN_HEADS = 4
HEAD_DIM = 128
CHUNK = 64
LANES = 128
SUBLANES = 8
VMEM_LIMIT = 56 * 1024 * 1024


def _cparams(sem):
    return pltpu.CompilerParams(dimension_semantics=sem, vmem_limit_bytes=VMEM_LIMIT)


def _rms(x):
    return x * lax.rsqrt(jnp.mean(x * x, axis=-1, keepdims=True) + EPS)


def _sigmoid(x):
    return 1.0 / (1.0 + jnp.exp(-x))


def _softplus(x):
    return jnp.maximum(x, 0.0) + jnp.log(1.0 + jnp.exp(-jnp.abs(x)))


def _gated_mixer_rows(yrg_ref, of_ref, ob_ref, z_ref, gn_ref, wo_ref, rows):
    o = of_ref[rows, :] + ob_ref[rows, :]
    z = z_ref[rows, :]
    gn = gn_ref[...]
    parts = [yrg_ref[rows, :].astype(BF16)]
    for h in range(GDN_HEADS):
        hs = slice(h * HEAD_DIM, (h + 1) * HEAD_DIM)
        zh = z[:, hs]
        parts.append((_rms(o[:, hs]) * gn * (zh * _sigmoid(zh))).astype(BF16))
    return jnp.dot(jnp.concatenate(parts, axis=-1), wo_ref[...], preferred_element_type=F32)


W_CHUNKS = 8


def _stage_weights_bf16(streams):
    def copy(stream, k):
        src, _, stage, sem = stream
        rows = stage.shape[1]
        return pltpu.make_async_copy(src.at[pl.ds(k * rows, rows), :], stage.at[k % 2], sem.at[k % 2])

    for s in streams:
        copy(s, 0).start()
    for k in range(W_CHUNKS):
        if k + 1 < W_CHUNKS:
            for s in streams:
                copy(s, k + 1).start()
        for s in streams:
            copy(s, k).wait()
            _, dst, stage, _ = s
            rows = stage.shape[1]
            dst[k * rows:(k + 1) * rows, :] = stage[k % 2].astype(BF16)


def _ffn_kernel(*refs, final_norm, n_sub, with_mixer):
    refs, (wg_ref, wu_ref, wd_ref, stage_in, stage_dn, sem) = refs[:-6], refs[-6:]
    if with_mixer:
        x_ref, mixer_refs, (g_ref, wg_hbm, wu_hbm, wd_hbm, fg_ref, o_ref) = refs[0], refs[1:7], refs[7:]
    else:
        x_ref, g_ref, wg_hbm, wu_hbm, wd_hbm, fg_ref, o_ref = refs

    @pl.when(pl.program_id(0) == 0)
    def _():
        _stage_weights_bf16([(wg_hbm, wg_ref, stage_in.at[0], sem.at[0]),
                             (wu_hbm, wu_ref, stage_in.at[1], sem.at[1])])
        half = wd_hbm.shape[0] // 2
        _stage_weights_bf16([(wd_hbm.at[pl.ds(j * half, half), :], wd_ref.at[pl.ds(j * half, half), :],
                              stage_dn.at[j], sem.at[j]) for j in range(2)])

    sub = x_ref.shape[0] // n_sub
    g = g_ref[...]
    for r in range(n_sub):
        rows = slice(r * sub, (r + 1) * sub)
        x = x_ref[rows, :]
        if with_mixer:
            x = x + _gated_mixer_rows(*mixer_refs, rows)
        h = (_rms(x) * g).astype(BF16)
        gate = jnp.dot(h, wg_ref[...], preferred_element_type=F32)
        up = jnp.dot(h, wu_ref[...], preferred_element_type=F32)
        act = (gate * _sigmoid(gate) * up).astype(BF16)
        y = x + 0.5 * jnp.dot(act, wd_ref[...], preferred_element_type=F32)
        if final_norm:
            y = _rms(y) * fg_ref[...]
        o_ref[rows, :] = y


def _ffn(x, g, wg, wu, wd, fg, *, final_norm, mixer=None, tm=512, n_sub=2):
    m, d = x.shape
    f = wg.shape[1]
    assert m % tm == 0 and tm % n_sub == 0
    resident = dict(pipeline_mode=pl.Buffered(1))
    row_spec = lambda w, col=0: pl.BlockSpec((tm, w), lambda i: (i, col))
    const_spec = lambda shape, **kw: pl.BlockSpec(shape, lambda i: (0, 0), **kw)
    operands, in_specs = [x], [row_spec(d)]
    if mixer is not None:
        y_rg, o_f, o_b, p, z_col, gdn_norm, w_out = mixer
        wr, wv = y_rg.shape[1], o_f.shape[1]
        operands += [y_rg, o_f, o_b, p, gdn_norm.reshape(1, -1), w_out]
        in_specs += [row_spec(wr), row_spec(wv), row_spec(wv), row_spec(wv, z_col),
                     const_spec((1, HEAD_DIM)), const_spec((wr + wv, d), **resident)]
    assert d % W_CHUNKS == 0 and f % (W_CHUNKS * 2 * SUBLANES) == 0
    in_hbm = pl.BlockSpec(memory_space=pl.ANY)
    operands += [g.reshape(1, d), wg, wu, wd, fg.reshape(1, d)]
    in_specs += [const_spec((1, d)), in_hbm, in_hbm, in_hbm, const_spec((1, d))]
    return pl.pallas_call(
        functools.partial(_ffn_kernel, final_norm=final_norm, n_sub=n_sub, with_mixer=mixer is not None),
        grid=(m // tm,),
        in_specs=in_specs,
        out_specs=row_spec(d),
        out_shape=jax.ShapeDtypeStruct((m, d), F32),
        scratch_shapes=[
            pltpu.VMEM((d, f), BF16), pltpu.VMEM((d, f), BF16), pltpu.VMEM((f, d), BF16),
            pltpu.VMEM((2, 2, d // W_CHUNKS, f), F32),
            pltpu.VMEM((2, 2, f // (2 * W_CHUNKS), d), F32),
            pltpu.SemaphoreType.DMA((2, 2)),
        ],
        compiler_params=_cparams(("arbitrary",)),
        name="ffn_final" if final_norm else "ffn",
    )(*operands)


def _inproj_kernel(x_ref, xp_ref, xn_ref, g_ref, w32_ref, cwr_ref, cbr_ref, cwq_ref, p_ref, ba_ref,
                   stage_scr, w_ref, *, n_sub, tiles_per_seq, rg_w, qkv_w, n_qk, q_scale):
    tm, n = p_ref.shape
    sub = tm // n_sub
    hal = SUBLANES
    i = pl.program_id(0)

    @pl.when(i == 0)
    def _():
        n_in = w32_ref.shape[0]
        for c in range(0, n, LANES):
            w_ref[:, c:c + LANES] = w32_ref[c:c + LANES, :].T.astype(BF16)
        tail = w32_ref[n_in - LANES:n_in, :].T
        w_ref[:, n:] = jnp.zeros((w_ref.shape[0], LANES), BF16)
        w_ref[:, n:n_in] = tail[:, LANES - (n_in - n):].astype(BF16)

    pos = i % tiles_per_seq
    prev = jnp.where(pos == 0, 0.0, xp_ref[...])
    nxt = jnp.where(pos == tiles_per_seq - 1, 0.0, xn_ref[...])
    g = g_ref[...]
    cwr = cwr_ref[...]
    mid = slice(hal, hal + sub)
    blk = GDN_HEADS * HEAD_DIM

    def stage(slot, pw):
        for j in range(blk // LANES):
            stage_scr[slot * (blk // LANES) + j] = pw[:, j * LANES:(j + 1) * LANES]

    def conv(slab, cw):
        tap = lambda off: stage_scr[slab, hal + off:hal + off + sub, :]
        acc = cw[0:1, :] * tap(-2) + cw[1:2, :] * tap(-1)
        return acc + cw[2:3, :] * tap(0) + cw[3:4, :] * tap(1)

    for r in range(n_sub):
        lo, hi = r * sub - hal, (r + 1) * sub + hal
        parts = ([prev] if lo < 0 else []) + [x_ref[max(lo, 0):min(hi, tm), :]] + ([nxt] if hi > tm else [])
        xw = jnp.concatenate(parts, axis=0)
        h = (_rms(xw) * g).astype(BF16)
        rows = slice(r * sub, (r + 1) * sub)
        proj = lambda c0, c1: jnp.dot(h, w_ref[:, c0:c1], preferred_element_type=F32)
        stage(0, proj(0, rg_w))
        for c in range(0, rg_w, LANES):
            p_ref[rows, c:c + LANES] = conv(c // LANES, cwr[:, c:c + LANES]) + cbr_ref[:, c:c + LANES]
        p_ref[rows, rg_w:2 * rg_w] = proj(rg_w, 2 * rg_w)[mid]
        for c0 in range(0, qkv_w, blk):
            slot = 1 + c0 // blk
            stage(slot, proj(2 * rg_w + c0, 2 * rg_w + c0 + blk))
            for c in range(c0, c0 + blk, LANES):
                y = conv((blk + c) // LANES, cwq_ref[:, c:c + LANES])
                y = y * _sigmoid(y)
                if c < n_qk * LANES:
                    scale = q_scale if c < n_qk * LANES // 2 else 1.0
                    y = y * (lax.rsqrt(jnp.sum(y * y, axis=-1, keepdims=True) + EPS) * scale)
                p_ref[rows, 2 * rg_w + c:2 * rg_w + c + LANES] = y
        tail = proj(2 * rg_w + qkv_w, n + LANES)[mid]
        p_ref[rows, 2 * rg_w + qkv_w:n] = tail[:, :n - 2 * rg_w - qkv_w]
        ba_ref[rows, :] = tail[:, n - 2 * rg_w - qkv_w:]


def _in_proj(x, g, w_in_t, rg_conv_w, rg_conv_b, qkv_conv_w, *, n, seq, tm=512, n_sub=2):
    m, d = x.shape
    rg_w, qkv_w = rg_conv_w.shape[1], qkv_conv_w.shape[1]
    assert n % LANES == 0 and n < w_in_t.shape[0] <= n + LANES and w_in_t.shape[0] % SUBLANES == 0
    assert m % tm == 0 and tm % n_sub == 0 and seq % tm == 0
    assert rg_w == GDN_HEADS * HEAD_DIM and qkv_w % rg_w == 0
    hb = tm // SUBLANES
    const_spec = lambda shape, **kw: pl.BlockSpec(shape, lambda i: (0, 0), **kw)
    return pl.pallas_call(
        functools.partial(_inproj_kernel, n_sub=n_sub, tiles_per_seq=seq // tm, rg_w=rg_w, qkv_w=qkv_w,
                          n_qk=2 * GDN_HEADS, q_scale=HEAD_DIM ** -0.5),
        grid=(m // tm,),
        in_specs=[
            pl.BlockSpec((tm, d), lambda i: (i, 0)),
            pl.BlockSpec((SUBLANES, d), lambda i: (jnp.maximum(i * hb - 1, 0), 0)),
            pl.BlockSpec((SUBLANES, d), lambda i: (jnp.minimum((i + 1) * hb, m // SUBLANES - 1), 0)),
            const_spec((1, d)),
            const_spec(w_in_t.shape, pipeline_mode=pl.Buffered(1)),
            const_spec((4, rg_w)),
            const_spec((1, rg_w)),
            const_spec((4, qkv_w)),
        ],
        out_specs=[
            pl.BlockSpec((tm, n), lambda i: (i, 0)),
            pl.BlockSpec((tm, LANES), lambda i: (i, 0)),
        ],
        out_shape=[jax.ShapeDtypeStruct((m, n), F32), jax.ShapeDtypeStruct((m, LANES), F32)],
        scratch_shapes=[pltpu.VMEM(((rg_w + qkv_w) // LANES, tm // n_sub + 2 * SUBLANES, LANES), F32),
                        pltpu.VMEM((d, n + LANES), BF16)],
        compiler_params=_cparams(("arbitrary",)),
        name="in_proj",
    )(x, x, x, g.reshape(1, d), w_in_t, rg_conv_w, rg_conv_b.reshape(1, -1), qkv_conv_w)


RG_SEGS = 2 * SUBLANES
RG_PAD = 4


def _rglru_kernel(x_ref, gate_ref, wg_ref, bg_ref, lam_ref, o_ref,
                  af_scr, bf_scr, ab_scr, bb_scr, *, seq, rt):
    seg_len = seq // RG_SEGS
    pitch = seg_len + RG_PAD
    tiles_per_seg = seg_len // rt
    n_tiles = seq // rt
    a_scr = (af_scr, ab_scr)
    b_scr = (bf_scr, bb_scr)

    bg = bg_ref[...]
    half_log2_a = (-0.5 * RG_C * LOG2_E) * _softplus(-lam_ref[...])

    def gates(i, _):
        r0 = pl.multiple_of(i * rt, rt)
        xc = x_ref[pl.ds(r0, rt), :]
        pre = jnp.dot(xc.astype(BF16), wg_ref[...], preferred_element_type=F32) + bg
        hx = 0.5 * xc
        seg = i // tiles_per_seg
        off = seg * pitch + (i - seg * tiles_per_seg) * rt
        for d in range(2):
            tr = jnp.tanh(pre[:, (2 * d) * LANES:(2 * d + 1) * LANES])
            ti = jnp.tanh(pre[:, (2 * d + 1) * LANES:(2 * d + 2) * LANES])
            hl = half_log2_a[d:d + 1, :]
            a = jnp.exp2(hl + hl * tr)
            om = 1.0 - a * a
            b = (om * lax.rsqrt(jnp.maximum(om, RSQRT_FLOOR))) * (hx + hx * ti)
            a_scr[d][pl.ds(off, rt), :] = a
            b_scr[d][pl.ds(off, rt), :] = b
        return 0

    lax.fori_loop(0, n_tiles, gates, 0, unroll=4)

    def scan(t, carry):
        hf, pf, hb, pb = carry
        idx = pl.ds(t, RG_SEGS, stride=pitch)
        a = af_scr[idx, :]
        hf = a * hf + bf_scr[idx, :]
        pf = a * pf
        bf_scr[idx, :] = hf
        af_scr[idx, :] = pf
        idx = pl.ds(seg_len - 1 - t, RG_SEGS, stride=pitch)
        a = ab_scr[idx, :]
        hb = a * hb + bb_scr[idx, :]
        pb = a * pb
        bb_scr[idx, :] = hb
        ab_scr[idx, :] = pb
        return hf, pf, hb, pb

    zeros = jnp.zeros((RG_SEGS, LANES), F32)
    ones = jnp.ones((RG_SEGS, LANES), F32)
    hf, pf, hb, pb = lax.fori_loop(0, seg_len, scan, (zeros, ones, zeros, ones), unroll=8)

    cf = [jnp.zeros((1, LANES), F32)]
    for s in range(1, RG_SEGS):
        cf.append(hf[s - 1:s, :] + pf[s - 1:s, :] * cf[s - 1])
    cbk = [None] * RG_SEGS
    cbk[RG_SEGS - 1] = jnp.zeros((1, LANES), F32)
    for s in range(RG_SEGS - 2, -1, -1):
        cbk[s] = hb[s + 1:s + 2, :] + pb[s + 1:s + 2, :] * cbk[s + 1]

    for s in range(RG_SEGS):
        def fix(k, _, s=s):
            off = s * pitch + k * rt
            rows = pl.multiple_of(s * seg_len + k * rt, SUBLANES)
            h = bf_scr[pl.ds(off, rt), :] + af_scr[pl.ds(off, rt), :] * cf[s]
            h = h + (bb_scr[pl.ds(off, rt), :] + ab_scr[pl.ds(off, rt), :] * cbk[s])
            o_ref[pl.ds(rows, rt), :] = h * jax.nn.gelu(gate_ref[pl.ds(rows, rt), :], approximate=True)
            return 0

        lax.fori_loop(0, tiles_per_seg, fix, 0)


def _rglru(p, w_gates, b_gates, lam, *, batch, seq, gate_col0, rt=512):
    ngrp = w_gates.shape[0]
    seg_len = seq // RG_SEGS
    rt = min(rt, seg_len)
    assert seq % RG_SEGS == 0 and seg_len % rt == 0
    scr = pltpu.VMEM((RG_SEGS * (seg_len + RG_PAD), LANES), F32)
    return pl.pallas_call(
        functools.partial(_rglru_kernel, seq=seq, rt=rt),
        grid=(batch, ngrp),
        in_specs=[
            pl.BlockSpec((seq, LANES), lambda b, c: (b, c)),
            pl.BlockSpec((seq, LANES), lambda b, c: (b, gate_col0 + c)),
            pl.BlockSpec((None, LANES, 4 * LANES), lambda b, c: (c, 0, 0)),
            pl.BlockSpec((None, 1, 4 * LANES), lambda b, c: (c, 0, 0)),
            pl.BlockSpec((2, LANES), lambda b, c: (0, c)),
        ],
        out_specs=pl.BlockSpec((seq, LANES), lambda b, c: (b, c)),
        out_shape=jax.ShapeDtypeStruct((batch * seq, ngrp * LANES), F32),
        scratch_shapes=[scr, scr, scr, scr],
        compiler_params=_cparams(("parallel", "parallel")),
        name="rglru",
    )(p, p, w_gates, b_gates, lam)


def _rg_gate_weights(wa, ba, wx, bx):
    ndir, nblk, blk, _ = wa.shape
    ngrp = nblk * blk // LANES
    per = LANES // blk

    def bd(w):
        w = w.reshape(ndir, ngrp, per, blk, blk)
        eye = jnp.eye(per, dtype=w.dtype)
        return jnp.einsum("dgpij,pq->dgpiqj", w, eye).reshape(ndir, ngrp, LANES, LANES)

    a, x = bd(wa), bd(wx)
    w = jnp.concatenate([a[0], x[0], a[1], x[1]], axis=-1)
    ba = ba.reshape(ndir, ngrp, 1, LANES)
    bx = bx.reshape(ndir, ngrp, 1, LANES)
    b = jnp.concatenate([ba[0], bx[0], ba[1], bx[1]], axis=-1)
    return (0.5 * w).astype(BF16), (0.5 * b).astype(F32)


def _gate_kernel(ba_ref, alog_ref, dtb_ref, col_ref, row_ref, *, tm):
    nh = GDN_HEADS
    ri = lax.broadcasted_iota(jnp.int32, (LANES, LANES), 0)
    ci = lax.broadcasted_iota(jnp.int32, (LANES, LANES), 1)
    same = (ri // CHUNK) == (ci // CHUNK)
    lower = jnp.where(same & (ri >= ci), 1.0, 0.0).astype(BF16)
    upper = jnp.where(same & (ri <= ci), 1.0, 0.0).astype(BF16)
    masks = jnp.concatenate([lower, upper], axis=0)
    lane = ci
    neg_a = -jnp.exp(alog_ref[...])
    dtb = dtb_ref[...]
    for k in range(tm // LANES):
        sl = slice(k * LANES, (k + 1) * LANES)
        raw = ba_ref[sl, :]
        beta = _sigmoid(raw)
        gk = neg_a * _softplus(raw + dtb)
        gk = jnp.where((lane >= 2 * nh) & (lane < 4 * nh), gk, 0.0)
        csum = jnp.zeros((2 * LANES, LANES), F32)
        rest = gk
        for _ in range(3):
            piece = rest.astype(BF16)
            csum = csum + jnp.dot(masks, piece, preferred_element_type=F32)
            rest = rest - piece.astype(F32)
        gc = jnp.where(lane < 3 * nh, csum[:LANES], csum[LANES:])
        col = jnp.where(lane < 2 * nh, beta, gc)
        col_ref[sl, :] = col
        colt = col.T
        for q in range(LANES // CHUNK):
            rows = [jnp.concatenate([colt[(2 + dr) * nh + h:(2 + dr) * nh + h + 1, q * CHUNK:(q + 1) * CHUNK]
                                     for h in range(nh)], axis=1) for dr in range(2)]
            rows.append(jnp.zeros((SUBLANES - 2, nh * CHUNK), F32))
            row_ref[k * (LANES // CHUNK) + q] = jnp.concatenate(rows, axis=0)


def _gate_prep(ba, a_log, dt_bias, *, tm=2048):
    m = ba.shape[0]
    nh = GDN_HEADS
    pad = lambda v: jnp.zeros((1, LANES), F32).at[0, 2 * nh:4 * nh].set(v.reshape(-1).astype(F32))
    return pl.pallas_call(
        functools.partial(_gate_kernel, tm=tm),
        grid=(m // tm,),
        in_specs=[
            pl.BlockSpec((tm, LANES), lambda i: (i, 0)),
            pl.BlockSpec((1, LANES), lambda i: (0, 0)),
            pl.BlockSpec((1, LANES), lambda i: (0, 0)),
        ],
        out_specs=[
            pl.BlockSpec((tm, LANES), lambda i: (i, 0)),
            pl.BlockSpec((tm // CHUNK, SUBLANES, nh * CHUNK), lambda i: (i, 0, 0)),
        ],
        out_shape=[jax.ShapeDtypeStruct((m, LANES), F32),
                   jax.ShapeDtypeStruct((m // CHUNK, SUBLANES, nh * CHUNK), F32)],
        compiler_params=_cparams(("parallel",)),
        name="gate_prep",
    )(ba, pad(a_log), pad(dt_bias))


def _bmm(a, b):
    return lax.dot_general(a.astype(BF16), b.astype(BF16), (((2,), (1,)), ((0,), (0,))),
                           preferred_element_type=F32)


def _bmm_nt(a, b):
    return lax.dot_general(a.astype(BF16), b.astype(BF16), (((2,), (2,)), ((0,), (0,))),
                           preferred_element_type=F32)


def _bmm_tn(a, b):
    return lax.dot_general(a.astype(BF16), b.astype(BF16), (((1,), (1,)), ((0,), (0,))),
                           preferred_element_type=F32)


def _unit_tri_inverse_levels(lm, ri, ci, bdmask, nb):
    def blockdiag(a):
        ab = a.astype(BF16)
        return jnp.where(bdmask, jnp.concatenate([ab] * nb, axis=1), jnp.zeros((), BF16))

    eye = jnp.where(ri == ci, 1.0, 0.0).astype(F32)
    x = eye - jnp.where((ri // 2) == (ci // 2), lm, 0.0)
    s = 2
    while s < CHUNK:
        cm = jnp.where(((ri // (2 * s)) == (ci // (2 * s))) & ((ri // s) != (ci // s)), lm, 0.0)
        x = x - _bmm(_bmm(x, blockdiag(cm)), blockdiag(x))
        yield x
        s *= 2


def _gdn_chunk_local(q_ref, k_ref, v_ref, col_ref, row_ref, dst, *, reverse, g, d):
    u_scr, wq_scr, kd_scr, at_scr, cd_scr = dst
    nh, c, hd = GDN_HEADS, CHUNK, HEAD_DIM
    pw = nh * c
    fw = nh * hd
    ri = lax.broadcasted_iota(jnp.int32, (1, c, pw), 1)
    ci = lax.broadcasted_iota(jnp.int32, (1, c, pw), 2) % c
    if reverse:
        incl, strict, last = ri <= ci, ri < ci, 0
    else:
        incl, strict, last = ri >= ci, ri > ci, c - 1
    bdmask = (lax.broadcasted_iota(jnp.int32, (1, pw, pw), 1) // c
              == lax.broadcasted_iota(jnp.int32, (1, pw, pw), 2) // c)
    kmask = (lax.broadcasted_iota(jnp.int32, (1, pw, fw), 1) // c
             == lax.broadcasted_iota(jnp.int32, (1, pw, fw), 2) // hd)
    low_half = lax.broadcasted_iota(jnp.int32, (g * c, hd), 1) < c

    col = col_ref[...]
    bcast = lambda j: jnp.broadcast_to(col[:, j:j + 1], (g * c, hd))
    beta = jnp.concatenate([bcast(d * nh + h) for h in range(nh)], axis=1).reshape(g, c, fw)
    gcs = [bcast((2 + d) * nh + h) for h in range(nh)]
    gc = jnp.concatenate(gcs, axis=1).reshape(g, c, fw)
    gc_col = jnp.concatenate([jnp.where(low_half, gcs[h], gcs[h + 1]) for h in range(0, nh, 2)],
                             axis=1).reshape(g, c, pw)
    gc_row = row_ref[:, d:d + 1, :]
    g_last = gc[:, last:last + 1, :]
    eg = jnp.exp(gc)
    decay = jnp.where(incl, jnp.exp(gc_col - gc_row), 0.0)
    q = q_ref[...].reshape(g, c, fw)
    k = k_ref[...].reshape(g, c, fw)
    v = v_ref[...].reshape(g, c, fw)
    k_beta = k * beta
    v_beta = v * beta
    kb16 = k.astype(BF16)
    k_bd = jnp.where(kmask, jnp.concatenate([kb16] * nh, axis=1), jnp.zeros((), BF16))
    kq = _bmm_nt(jnp.concatenate([k_beta, q], axis=1), k_bd)
    lm = jnp.where(strict, kq[:, :c] * decay, 0.0)
    attn = kq[:, c:] * decay
    yield
    for t in _unit_tri_inverse_levels(lm, ri, ci, bdmask, nh):
        yield
    kbe = k_beta * eg
    qd = q * eg
    kd = k * jnp.exp(g_last - gc)
    cd = jnp.exp(g_last)
    for h in range(nh):
        hs = slice(h * hd, (h + 1) * hd)
        ps = slice(h * c, (h + 1) * c)
        uw = _bmm(t[:, :, ps], jnp.concatenate([v_beta[:, :, hs], kbe[:, :, hs]], axis=-1))
        u_scr[h] = uw[:, :, :hd]
        wq_scr[h] = jnp.concatenate([uw[:, :, hd:], qd[:, :, hs]], axis=1).astype(BF16)
        at_scr[h] = attn[:, :, ps].astype(BF16)
        kd_scr[h] = kd[:, :, hs].astype(BF16)
        cd_scr[h] = cd[:, :, hs]
        if h % 2 == 1:
            yield


def _gdn_recurrence(src, s_scr, o_ref, *, reverse, g):
    u_scr, wq_scr, kd_scr, at_scr, cd_scr = src
    nh, c, hd = GDN_HEADS, CHUNK, HEAD_DIM
    for step in range(g):
        n = g - 1 - step if reverse else step
        state = s_scr[...]
        sb = state.astype(BF16)
        ws = _bmm(wq_scr[:, n], sb)
        v_new = u_scr[:, n] - ws[:, :c]
        vb = v_new.astype(BF16)
        o = ws[:, c:] + _bmm(at_scr[:, n], vb)
        s_scr[...] = state * cd_scr[:, n] + _bmm_tn(kd_scr[:, n], vb)
        for h in range(nh):
            o_ref[n * c:(n + 1) * c, h * hd:(h + 1) * hd] = o[h]
        yield


def _gdn_kernel(q_ref, k_ref, v_ref, col_ref, row_ref, o_ref, s_scr, *sets, reverse, n_chunks, n_tiles, d):
    set0, set1 = sets[:len(sets) // 2], sets[len(sets) // 2:]
    i = pl.program_id(1)

    @pl.when(i == 0)
    def _():
        s_scr[...] = jnp.zeros_like(s_scr)
        for r in set1:
            r[...] = jnp.zeros_like(r)

    def step(dst, src):
        local = _gdn_chunk_local(q_ref, k_ref, v_ref, col_ref, row_ref, dst, reverse=reverse, g=n_chunks, d=d)
        recur = _gdn_recurrence(src, s_scr, o_ref, reverse=reverse, g=n_chunks)
        for _ in itertools.zip_longest(local, recur):
            pass

    last = pl.num_programs(1) - 1

    @pl.when((i % 2 == 0) & (i < last))
    def _():
        step(set0, set1)

    @pl.when((i % 2 == 1) & (i < last))
    def _():
        step(set1, set0)

    @pl.when(i == last)
    def _():
        src = set0 if n_tiles % 2 == 1 else set1
        for _ in _gdn_recurrence(src, s_scr, o_ref, reverse=reverse, g=n_chunks):
            pass


def _gdn(p, col, row, *, q_col, batch, seq, d, tt=512):
    nh = GDN_HEADS
    width = nh * HEAD_DIM
    nt = seq // tt
    n_chunks = tt // CHUNK
    reverse = d == 1
    order = (lambda t: nt - 1 - t) if reverse else (lambda t: t)
    tile_in = lambda b, i: b * nt + order(jnp.minimum(i, nt - 1))
    tile_out = lambda b, i: b * nt + order(jnp.maximum(i - 1, 0))
    scratch_set = [
        pltpu.VMEM((nh, n_chunks, CHUNK, HEAD_DIM), F32),
        pltpu.VMEM((nh, n_chunks, 2 * CHUNK, HEAD_DIM), BF16),
        pltpu.VMEM((nh, n_chunks, CHUNK, HEAD_DIM), BF16),
        pltpu.VMEM((nh, n_chunks, CHUNK, CHUNK), BF16),
        pltpu.VMEM((nh, n_chunks, 1, HEAD_DIM), F32),
    ]
    return pl.pallas_call(
        functools.partial(_gdn_kernel, reverse=reverse, n_chunks=n_chunks, n_tiles=nt, d=d),
        grid=(batch, nt + 1),
        in_specs=[
            pl.BlockSpec((tt, width), lambda b, i: (tile_in(b, i), q_col)),
            pl.BlockSpec((tt, width), lambda b, i: (tile_in(b, i), q_col + 1)),
            pl.BlockSpec((tt, width), lambda b, i: (tile_in(b, i), q_col + 2)),
            pl.BlockSpec((tt, LANES), lambda b, i: (tile_in(b, i), 0)),
            pl.BlockSpec((n_chunks, SUBLANES, nh * CHUNK), lambda b, i: (tile_in(b, i), 0, 0)),
        ],
        out_specs=pl.BlockSpec((tt, width), lambda b, i: (tile_out(b, i), 0)),
        out_shape=jax.ShapeDtypeStruct((batch * seq, width), F32),
        scratch_shapes=[pltpu.VMEM((nh, HEAD_DIM, HEAD_DIM), F32)] + scratch_set + scratch_set,
        compiler_params=_cparams(("parallel", "arbitrary")),
        name="gdn_bwd" if reverse else "gdn_fwd",
    )(p, p, p, col, row)


def _mixer(x1, batch, seq, mix_norm, w_in, w_out, rg_conv_w, rg_conv_b, rg_gate_a_w, rg_gate_a_b,
           rg_gate_x_w, rg_gate_x_b, rg_lambda, gdn_conv_w, gdn_a_log, gdn_dt_bias, gdn_norm):
    rg_w = rg_conv_w.shape[1]
    qkv_w = gdn_conv_w.shape[1]
    gdn_vw = GDN_HEADS * HEAD_DIM
    n_main = 2 * rg_w + qkv_w + gdn_vw
    p, ba = _in_proj(x1, mix_norm, w_in.T, rg_conv_w, rg_conv_b, gdn_conv_w, n=n_main, seq=seq)

    w_gates, b_gates = _rg_gate_weights(rg_gate_a_w, rg_gate_a_b, rg_gate_x_w, rg_gate_x_b)
    y_rg = _rglru(p, w_gates, b_gates, rg_lambda.astype(F32), batch=batch, seq=seq, gate_col0=rg_w // LANES)

    col, row = _gate_prep(ba, gdn_a_log, gdn_dt_bias)
    q_col = 2 * rg_w // gdn_vw
    o_f = _gdn(p, col, row, q_col=q_col, batch=batch, seq=seq, d=0)
    o_b = _gdn(p, col, row, q_col=q_col, batch=batch, seq=seq, d=1)
    return y_rg, o_f, o_b, p, (2 * rg_w + qkv_w) // gdn_vw, gdn_norm, w_out.astype(BF16)


def kernel(x, ffn1_norm, ffn1_w_gate, ffn1_w_up, ffn1_w_down, mix_norm, w_in, w_out, rg_conv_w, rg_conv_b, rg_gate_a_w, rg_gate_a_b, rg_gate_x_w, rg_gate_x_b, rg_lambda, gdn_conv_w, gdn_a_log, gdn_dt_bias, gdn_norm, ffn2_norm, ffn2_w_gate, ffn2_w_up, ffn2_w_down, final_norm):
    batch, seq, d_model = x.shape
    depth = ffn1_norm.shape[0]
    h = x.reshape(batch * seq, d_model)
    for l in range(depth):
        last = l == depth - 1
        h = _ffn(h, ffn1_norm[l], ffn1_w_gate[l], ffn1_w_up[l], ffn1_w_down[l], final_norm, final_norm=False,
                 tm=1024, n_sub=4)
        mixer = _mixer(h, batch, seq, mix_norm[l], w_in[l], w_out[l], rg_conv_w[l], rg_conv_b[l],
                   rg_gate_a_w[l], rg_gate_a_b[l], rg_gate_x_w[l], rg_gate_x_b[l], rg_lambda[l],
                   gdn_conv_w[l], gdn_a_log[l], gdn_dt_bias[l], gdn_norm[l])
        h = _ffn(h, ffn2_norm[l], ffn2_w_gate[l], ffn2_w_up[l], ffn2_w_down[l], final_norm,
                 final_norm=last, mixer=mixer)
    return h.reshape(batch, seq, d_model)
```

```python
import functools
import itertools

import jax
import jax.numpy as jnp
from jax import lax
from jax.experimental import pallas as pl
from jax.experimental.pallas import tpu as pltpu

F32 = jnp.float32
BF16 = jnp.bfloat16

EPS = 1e-6
LOG2_E = 1.4426950408889634
RSQRT_FLOOR = 1e-30
RG_C = 8.0
GDN_HEADS = 4
HEAD_DIM = 128
CHUNK = 64
LANES = 128
SUBLANES = 8
VMEM_LIMIT = 56 * 1024 * 1024


def _cparams(sem):
    return pltpu.CompilerParams(dimension_semantics=sem, vmem_limit_bytes=VMEM_LIMIT)


def _rms(x):
    return x * lax.rsqrt(jnp.mean(x * x, axis=-1, keepdims=True) + EPS)


def _sigmoid(x):
    return 1.0 / (1.0 + jnp.exp(-x))


def _softplus(x):
    return jnp.maximum(x, 0.0) + jnp.log(1.0 + jnp.exp(-jnp.abs(x)))


def _gated_mixer_rows(yrg_ref, of_ref, ob_ref, z_ref, gn_ref, wo_ref, rows):
    o = of_ref[rows, :] + ob_ref[rows, :]
    z = z_ref[rows, :]
    gn = gn_ref[...]
    parts = [yrg_ref[rows, :].astype(BF16)]
    for h in range(GDN_HEADS):
        hs = slice(h * HEAD_DIM, (h + 1) * HEAD_DIM)
        zh = z[:, hs]
        parts.append((_rms(o[:, hs]) * gn * (zh * _sigmoid(zh))).astype(BF16))
    return jnp.dot(jnp.concatenate(parts, axis=-1), wo_ref[...], preferred_element_type=F32)


W_CHUNKS = 8


def _stage_weight_bf16(w_hbm, w_scr, stage, sem):
    rows = w_hbm.shape[0] // W_CHUNKS
    copy = lambda k: pltpu.make_async_copy(w_hbm.at[pl.ds(k * rows, rows), :], stage.at[k % 2], sem.at[k % 2])
    copy(0).start()
    for k in range(W_CHUNKS):
        if k + 1 < W_CHUNKS:
            copy(k + 1).start()
        copy(k).wait()
        w_scr[k * rows:(k + 1) * rows, :] = stage[k % 2].astype(BF16)


def _ffn_kernel(*refs, final_norm, n_sub, with_mixer):
    refs, (wg_ref, wu_ref, wd_ref, stage_in, stage_dn, sem) = refs[:-6], refs[-6:]
    if with_mixer:
        x_ref, mixer_refs, (g_ref, wg_hbm, wu_hbm, wd_hbm, fg_ref, o_ref) = refs[0], refs[1:7], refs[7:]
    else:
        x_ref, g_ref, wg_hbm, wu_hbm, wd_hbm, fg_ref, o_ref = refs

    @pl.when(pl.program_id(0) == 0)
    def _():
        _stage_weight_bf16(wg_hbm, wg_ref, stage_in, sem)
        _stage_weight_bf16(wu_hbm, wu_ref, stage_in, sem)
        _stage_weight_bf16(wd_hbm, wd_ref, stage_dn, sem)

    sub = x_ref.shape[0] // n_sub
    g = g_ref[...]
    for r in range(n_sub):
        rows = slice(r * sub, (r + 1) * sub)
        x = x_ref[rows, :]
        if with_mixer:
            x = x + _gated_mixer_rows(*mixer_refs, rows)
        h = (_rms(x) * g).astype(BF16)
        gate = jnp.dot(h, wg_ref[...], preferred_element_type=F32)
        up = jnp.dot(h, wu_ref[...], preferred_element_type=F32)
        act = (gate * _sigmoid(gate) * up).astype(BF16)
        y = x + 0.5 * jnp.dot(act, wd_ref[...], preferred_element_type=F32)
        if final_norm:
            y = _rms(y) * fg_ref[...]
        o_ref[rows, :] = y


def _ffn(x, g, wg, wu, wd, fg, *, final_norm, mixer=None, tm=512, n_sub=2):
    m, d = x.shape
    f = wg.shape[1]
    assert m % tm == 0 and tm % n_sub == 0
    resident = dict(pipeline_mode=pl.Buffered(1))
    row_spec = lambda w, col=0: pl.BlockSpec((tm, w), lambda i: (i, col))
    const_spec = lambda shape, **kw: pl.BlockSpec(shape, lambda i: (0, 0), **kw)
    operands, in_specs = [x], [row_spec(d)]
    if mixer is not None:
        y_rg, o_f, o_b, p, z_col, gdn_norm, w_out = mixer
        wr, wv = y_rg.shape[1], o_f.shape[1]
        operands += [y_rg, o_f, o_b, p, gdn_norm.reshape(1, -1), w_out]
        in_specs += [row_spec(wr), row_spec(wv), row_spec(wv), row_spec(wv, z_col),
                     const_spec((1, HEAD_DIM)), const_spec((wr + wv, d), **resident)]
    assert d % W_CHUNKS == 0 and f % (W_CHUNKS * 2 * SUBLANES) == 0
    in_hbm = pl.BlockSpec(memory_space=pl.ANY)
    operands += [g.reshape(1, d), wg, wu, wd, fg.reshape(1, d)]
    in_specs += [const_spec((1, d)), in_hbm, in_hbm, in_hbm, const_spec((1, d))]
    return pl.pallas_call(
        functools.partial(_ffn_kernel, final_norm=final_norm, n_sub=n_sub, with_mixer=mixer is not None),
        grid=(m // tm,),
        in_specs=in_specs,
        out_specs=row_spec(d),
        out_shape=jax.ShapeDtypeStruct((m, d), F32),
        scratch_shapes=[
            pltpu.VMEM((d, f), BF16), pltpu.VMEM((d, f), BF16), pltpu.VMEM((f, d), BF16),
            pltpu.VMEM((2, d // W_CHUNKS, f), F32), pltpu.VMEM((2, f // W_CHUNKS, d), F32),
            pltpu.SemaphoreType.DMA((2,)),
        ],
        compiler_params=_cparams(("arbitrary",)),
        name="ffn_final" if final_norm else "ffn",
    )(*operands)


def _inproj_kernel(x_ref, xp_ref, xn_ref, g_ref, w32_ref, cwr_ref, cbr_ref, cwq_ref, p_ref, ba_ref,
                   stage_scr, w_ref, *, n_sub, tiles_per_seq, rg_w, qkv_w, n_qk, q_scale):
    tm, n = p_ref.shape
    sub = tm // n_sub
    hal = SUBLANES
    i = pl.program_id(0)

    @pl.when(i == 0)
    def _():
        n_in = w32_ref.shape[0]
        for c in range(0, n, LANES):
            w_ref[:, c:c + LANES] = w32_ref[c:c + LANES, :].T.astype(BF16)
        tail = w32_ref[n_in - LANES:n_in, :].T
        w_ref[:, n:] = jnp.zeros((w_ref.shape[0], LANES), BF16)
        w_ref[:, n:n_in] = tail[:, LANES - (n_in - n):].astype(BF16)

    pos = i % tiles_per_seq
    prev = jnp.where(pos == 0, 0.0, xp_ref[...])
    nxt = jnp.where(pos == tiles_per_seq - 1, 0.0, xn_ref[...])
    g = g_ref[...]
    cwr = cwr_ref[...]
    mid = slice(hal, hal + sub)
    blk = GDN_HEADS * HEAD_DIM

    def stage(slot, pw):
        for j in range(blk // LANES):
            stage_scr[slot * (blk // LANES) + j] = pw[:, j * LANES:(j + 1) * LANES]

    def conv(slab, cw):
        tap = lambda off: stage_scr[slab, hal + off:hal + off + sub, :]
        acc = cw[0:1, :] * tap(-2) + cw[1:2, :] * tap(-1)
        return acc + cw[2:3, :] * tap(0) + cw[3:4, :] * tap(1)

    for r in range(n_sub):
        lo, hi = r * sub - hal, (r + 1) * sub + hal
        parts = ([prev] if lo < 0 else []) + [x_ref[max(lo, 0):min(hi, tm), :]] + ([nxt] if hi > tm else [])
        xw = jnp.concatenate(parts, axis=0)
        h = (_rms(xw) * g).astype(BF16)
        rows = slice(r * sub, (r + 1) * sub)
        proj = lambda c0, c1: jnp.dot(h, w_ref[:, c0:c1], preferred_element_type=F32)
        stage(0, proj(0, rg_w))
        for c in range(0, rg_w, LANES):
            p_ref[rows, c:c + LANES] = conv(c // LANES, cwr[:, c:c + LANES]) + cbr_ref[:, c:c + LANES]
        p_ref[rows, rg_w:2 * rg_w] = proj(rg_w, 2 * rg_w)[mid]
        for c0 in range(0, qkv_w, blk):
            slot = 1 + c0 // blk
            stage(slot, proj(2 * rg_w + c0, 2 * rg_w + c0 + blk))
            for c in range(c0, c0 + blk, LANES):
                y = conv((blk + c) // LANES, cwq_ref[:, c:c + LANES])
                y = y * _sigmoid(y)
                if c < n_qk * LANES:
                    scale = q_scale if c < n_qk * LANES // 2 else 1.0
                    y = y * (lax.rsqrt(jnp.sum(y * y, axis=-1, keepdims=True) + EPS) * scale)
                p_ref[rows, 2 * rg_w + c:2 * rg_w + c + LANES] = y
        tail = proj(2 * rg_w + qkv_w, n + LANES)[mid]
        p_ref[rows, 2 * rg_w + qkv_w:n] = tail[:, :n - 2 * rg_w - qkv_w]
        ba_ref[rows, :] = tail[:, n - 2 * rg_w - qkv_w:]


def _in_proj(x, g, w_in_t, rg_conv_w, rg_conv_b, qkv_conv_w, *, n, seq, tm=512, n_sub=2):
    m, d = x.shape
    rg_w, qkv_w = rg_conv_w.shape[1], qkv_conv_w.shape[1]
    assert n % LANES == 0 and n < w_in_t.shape[0] <= n + LANES and w_in_t.shape[0] % SUBLANES == 0
    assert m % tm == 0 and tm % n_sub == 0 and seq % tm == 0
    assert rg_w == GDN_HEADS * HEAD_DIM and qkv_w % rg_w == 0
    hb = tm // SUBLANES
    const_spec = lambda shape, **kw: pl.BlockSpec(shape, lambda i: (0, 0), **kw)
    return pl.pallas_call(
        functools.partial(_inproj_kernel, n_sub=n_sub, tiles_per_seq=seq // tm, rg_w=rg_w, qkv_w=qkv_w,
                          n_qk=2 * GDN_HEADS, q_scale=HEAD_DIM ** -0.5),
        grid=(m // tm,),
        in_specs=[
            pl.BlockSpec((tm, d), lambda i: (i, 0)),
            pl.BlockSpec((SUBLANES, d), lambda i: (jnp.maximum(i * hb - 1, 0), 0)),
            pl.BlockSpec((SUBLANES, d), lambda i: (jnp.minimum((i + 1) * hb, m // SUBLANES - 1), 0)),
            const_spec((1, d)),
            const_spec(w_in_t.shape, pipeline_mode=pl.Buffered(1)),
            const_spec((4, rg_w)),
            const_spec((1, rg_w)),
            const_spec((4, qkv_w)),
        ],
        out_specs=[
            pl.BlockSpec((tm, n), lambda i: (i, 0)),
            pl.BlockSpec((tm, LANES), lambda i: (i, 0)),
        ],
        out_shape=[jax.ShapeDtypeStruct((m, n), F32), jax.ShapeDtypeStruct((m, LANES), F32)],
        scratch_shapes=[pltpu.VMEM(((rg_w + qkv_w) // LANES, tm // n_sub + 2 * SUBLANES, LANES), F32),
                        pltpu.VMEM((d, n + LANES), BF16)],
        compiler_params=_cparams(("arbitrary",)),
        name="in_proj",
    )(x, x, x, g.reshape(1, d), w_in_t, rg_conv_w, rg_conv_b.reshape(1, -1), qkv_conv_w)


RG_SEGS = 2 * SUBLANES
RG_PAD = 4


def _rglru_kernel(x_ref, gate_ref, wg_ref, bg_ref, lam_ref, o_ref,
                  af_scr, bf_scr, ab_scr, bb_scr, *, seq, rt):
    seg_len = seq // RG_SEGS
    pitch = seg_len + RG_PAD
    tiles_per_seg = seg_len // rt
    n_tiles = seq // rt
    a_scr = (af_scr, ab_scr)
    b_scr = (bf_scr, bb_scr)

    bg = bg_ref[...]
    half_log2_a = (-0.5 * RG_C * LOG2_E) * _softplus(-lam_ref[...])

    def gates(i, _):
        r0 = pl.multiple_of(i * rt, rt)
        xc = x_ref[pl.ds(r0, rt), :]
        pre = jnp.dot(xc.astype(BF16), wg_ref[...], preferred_element_type=F32) + bg
        hx = 0.5 * xc
        seg = i // tiles_per_seg
        off = seg * pitch + (i - seg * tiles_per_seg) * rt
        for d in range(2):
            tr = jnp.tanh(pre[:, (2 * d) * LANES:(2 * d + 1) * LANES])
            ti = jnp.tanh(pre[:, (2 * d + 1) * LANES:(2 * d + 2) * LANES])
            hl = half_log2_a[d:d + 1, :]
            a = jnp.exp2(hl + hl * tr)
            om = 1.0 - a * a
            b = (om * lax.rsqrt(jnp.maximum(om, RSQRT_FLOOR))) * (hx + hx * ti)
            a_scr[d][pl.ds(off, rt), :] = a
            b_scr[d][pl.ds(off, rt), :] = b
        return 0

    lax.fori_loop(0, n_tiles, gates, 0, unroll=4)

    def scan(t, carry):
        hf, pf, hb, pb = carry
        idx = pl.ds(t, RG_SEGS, stride=pitch)
        a = af_scr[idx, :]
        hf = a * hf + bf_scr[idx, :]
        pf = a * pf
        bf_scr[idx, :] = hf
        af_scr[idx, :] = pf
        idx = pl.ds(seg_len - 1 - t, RG_SEGS, stride=pitch)
        a = ab_scr[idx, :]
        hb = a * hb + bb_scr[idx, :]
        pb = a * pb
        bb_scr[idx, :] = hb
        ab_scr[idx, :] = pb
        return hf, pf, hb, pb

    zeros = jnp.zeros((RG_SEGS, LANES), F32)
    ones = jnp.ones((RG_SEGS, LANES), F32)
    hf, pf, hb, pb = lax.fori_loop(0, seg_len, scan, (zeros, ones, zeros, ones), unroll=8)

    cf = [jnp.zeros((1, LANES), F32)]
    for s in range(1, RG_SEGS):
        cf.append(hf[s - 1:s, :] + pf[s - 1:s, :] * cf[s - 1])
    cbk = [None] * RG_SEGS
    cbk[RG_SEGS - 1] = jnp.zeros((1, LANES), F32)
    for s in range(RG_SEGS - 2, -1, -1):
        cbk[s] = hb[s + 1:s + 2, :] + pb[s + 1:s + 2, :] * cbk[s + 1]

    for s in range(RG_SEGS):
        def fix(k, _, s=s):
            off = s * pitch + k * rt
            rows = pl.multiple_of(s * seg_len + k * rt, SUBLANES)
            h = bf_scr[pl.ds(off, rt), :] + af_scr[pl.ds(off, rt), :] * cf[s]
            h = h + (bb_scr[pl.ds(off, rt), :] + ab_scr[pl.ds(off, rt), :] * cbk[s])
            o_ref[pl.ds(rows, rt), :] = h * jax.nn.gelu(gate_ref[pl.ds(rows, rt), :], approximate=True)
            return 0

        lax.fori_loop(0, tiles_per_seg, fix, 0)


def _rglru(p, w_gates, b_gates, lam, *, batch, seq, gate_col0, rt=512):
    ngrp = w_gates.shape[0]
    seg_len = seq // RG_SEGS
    rt = min(rt, seg_len)
    assert seq % RG_SEGS == 0 and seg_len % rt == 0
    scr = pltpu.VMEM((RG_SEGS * (seg_len + RG_PAD), LANES), F32)
    return pl.pallas_call(
        functools.partial(_rglru_kernel, seq=seq, rt=rt),
        grid=(batch, ngrp),
        in_specs=[
            pl.BlockSpec((seq, LANES), lambda b, c: (b, c)),
            pl.BlockSpec((seq, LANES), lambda b, c: (b, gate_col0 + c)),
            pl.BlockSpec((None, LANES, 4 * LANES), lambda b, c: (c, 0, 0)),
            pl.BlockSpec((None, 1, 4 * LANES), lambda b, c: (c, 0, 0)),
            pl.BlockSpec((2, LANES), lambda b, c: (0, c)),
        ],
        out_specs=pl.BlockSpec((seq, LANES), lambda b, c: (b, c)),
        out_shape=jax.ShapeDtypeStruct((batch * seq, ngrp * LANES), F32),
        scratch_shapes=[scr, scr, scr, scr],
        compiler_params=_cparams(("parallel", "parallel")),
        name="rglru",
    )(p, p, w_gates, b_gates, lam)


def _rg_gate_weights(wa, ba, wx, bx):
    ndir, nblk, blk, _ = wa.shape
    ngrp = nblk * blk // LANES
    per = LANES // blk

    def bd(w):
        w = w.reshape(ndir, ngrp, per, blk, blk)
        eye = jnp.eye(per, dtype=w.dtype)
        return jnp.einsum("dgpij,pq->dgpiqj", w, eye).reshape(ndir, ngrp, LANES, LANES)

    a, x = bd(wa), bd(wx)
    w = jnp.concatenate([a[0], x[0], a[1], x[1]], axis=-1)
    ba = ba.reshape(ndir, ngrp, 1, LANES)
    bx = bx.reshape(ndir, ngrp, 1, LANES)
    b = jnp.concatenate([ba[0], bx[0], ba[1], bx[1]], axis=-1)
    return (0.5 * w).astype(BF16), (0.5 * b).astype(F32)


def _gate_kernel(ba_ref, alog_ref, dtb_ref, col_ref, row_ref, *, tm):
    nh = GDN_HEADS
    ri = lax.broadcasted_iota(jnp.int32, (LANES, LANES), 0)
    ci = lax.broadcasted_iota(jnp.int32, (LANES, LANES), 1)
    same = (ri // CHUNK) == (ci // CHUNK)
    lower = jnp.where(same & (ri >= ci), 1.0, 0.0).astype(BF16)
    upper = jnp.where(same & (ri <= ci), 1.0, 0.0).astype(BF16)
    masks = jnp.concatenate([lower, upper], axis=0)
    lane = ci
    neg_a = -jnp.exp(alog_ref[...])
    dtb = dtb_ref[...]
    for k in range(tm // LANES):
        sl = slice(k * LANES, (k + 1) * LANES)
        raw = ba_ref[sl, :]
        beta = _sigmoid(raw)
        gk = neg_a * _softplus(raw + dtb)
        gk = jnp.where((lane >= 2 * nh) & (lane < 4 * nh), gk, 0.0)
        csum = jnp.zeros((2 * LANES, LANES), F32)
        rest = gk
        for _ in range(3):
            piece = rest.astype(BF16)
            csum = csum + jnp.dot(masks, piece, preferred_element_type=F32)
            rest = rest - piece.astype(F32)
        gc = jnp.where(lane < 3 * nh, csum[:LANES], csum[LANES:])
        col = jnp.where(lane < 2 * nh, beta, gc)
        col_ref[sl, :] = col
        colt = col.T
        for q in range(LANES // CHUNK):
            rows = [jnp.concatenate([colt[(2 + dr) * nh + h:(2 + dr) * nh + h + 1, q * CHUNK:(q + 1) * CHUNK]
                                     for h in range(nh)], axis=1) for dr in range(2)]
            rows.append(jnp.zeros((SUBLANES - 2, nh * CHUNK), F32))
            row_ref[k * (LANES // CHUNK) + q] = jnp.concatenate(rows, axis=0)


def _gate_prep(ba, a_log, dt_bias, *, tm=2048):
    m = ba.shape[0]
    nh = GDN_HEADS
    pad = lambda v: jnp.zeros((1, LANES), F32).at[0, 2 * nh:4 * nh].set(v.reshape(-1).astype(F32))
    return pl.pallas_call(
        functools.partial(_gate_kernel, tm=tm),
        grid=(m // tm,),
        in_specs=[
            pl.BlockSpec((tm, LANES), lambda i: (i, 0)),
            pl.BlockSpec((1, LANES), lambda i: (0, 0)),
            pl.BlockSpec((1, LANES), lambda i: (0, 0)),
        ],
        out_specs=[
            pl.BlockSpec((tm, LANES), lambda i: (i, 0)),
            pl.BlockSpec((tm // CHUNK, SUBLANES, nh * CHUNK), lambda i: (i, 0, 0)),
        ],
        out_shape=[jax.ShapeDtypeStruct((m, LANES), F32),
                   jax.ShapeDtypeStruct((m // CHUNK, SUBLANES, nh * CHUNK), F32)],
        compiler_params=_cparams(("parallel",)),
        name="gate_prep",
    )(ba, pad(a_log), pad(dt_bias))


def _bmm(a, b):
    return lax.dot_general(a.astype(BF16), b.astype(BF16), (((2,), (1,)), ((0,), (0,))),
                           preferred_element_type=F32)


def _bmm_nt(a, b):
    return lax.dot_general(a.astype(BF16), b.astype(BF16), (((2,), (2,)), ((0,), (0,))),
                           preferred_element_type=F32)


def _bmm_tn(a, b):
    return lax.dot_general(a.astype(BF16), b.astype(BF16), (((1,), (1,)), ((0,), (0,))),
                           preferred_element_type=F32)


def _unit_tri_inverse_levels(lm, ri, ci, bdmask, nb):
    def blockdiag(a):
        ab = a.astype(BF16)
        return jnp.where(bdmask, jnp.concatenate([ab] * nb, axis=1), jnp.zeros((), BF16))

    eye = jnp.where(ri == ci, 1.0, 0.0).astype(F32)
    x = eye - jnp.where((ri // 2) == (ci // 2), lm, 0.0)
    s = 2
    while s < CHUNK:
        cm = jnp.where(((ri // (2 * s)) == (ci // (2 * s))) & ((ri // s) != (ci // s)), lm, 0.0)
        x = x - _bmm(_bmm(x, blockdiag(cm)), blockdiag(x))
        yield x
        s *= 2


def _gdn_chunk_local(q_ref, k_ref, v_ref, col_ref, row_ref, dst, *, reverse, g, d):
    u_scr, wq_scr, kd_scr, at_scr, cd_scr = dst
    nh, c, hd = GDN_HEADS, CHUNK, HEAD_DIM
    pw = nh * c
    fw = nh * hd
    ri = lax.broadcasted_iota(jnp.int32, (1, c, pw), 1)
    ci = lax.broadcasted_iota(jnp.int32, (1, c, pw), 2) % c
    if reverse:
        incl, strict, last = ri <= ci, ri < ci, 0
    else:
        incl, strict, last = ri >= ci, ri > ci, c - 1
    bdmask = (lax.broadcasted_iota(jnp.int32, (1, pw, pw), 1) // c
              == lax.broadcasted_iota(jnp.int32, (1, pw, pw), 2) // c)
    kmask = (lax.broadcasted_iota(jnp.int32, (1, pw, fw), 1) // c
             == lax.broadcasted_iota(jnp.int32, (1, pw, fw), 2) // hd)
    low_half = lax.broadcasted_iota(jnp.int32, (g * c, hd), 1) < c

    col = col_ref[...]
    bcast = lambda j: jnp.broadcast_to(col[:, j:j + 1], (g * c, hd))
    beta = jnp.concatenate([bcast(d * nh + h) for h in range(nh)], axis=1).reshape(g, c, fw)
    gcs = [bcast((2 + d) * nh + h) for h in range(nh)]
    gc = jnp.concatenate(gcs, axis=1).reshape(g, c, fw)
    gc_col = jnp.concatenate([jnp.where(low_half, gcs[h], gcs[h + 1]) for h in range(0, nh, 2)],
                             axis=1).reshape(g, c, pw)
    gc_row = row_ref[:, d:d + 1, :]
    g_last = gc[:, last:last + 1, :]
    eg = jnp.exp(gc)
    decay = jnp.where(incl, jnp.exp(gc_col - gc_row), 0.0)
    q = q_ref[...].reshape(g, c, fw)
    k = k_ref[...].reshape(g, c, fw)
    v = v_ref[...].reshape(g, c, fw)
    k_beta = k * beta
    v_beta = v * beta
    kb16 = k.astype(BF16)
    k_bd = jnp.where(kmask, jnp.concatenate([kb16] * nh, axis=1), jnp.zeros((), BF16))
    kq = _bmm_nt(jnp.concatenate([k_beta, q], axis=1), k_bd)
    lm = jnp.where(strict, kq[:, :c] * decay, 0.0)
    attn = kq[:, c:] * decay
    yield
    for t in _unit_tri_inverse_levels(lm, ri, ci, bdmask, nh):
        yield
    kbe = k_beta * eg
    qd = q * eg
    kd = k * jnp.exp(g_last - gc)
    cd = jnp.exp(g_last)
    for h in range(nh):
        hs = slice(h * hd, (h + 1) * hd)
        ps = slice(h * c, (h + 1) * c)
        uw = _bmm(t[:, :, ps], jnp.concatenate([v_beta[:, :, hs], kbe[:, :, hs]], axis=-1))
        u_scr[h] = uw[:, :, :hd]
        wq_scr[h] = jnp.concatenate([uw[:, :, hd:], qd[:, :, hs]], axis=1).astype(BF16)
        at_scr[h] = attn[:, :, ps].astype(BF16)
        kd_scr[h] = kd[:, :, hs].astype(BF16)
        cd_scr[h] = cd[:, :, hs]
        if h % 2 == 1:
            yield


def _gdn_recurrence(src, s_scr, o_ref, *, reverse, g):
    u_scr, wq_scr, kd_scr, at_scr, cd_scr = src
    nh, c, hd = GDN_HEADS, CHUNK, HEAD_DIM
    for step in range(g):
        n = g - 1 - step if reverse else step
        state = s_scr[...]
        sb = state.astype(BF16)
        ws = _bmm(wq_scr[:, n], sb)
        v_new = u_scr[:, n] - ws[:, :c]
        vb = v_new.astype(BF16)
        o = ws[:, c:] + _bmm(at_scr[:, n], vb)
        s_scr[...] = state * cd_scr[:, n] + _bmm_tn(kd_scr[:, n], vb)
        for h in range(nh):
            o_ref[n * c:(n + 1) * c, h * hd:(h + 1) * hd] = o[h]
        yield


def _gdn_kernel(q_ref, k_ref, v_ref, col_ref, row_ref, o_ref, s_scr, *sets, reverse, n_chunks, n_tiles, d):
    set0, set1 = sets[:len(sets) // 2], sets[len(sets) // 2:]
    i = pl.program_id(1)

    @pl.when(i == 0)
    def _():
        s_scr[...] = jnp.zeros_like(s_scr)
        for r in set1:
            r[...] = jnp.zeros_like(r)

    def step(dst, src):
        local = _gdn_chunk_local(q_ref, k_ref, v_ref, col_ref, row_ref, dst, reverse=reverse, g=n_chunks, d=d)
        recur = _gdn_recurrence(src, s_scr, o_ref, reverse=reverse, g=n_chunks)
        for _ in itertools.zip_longest(recur, local):
            pass

    last = pl.num_programs(1) - 1

    @pl.when((i % 2 == 0) & (i < last))
    def _():
        step(set0, set1)

    @pl.when((i % 2 == 1) & (i < last))
    def _():
        step(set1, set0)

    @pl.when(i == last)
    def _():
        src = set0 if n_tiles % 2 == 1 else set1
        for _ in _gdn_recurrence(src, s_scr, o_ref, reverse=reverse, g=n_chunks):
            pass


def _gdn(p, col, row, *, q_col, batch, seq, d, tt=512):
    nh = GDN_HEADS
    width = nh * HEAD_DIM
    nt = seq // tt
    n_chunks = tt // CHUNK
    reverse = d == 1
    order = (lambda t: nt - 1 - t) if reverse else (lambda t: t)
    tile_in = lambda b, i: b * nt + order(jnp.minimum(i, nt - 1))
    tile_out = lambda b, i: b * nt + order(jnp.maximum(i - 1, 0))
    scratch_set = [
        pltpu.VMEM((nh, n_chunks, CHUNK, HEAD_DIM), F32),
        pltpu.VMEM((nh, n_chunks, 2 * CHUNK, HEAD_DIM), BF16),
        pltpu.VMEM((nh, n_chunks, CHUNK, HEAD_DIM), BF16),
        pltpu.VMEM((nh, n_chunks, CHUNK, CHUNK), BF16),
        pltpu.VMEM((nh, n_chunks, 1, HEAD_DIM), F32),
    ]
    return pl.pallas_call(
        functools.partial(_gdn_kernel, reverse=reverse, n_chunks=n_chunks, n_tiles=nt, d=d),
        grid=(batch, nt + 1),
        in_specs=[
            pl.BlockSpec((tt, width), lambda b, i: (tile_in(b, i), q_col)),
            pl.BlockSpec((tt, width), lambda b, i: (tile_in(b, i), q_col + 1)),
            pl.BlockSpec((tt, width), lambda b, i: (tile_in(b, i), q_col + 2)),
            pl.BlockSpec((tt, LANES), lambda b, i: (tile_in(b, i), 0)),
            pl.BlockSpec((n_chunks, SUBLANES, nh * CHUNK), lambda b, i: (tile_in(b, i), 0, 0)),
        ],
        out_specs=pl.BlockSpec((tt, width), lambda b, i: (tile_out(b, i), 0)),
        out_shape=jax.ShapeDtypeStruct((batch * seq, width), F32),
        scratch_shapes=[pltpu.VMEM((nh, HEAD_DIM, HEAD_DIM), F32)] + scratch_set + scratch_set,
        compiler_params=_cparams(("parallel", "arbitrary")),
        name="gdn_bwd" if reverse else "gdn_fwd",
    )(p, p, p, col, row)


def _mixer(x1, batch, seq, mix_norm, w_in, w_out, rg_conv_w, rg_conv_b, rg_gate_a_w, rg_gate_a_b,
           rg_gate_x_w, rg_gate_x_b, rg_lambda, gdn_conv_w, gdn_a_log, gdn_dt_bias, gdn_norm):
    rg_w = rg_conv_w.shape[1]
    qkv_w = gdn_conv_w.shape[1]
    gdn_vw = GDN_HEADS * HEAD_DIM
    n_main = 2 * rg_w + qkv_w + gdn_vw
    p, ba = _in_proj(x1, mix_norm, w_in.T, rg_conv_w, rg_conv_b, gdn_conv_w, n=n_main, seq=seq)

    w_gates, b_gates = _rg_gate_weights(rg_gate_a_w, rg_gate_a_b, rg_gate_x_w, rg_gate_x_b)
    y_rg = _rglru(p, w_gates, b_gates, rg_lambda.astype(F32), batch=batch, seq=seq, gate_col0=rg_w // LANES)

    col, row = _gate_prep(ba, gdn_a_log, gdn_dt_bias)
    q_col = 2 * rg_w // gdn_vw
    o_f = _gdn(p, col, row, q_col=q_col, batch=batch, seq=seq, d=0)
    o_b = _gdn(p, col, row, q_col=q_col, batch=batch, seq=seq, d=1)
    return y_rg, o_f, o_b, p, (2 * rg_w + qkv_w) // gdn_vw, gdn_norm, w_out.astype(BF16)


def kernel(x, ffn1_norm, ffn1_w_gate, ffn1_w_up, ffn1_w_down, mix_norm, w_in, w_out, rg_conv_w, rg_conv_b, rg_gate_a_w, rg_gate_a_b, rg_gate_x_w, rg_gate_x_b, rg_lambda, gdn_conv_w, gdn_a_log, gdn_dt_bias, gdn_norm, ffn2_norm, ffn2_w_gate, ffn2_w_up, ffn2_w_down, final_norm):
    batch, seq, d_model = x.shape
    depth = ffn1_norm.shape[0]
    h = x.reshape(batch * seq, d_model)
    for l in range(depth):
        last = l == depth - 1
        h = _ffn(h, ffn1_norm[l], ffn1_w_gate[l], ffn1_w_up[l], ffn1_w_down[l], final_norm, final_norm=False,
                 tm=1024, n_sub=4)
        mixer = _mixer(h, batch, seq, mix_norm[l], w_in[l], w_out[l], rg_conv_w[l], rg_conv_b[l],
                   rg_gate_a_w[l], rg_gate_a_b[l], rg_gate_x_w[l], rg_gate_x_b[l], rg_lambda[l],
                   gdn_conv_w[l], gdn_a_log[l], gdn_dt_bias[l], gdn_norm[l])
        h = _ffn(h, ffn2_norm[l], ffn2_w_gate[l], ffn2_w_up[l], ffn2_w_down[l], final_norm,
                 final_norm=last, mixer=mixer)
    return h.reshape(batch, seq, d_model)
```
